```python
import jax, jax.numpy as jnp
from jax import lax
import numpy as np

D_MODEL = 1024
BATCH = 2
SEQ = 8192
DEPTH = 1

D_MIX = D_MODEL
D_CONV = D_MIX // 2
CONV_KERNEL = 31
D_GLA_V = D_MIX - D_CONV
D_GLA_K = D_GLA_V // 2
GLA_HEADS = 4
HEAD_K = D_GLA_K // GLA_HEADS
HEAD_V = D_GLA_V // GLA_HEADS
GATE_RANK = 16
GATE_TAU = 16.0
CHUNK = 64
D_FF = ((8 * D_MODEL + 3 * 256 - 1) // (3 * 256)) * 256
D_IN = 2 * D_CONV + 2 * D_GLA_K + 2 * D_GLA_V + GATE_RANK
EPS = 1e-6

kernel_name = "hybrid_conv_gla_adaln_block"


def rms_norm(x, g):
    xf = x.astype(jnp.float32)
    y = xf * lax.rsqrt(jnp.mean(xf * xf, axis=-1, keepdims=True) + EPS)
    return (y * g.astype(jnp.float32)).astype(x.dtype)


def layer_norm(x, g, b):
    xf = x.astype(jnp.float32)
    mu = jnp.mean(xf, axis=-1, keepdims=True)
    var = jnp.mean(jnp.square(xf - mu), axis=-1, keepdims=True)
    y = (xf - mu) * lax.rsqrt(var + EPS)
    return (y * g.astype(jnp.float32) + b.astype(jnp.float32)).astype(x.dtype)


def causal_depthwise_conv(u, w, bias):
    k = w.shape[0]
    out = lax.conv_general_dilated(
        u, w[:, None, :], window_strides=(1,), padding=[(k - 1, 0)],
        dimension_numbers=('NWC', 'WIO', 'NWC'), feature_group_count=u.shape[-1])
    return out + bias


def gla_chunked(q, k, v, log_a):
    out_dtype = v.dtype
    b_sz, t_len, n_h, dk = q.shape
    dv = v.shape[-1]
    n_chunks = t_len // CHUNK

    def to_chunks(t):
        return t.astype(jnp.float32).reshape(b_sz, n_chunks, CHUNK, n_h, t.shape[-1]).transpose(1, 0, 3, 2, 4)

    qc, kc, vc, ac = to_chunks(q), to_chunks(k), to_chunks(v), to_chunks(log_a)
    causal = jnp.tril(jnp.ones((CHUNK, CHUNK), dtype=bool))[:, :, None]

    def step(state, inp):
        qi, ki, vi, ai = inp
        cum = jnp.cumsum(ai, axis=2)
        o_inter = jnp.einsum('bhld,bhde->bhle', qi * jnp.exp(cum), state)
        diff = cum[:, :, :, None, :] - cum[:, :, None, :, :]
        decay = jnp.exp(jnp.where(causal, diff, -jnp.inf))
        scores = jnp.einsum('bhid,bhijd,bhjd->bhij', qi, decay, ki)
        o_intra = jnp.einsum('bhij,bhje->bhie', scores, vi)
        cum_last = cum[:, :, -1:, :]
        new_state = jnp.exp(cum_last[:, :, 0, :, None]) * state + jnp.einsum(
            'bhld,bhle->bhde', ki * jnp.exp(cum_last - cum), vi)
        return new_state, o_inter + o_intra

    s0 = jnp.zeros((b_sz, n_h, dk, dv), jnp.float32)
    _, o = lax.scan(step, s0, (qc, kc, vc, ac))
    return o.transpose(1, 0, 3, 2, 4).reshape(b_sz, t_len, n_h, dv).astype(out_dtype)


def setup_inputs(seed: int = 0) -> dict:
    key = jax.random.key(seed)
    ks = jax.random.split(key, 20)
    f32 = jnp.float32

    def nrm(k, shape, scale):
        return jax.random.normal(k, shape, f32) * scale

    return {
        "x": nrm(ks[0], (BATCH, SEQ, D_MODEL), 1.0),
        "c": nrm(ks[1], (BATCH, D_MODEL), 1.0),
        "w_ada": nrm(ks[2], (DEPTH, D_MODEL, 6 * D_MODEL), D_MODEL ** -0.5),
        "b_ada": nrm(ks[3], (DEPTH, 6 * D_MODEL), 0.02),
        "g_mix": 1.0 + nrm(ks[4], (DEPTH, D_MODEL), 0.02),
        "w_in": nrm(ks[5], (DEPTH, D_MODEL, D_IN), D_MODEL ** -0.5),
        "conv_w": nrm(ks[6], (DEPTH, CONV_KERNEL, D_CONV), CONV_KERNEL ** -0.5),
        "conv_b": nrm(ks[7], (DEPTH, D_CONV), 0.02),
        "ln_g": 1.0 + nrm(ks[8], (DEPTH, D_CONV), 0.02),
        "ln_b": nrm(ks[9], (DEPTH, D_CONV), 0.02),
        "w_a2": nrm(ks[10], (DEPTH, GATE_RANK, D_GLA_K), GATE_RANK ** -0.5),
        "b_a2": nrm(ks[11], (DEPTH, D_GLA_K), 0.02),
        "gla_norm_g": 1.0 + nrm(ks[12], (DEPTH, HEAD_V), 0.02),
        "w_out": nrm(ks[13], (DEPTH, D_MIX, D_MODEL), D_MIX ** -0.5),
        "g_ffn": 1.0 + nrm(ks[14], (DEPTH, D_MODEL), 0.02),
        "w_ffn_in": nrm(ks[15], (DEPTH, D_MODEL, 2 * D_FF), D_MODEL ** -0.5),
        "w_ffn_out": nrm(ks[16], (DEPTH, D_FF, D_MODEL), D_FF ** -0.5),
        "w_ada_final": nrm(ks[17], (D_MODEL, 2 * D_MODEL), D_MODEL ** -0.5),
        "b_ada_final": nrm(ks[18], (2 * D_MODEL,), 0.02),
        "g_final": 1.0 + nrm(ks[19], (D_MODEL,), 0.02),
    }


def reference(x, c, w_ada, b_ada, g_mix, w_in, conv_w, conv_b, ln_g, ln_b, w_a2, b_a2,
              gla_norm_g, w_out, g_ffn, w_ffn_in, w_ffn_out, w_ada_final, b_ada_final, g_final):
    b_sz, t_len, _ = x.shape
    c_act = jax.nn.silu(c)
    split_pts = [D_CONV, 2 * D_CONV, 2 * D_CONV + D_GLA_K, 2 * D_CONV + 2 * D_GLA_K,
                 2 * D_CONV + 2 * D_GLA_K + D_GLA_V, 2 * D_CONV + 2 * D_GLA_K + 2 * D_GLA_V]

    for l in range(DEPTH):
        mod = (c_act @ w_ada[l] + b_ada[l])[:, None, :]
        sh1, sc1, gt1, sh2, sc2, gt2 = jnp.split(mod, 6, axis=-1)

        h = rms_norm(x, g_mix[l]) * (1.0 + sc1) + sh1
        proj = h @ w_in[l]
        cv, cg, q, k, v, og, a_low = jnp.split(proj, split_pts, axis=-1)

        u = cv * jax.nn.sigmoid(cg)
        u = causal_depthwise_conv(u, conv_w[l], conv_b[l])
        y_conv = jax.nn.silu(layer_norm(u, ln_g[l], ln_b[l]))

        log_a = jax.nn.log_sigmoid(a_low @ w_a2[l] + b_a2[l]) / GATE_TAU
        qh = q.reshape(b_sz, t_len, GLA_HEADS, HEAD_K) * (HEAD_K ** -0.5)
        kh = k.reshape(b_sz, t_len, GLA_HEADS, HEAD_K)
        vh = v.reshape(b_sz, t_len, GLA_HEADS, HEAD_V)
        ah = log_a.reshape(b_sz, t_len, GLA_HEADS, HEAD_K)
        o = gla_chunked(qh, kh, vh, ah)
        o = rms_norm(o, gla_norm_g[l]).reshape(b_sz, t_len, D_GLA_V)
        y_gla = o * jax.nn.silu(og)

        mix = jnp.concatenate([y_conv, y_gla], axis=-1) @ w_out[l]
        x = x + gt1 * mix

        h = rms_norm(x, g_ffn[l]) * (1.0 + sc2) + sh2
        gate, up = jnp.split(h @ w_ffn_in[l], 2, axis=-1)
        x = x + gt2 * ((jax.nn.silu(gate) * up) @ w_ffn_out[l])

    fmod = (c_act @ w_ada_final + b_ada_final)[:, None, :]
    shf, scf = jnp.split(fmod, 2, axis=-1)
    return rms_norm(x, g_final) * (1.0 + scf) + shf
```

```python
import functools

import jax
import jax.numpy as jnp
from jax import lax
from jax.experimental import pallas as pl
from jax.experimental.pallas import tpu as pltpu

F32 = jnp.float32
BF16 = jnp.bfloat16

D_MODEL = 1024
D_CONV = 512
CONV_KERNEL = 31
D_GLA_V = 512
D_GLA_K = 256
GLA_HEADS = 4
HEAD_K = D_GLA_K // GLA_HEADS
HEAD_V = D_GLA_V // GLA_HEADS
GATE_RANK = 16
GATE_TAU = 16.0
D_FF = 2816
EPS = 1e-6

LANES = 128
SUBLANES = 8
MXU_DIM = 256
VMEM_LIMIT_BYTES = 56 * 1024 * 1024

MOD_COLS = 2048
TOKENS_IN = 512
TOKENS_MIX = 512
TOKENS_FFN = 512
CONV_HALO = 32
CONV_ROWS = 32
GLA_CHUNK = 128
GLA_MAX_FACTORISED_DECAY = 40.0
FFN_SLABS = ((0, 6 * MXU_DIM), (6 * MXU_DIM, D_FF))


def _sigmoid(x):
    return 1.0 / (1.0 + jnp.exp(-x))


def _dot(a, b):
    return jnp.dot(a, b, preferred_element_type=F32)


def _dot_nt(a, b):
    return lax.dot_general(a, b, (((1,), (1,)), ((), ())), preferred_element_type=F32)


def _dot_tn(a, b):
    return lax.dot_general(a, b, (((0,), (0,)), ((), ())), preferred_element_type=F32)


def _split3(a):
    hi = a.astype(BF16)
    r1 = a - hi.astype(F32)
    mid = r1.astype(BF16)
    lo = (r1 - mid.astype(F32)).astype(BF16)
    return hi, mid, lo


def _modulation_kernel(c_ref, w_ref, b_ref, o_ref):
    c = c_ref[...]
    c_act = c * _sigmoid(c)
    o_ref[...] = jnp.dot(c_act, w_ref[...], preferred_element_type=F32,
                         precision=lax.Precision.HIGHEST) + b_ref[...]


def _modulation(c_pad, w, b):
    rows, d = c_pad.shape
    n = w.shape[1]
    assert n % MOD_COLS == 0
    return pl.pallas_call(
        _modulation_kernel,
        grid=(n // MOD_COLS,),
        in_specs=[
            pl.BlockSpec((rows, d), lambda j: (0, 0)),
            pl.BlockSpec((d, MOD_COLS), lambda j: (0, j)),
            pl.BlockSpec((1, MOD_COLS), lambda j: (0, j)),
        ],
        out_specs=pl.BlockSpec((rows, MOD_COLS), lambda j: (0, j)),
        out_shape=jax.ShapeDtypeStruct((rows, n), F32),
        compiler_params=pltpu.CompilerParams(
            dimension_semantics=("arbitrary",), vmem_limit_bytes=VMEM_LIMIT_BYTES),
        name="modulation",
    )(c_pad, w, b.reshape(1, n))


def _in_proj_kernel(x_ref, sc_ref, sh_ref, g_ref, wcv_ref, wcg_ref, wq_ref, wk_ref,
                    wv_ref, wog_ref, wa_ref, wa2_ref, ba2_ref,
                    u_ref, q_ref, k_ref, v_ref, gate_ref, la_ref):
    x = x_ref[0]
    ms = jnp.mean(x * x, axis=-1, keepdims=True)
    h = x * lax.rsqrt(ms + EPS) * g_ref[...]
    h = h * (1.0 + sc_ref[0]) + sh_ref[0]
    hb = h.astype(BF16)

    cv = _dot(hb, wcv_ref[...])
    cg = _dot(hb, wcg_ref[...])
    u_ref[0] = cv * _sigmoid(cg)
    q_ref[0] = _dot(hb, wq_ref[...]) * (HEAD_K ** -0.5)
    k_ref[0] = _dot(hb, wk_ref[...])
    v_ref[0] = _dot(hb, wv_ref[...])
    og = _dot(hb, wog_ref[...])
    gate_ref[0] = og * _sigmoid(og)

    a_low = _dot(hb, wa_ref[...])
    z = _dot(a_low.astype(BF16), wa2_ref[...]) + ba2_ref[...]
    log_sig = jnp.minimum(z, 0.0) - jnp.log(1.0 + jnp.exp(-jnp.abs(z)))
    la_ref[0] = log_sig * (1.0 / GATE_TAU)


def _in_proj(x, sc, sh, g, wcv, wcg, wq, wk, wv, wog, wa, wa2, ba2):
    b_sz, t_len, d = x.shape
    tm = TOKENS_IN
    assert t_len % tm == 0

    def tok(width):
        return pl.BlockSpec((1, tm, width), lambda b, t: (b, t, 0))

    def per_batch(width):
        return pl.BlockSpec((1, 1, width), lambda b, t: (b, 0, 0))

    def const(arr):
        return pl.BlockSpec(arr.shape, lambda b, t: (0,) * arr.ndim)

    def out(width):
        return jax.ShapeDtypeStruct((b_sz, t_len, width), F32)

    return pl.pallas_call(
        _in_proj_kernel,
        grid=(b_sz, t_len // tm),
        in_specs=[tok(d), per_batch(d), per_batch(d), const(g), const(wcv), const(wcg),
                  const(wq), const(wk), const(wv), const(wog), const(wa), const(wa2),
                  const(ba2)],
        out_specs=[tok(D_CONV), tok(D_GLA_K), tok(D_GLA_K), tok(D_GLA_V), tok(D_GLA_V),
                   tok(D_GLA_K)],
        out_shape=[out(D_CONV), out(D_GLA_K), out(D_GLA_K), out(D_GLA_V), out(D_GLA_V),
                   out(D_GLA_K)],
        compiler_params=pltpu.CompilerParams(
            dimension_semantics=("arbitrary", "arbitrary"), vmem_limit_bytes=VMEM_LIMIT_BYTES),
        name="in_proj",
    )(x, sc, sh, g, wcv, wcg, wq, wk, wv, wog, wa, wa2, ba2)


def _mixer_kernel(u_ref, q_ref, k_ref, la_ref, v_ref, gate_ref, cw_ref, cb_ref,
                  lng_ref, lnb_ref, gng_ref, y_ref, win_ref, st_ref, cum_ref):
    tb = u_ref.shape[1]
    lc = GLA_CHUNK

    @pl.when(pl.program_id(1) == 0)
    def _():
        win_ref[0:CONV_HALO, :] = jnp.zeros((CONV_HALO, D_CONV), F32)
        st_ref[...] = jnp.zeros_like(st_ref)

    win_ref[CONV_HALO:CONV_HALO + tb, :] = u_ref[0]
    first_tap = CONV_HALO - (CONV_KERNEL - 1)

    def conv_tile(i, carry):
        r0 = pl.multiple_of(i * CONV_ROWS, CONV_ROWS)
        acc = jnp.broadcast_to(cb_ref[...], (CONV_ROWS, D_CONV))
        window = win_ref[pl.ds(r0, CONV_ROWS + CONV_HALO), :]
        for tap in range(CONV_KERNEL):
            lo = first_tap + tap
            acc = acc + cw_ref[tap:tap + 1, :] * window[lo:lo + CONV_ROWS, :]
        mu = jnp.mean(acc, axis=-1, keepdims=True)
        cen = acc - mu
        var = jnp.mean(cen * cen, axis=-1, keepdims=True)
        yn = cen * lax.rsqrt(var + EPS) * lng_ref[...] + lnb_ref[...]
        y_ref[0, pl.ds(r0, CONV_ROWS), 0:D_CONV] = (yn * _sigmoid(yn)).astype(y_ref.dtype)
        return carry

    lax.fori_loop(0, tb // CONV_ROWS, conv_tile, 0)
    win_ref[0:CONV_HALO, :] = win_ref[tb:tb + CONV_HALO, :]

    row = lax.broadcasted_iota(jnp.int32, (lc, lc), 0)
    col = lax.broadcasted_iota(jnp.int32, (lc, lc), 1)
    tril = (col <= row).astype(BF16)
    klane = lax.broadcasted_iota(jnp.int32, (1, D_GLA_K), 1) // HEAD_K
    vlane = lax.broadcasted_iota(jnp.int32, (1, D_GLA_V), 1) // HEAD_V
    srow = lax.broadcasted_iota(jnp.int32, (lc, GLA_HEADS * lc), 0)
    scol = lax.broadcasted_iota(jnp.int32, (lc, GLA_HEADS * lc), 1)
    causal = (scol % lc) <= srow
    st_row = lax.broadcasted_iota(jnp.int32, (D_GLA_V, D_GLA_K), 0) // HEAD_V
    st_col = lax.broadcasted_iota(jnp.int32, (D_GLA_V, D_GLA_K), 1) // HEAD_K
    st_mask = (st_row == st_col).astype(F32)
    e_row = lax.broadcasted_iota(jnp.int32, (D_GLA_K, D_GLA_V), 0) // HEAD_K
    e_col = lax.broadcasted_iota(jnp.int32, (D_GLA_K, D_GLA_V), 1) // HEAD_V
    rows1 = lax.broadcasted_iota(jnp.int32, (lc, 1), 0)

    def chunk(ci, carry):
        r0 = pl.multiple_of(ci * lc, lc)
        a = la_ref[0, pl.ds(r0, lc), :]
        q = q_ref[0, pl.ds(r0, lc), :]
        k = k_ref[0, pl.ds(r0, lc), :]
        v = v_ref[0, pl.ds(r0, lc), :]

        a_hi, a_mid, a_lo = _split3(a)
        cum = _dot(tril, a_hi) + _dot(tril, a_mid) + _dot(tril, a_lo)
        cum_last = cum[lc - 1:lc, :]
        ref = cum[lc // 2 - 1:lc // 2, :]
        vb = v.astype(BF16)

        def factorised(_):
            qt = (q * jnp.exp(cum - ref)).astype(BF16)
            kt = k * jnp.exp(ref - cum)
            kbd = jnp.concatenate(
                [jnp.where(klane == h, kt, 0.0).astype(BF16) for h in range(GLA_HEADS)], axis=0)
            s = _dot_nt(qt, kbd)
            s = jnp.where(causal, s, 0.0).astype(BF16)
            vbd = jnp.concatenate(
                [jnp.where(vlane == h, v, 0.0).astype(BF16) for h in range(GLA_HEADS)], axis=0)
            return _dot(s, vbd)

        def exact(_):
            cum_ref[...] = cum
            expand = (e_row == e_col).astype(BF16)

            def key_row(j, o):
                kj = k_ref[0, pl.ds(r0 + j, 1), :]
                cj = cum_ref[pl.ds(j, 1), :]
                vj = v_ref[0, pl.ds(r0 + j, 1), :]
                p = q * kj * jnp.exp(jnp.minimum(cum - cj, 0.0))
                p = jnp.where(rows1 >= j, p, 0.0).astype(BF16)
                return o + _dot(p, expand) * vj

            return lax.fori_loop(0, lc, key_row, jnp.zeros((lc, D_GLA_V), F32))

        total_decay = jnp.max(-cum_last)
        o_intra = lax.cond(total_decay > GLA_MAX_FACTORISED_DECAY, exact, factorised, 0)

        st = st_ref[...]
        o_inter = _dot_nt((q * jnp.exp(cum)).astype(BF16), st.astype(BF16))
        kh = (k * jnp.exp(cum_last - cum)).astype(BF16)
        st_ref[...] = st * jnp.exp(cum_last) + _dot_tn(vb, kh) * st_mask

        o = o_inter + o_intra
        normed = []
        for h in range(GLA_HEADS):
            oh = o[:, h * HEAD_V:(h + 1) * HEAD_V]
            ms = jnp.mean(oh * oh, axis=-1, keepdims=True)
            normed.append(oh * lax.rsqrt(ms + EPS))
        on = jnp.concatenate(normed, axis=-1) * gng_ref[...]
        y_gla = on * gate_ref[0, pl.ds(r0, lc), :]
        y_ref[0, pl.ds(r0, lc), D_CONV:D_CONV + D_GLA_V] = y_gla.astype(y_ref.dtype)
        return carry

    lax.fori_loop(0, tb // lc, chunk, 0)


def _mixer(u, q, k, la, v, gate, cw, cb, lng, lnb, gng):
    b_sz, t_len, _ = u.shape
    tb = TOKENS_MIX
    assert t_len % tb == 0 and tb % GLA_CHUNK == 0 and tb % CONV_ROWS == 0

    def tok(width):
        return pl.BlockSpec((1, tb, width), lambda b, t: (b, t, 0))

    def const(arr):
        return pl.BlockSpec(arr.shape, lambda b, t: (0,) * arr.ndim)

    return pl.pallas_call(
        _mixer_kernel,
        grid=(b_sz, t_len // tb),
        in_specs=[tok(D_CONV), tok(D_GLA_K), tok(D_GLA_K), tok(D_GLA_K), tok(D_GLA_V),
                  tok(D_GLA_V), const(cw), const(cb), const(lng), const(lnb), const(gng)],
        out_specs=tok(D_CONV + D_GLA_V),
        out_shape=jax.ShapeDtypeStruct((b_sz, t_len, D_CONV + D_GLA_V), BF16),
        scratch_shapes=[
            pltpu.VMEM((CONV_HALO + tb, D_CONV), F32),
            pltpu.VMEM((D_GLA_V, D_GLA_K), F32),
            pltpu.VMEM((GLA_CHUNK, D_GLA_K), F32),
        ],
        compiler_params=pltpu.CompilerParams(
            dimension_semantics=("arbitrary", "arbitrary"), vmem_limit_bytes=VMEM_LIMIT_BYTES),
        name="mixer",
    )(u, q, k, la, v, gate, cw, cb, lng, lnb, gng)


def _out_ffn_kernel(x_ref, y_ref, gt1_ref, sc2_ref, sh2_ref, gt2_ref, scf_ref, shf_ref,
                    gffn_ref, gfin_ref, wout_ref, wg_ref, wu_ref, wo_ref, o_ref):
    x = x_ref[0]
    x1 = x + gt1_ref[0] * _dot(y_ref[0], wout_ref[...])

    ms = jnp.mean(x1 * x1, axis=-1, keepdims=True)
    h = x1 * lax.rsqrt(ms + EPS) * gffn_ref[...]
    hb = (h * (1.0 + sc2_ref[0]) + sh2_ref[0]).astype(BF16)

    ffn = jnp.zeros_like(x1)
    for lo, hi in FFN_SLABS:
        gate = _dot(hb, wg_ref[:, lo:hi])
        up = _dot(hb, wu_ref[:, lo:hi])
        act = (gate * _sigmoid(gate) * up).astype(BF16)
        ffn = ffn + _dot(act, wo_ref[lo:hi, :])
    x2 = x1 + gt2_ref[0] * ffn

    ms = jnp.mean(x2 * x2, axis=-1, keepdims=True)
    xn = x2 * lax.rsqrt(ms + EPS) * gfin_ref[...]
    o_ref[0] = xn * (1.0 + scf_ref[0]) + shf_ref[0]


def _out_ffn(x, y, gt1, sc2, sh2, gt2, scf, shf, gffn, gfin, wout, wg, wu, wo):
    b_sz, t_len, d = x.shape
    tm = TOKENS_FFN
    assert t_len % tm == 0

    def tok(width):
        return pl.BlockSpec((1, tm, width), lambda b, t: (b, t, 0))

    def per_batch(width):
        return pl.BlockSpec((1, 1, width), lambda b, t: (b, 0, 0))

    def const(arr):
        return pl.BlockSpec(arr.shape, lambda b, t: (0,) * arr.ndim,
                            pipeline_mode=pl.Buffered(1))

    return pl.pallas_call(
        _out_ffn_kernel,
        grid=(b_sz, t_len // tm),
        in_specs=[tok(d), tok(d)] + [per_batch(d)] * 6
                 + [const(gffn), const(gfin), const(wout), const(wg), const(wu), const(wo)],
        out_specs=tok(d),
        out_shape=jax.ShapeDtypeStruct((b_sz, t_len, d), F32),
        compiler_params=pltpu.CompilerParams(
            dimension_semantics=("arbitrary", "arbitrary"), vmem_limit_bytes=VMEM_LIMIT_BYTES),
        name="out_ffn",
    )(x, y, gt1, sc2, sh2, gt2, scf, shf, gffn, gfin, wout, wg, wu, wo)


def kernel(x, c, w_ada, b_ada, g_mix, w_in, conv_w, conv_b, ln_g, ln_b, w_a2, b_a2,
           gla_norm_g, w_out, g_ffn, w_ffn_in, w_ffn_out, w_ada_final, b_ada_final, g_final):
    b_sz, t_len, d = x.shape
    assert w_ada.shape[0] == 1, "single-layer block"

    c_pad = jnp.pad(c, ((0, SUBLANES - b_sz), (0, 0)))
    mod = _modulation(c_pad, w_ada[0], b_ada[0])[:b_sz]
    fmod = _modulation(c_pad, w_ada_final, b_ada_final)[:b_sz]
    sh1, sc1, gt1, sh2, sc2, gt2 = [m.reshape(b_sz, 1, d) for m in jnp.split(mod, 6, axis=-1)]
    shf, scf = [m.reshape(b_sz, 1, d) for m in jnp.split(fmod, 2, axis=-1)]

    w = w_in[0]
    o_cg, o_q, o_k = D_CONV, 2 * D_CONV, 2 * D_CONV + D_GLA_K
    o_v = o_k + D_GLA_K
    o_og = o_v + D_GLA_V
    o_a = o_og + D_GLA_V
    wcv, wcg = w[:, :o_cg].astype(BF16), w[:, o_cg:o_q].astype(BF16)
    wq, wk = w[:, o_q:o_k].astype(BF16), w[:, o_k:o_v].astype(BF16)
    wv, wog = w[:, o_v:o_og].astype(BF16), w[:, o_og:o_a].astype(BF16)
    wa = jnp.pad(w[:, o_a:], ((0, 0), (0, LANES - GATE_RANK))).astype(BF16)
    wa2 = jnp.pad(w_a2[0], ((0, LANES - GATE_RANK), (0, 0))).astype(BF16)

    u, q, k, v, gate, la = _in_proj(
        x, sc1, sh1, g_mix[0].reshape(1, d), wcv, wcg, wq, wk, wv, wog, wa, wa2,
        b_a2[0].reshape(1, D_GLA_K))

    cw = jnp.pad(conv_w[0], ((0, 1), (0, 0)))
    y = _mixer(u, q, k, la, v, gate, cw, conv_b[0].reshape(1, D_CONV),
               ln_g[0].reshape(1, D_CONV), ln_b[0].reshape(1, D_CONV),
               jnp.tile(gla_norm_g[0], GLA_HEADS).reshape(1, D_GLA_V))

    wf = w_ffn_in[0]
    return _out_ffn(
        x, y, gt1, sc2, sh2, gt2, scf, shf, g_ffn[0].reshape(1, d), g_final.reshape(1, d),
        w_out[0].astype(BF16), wf[:, :D_FF].astype(BF16), wf[:, D_FF:].astype(BF16),
        w_ffn_out[0].astype(BF16))
```

```python
import functools

import jax
import jax.numpy as jnp
from jax import lax
from jax.experimental import pallas as pl
from jax.experimental.pallas import tpu as pltpu

F32 = jnp.float32
BF16 = jnp.bfloat16

D_MODEL = 1024
D_CONV = 512
CONV_KERNEL = 31
D_GLA_V = 512
D_GLA_K = 256
GLA_HEADS = 4
HEAD_K = D_GLA_K // GLA_HEADS
HEAD_V = D_GLA_V // GLA_HEADS
GATE_RANK = 16
GATE_TAU = 16.0
D_FF = 2816
EPS = 1e-6

LANES = 128
SUBLANES = 8
MXU_DIM = 256
VMEM_LIMIT_BYTES = 56 * 1024 * 1024

MOD_COLS = 2048
TOKENS_IN = 512
TOKENS_MIX = 512
TOKENS_FFN = 512
CONV_HALO = 32
CONV_ROWS = 64
CONV_PARTIALS = 2
NORM_ROWS = 128
GLA_CHUNK = 128
GLA_MAX_FACTORISED_DECAY = 40.0
FFN_SLABS = ((0, 6 * MXU_DIM), (6 * MXU_DIM, D_FF))


def _sigmoid(x):
    return 1.0 / (1.0 + jnp.exp(-x))


def _dot(a, b):
    return jnp.dot(a, b, preferred_element_type=F32)


def _dot_nt(a, b):
    return lax.dot_general(a, b, (((1,), (1,)), ((), ())), preferred_element_type=F32)


def _dot_tn(a, b):
    return lax.dot_general(a, b, (((0,), (0,)), ((), ())), preferred_element_type=F32)


def _split3(a):
    hi = a.astype(BF16)
    r1 = a - hi.astype(F32)
    mid = r1.astype(BF16)
    lo = (r1 - mid.astype(F32)).astype(BF16)
    return hi, mid, lo


def _modulation_kernel(c_ref, w_ref, b_ref, o_ref):
    c = c_ref[...]
    c_act = c * _sigmoid(c)
    o_ref[...] = jnp.dot(c_act, w_ref[...], preferred_element_type=F32,
                         precision=lax.Precision.HIGHEST) + b_ref[...]


def _modulation(c_pad, w, b):
    rows, d = c_pad.shape
    n = w.shape[1]
    assert n % MOD_COLS == 0
    return pl.pallas_call(
        _modulation_kernel,
        grid=(n // MOD_COLS,),
        in_specs=[
            pl.BlockSpec((rows, d), lambda j: (0, 0)),
            pl.BlockSpec((d, MOD_COLS), lambda j: (0, j)),
            pl.BlockSpec((1, MOD_COLS), lambda j: (0, j)),
        ],
        out_specs=pl.BlockSpec((rows, MOD_COLS), lambda j: (0, j)),
        out_shape=jax.ShapeDtypeStruct((rows, n), F32),
        compiler_params=pltpu.CompilerParams(
            dimension_semantics=("arbitrary",), vmem_limit_bytes=VMEM_LIMIT_BYTES),
        name="modulation",
    )(c_pad, w, b.reshape(1, n))


def _in_proj_kernel(x_ref, sc_ref, sh_ref, g_ref, wcv_ref, wcg_ref, wq_ref, wk_ref,
                    wv_ref, wog_ref, wa_ref, wa2_ref, ba2_ref,
                    u_ref, q_ref, k_ref, v_ref, gate_ref, la_ref):
    x = x_ref[0]
    ms = jnp.mean(x * x, axis=-1, keepdims=True)
    h = x * lax.rsqrt(ms + EPS) * g_ref[...]
    h = h * (1.0 + sc_ref[0]) + sh_ref[0]
    hb = h.astype(BF16)

    cv = _dot(hb, wcv_ref[...])
    cg = _dot(hb, wcg_ref[...])
    u_ref[0] = cv * _sigmoid(cg)
    q_ref[0] = _dot(hb, wq_ref[...]) * (HEAD_K ** -0.5)
    k_ref[0] = _dot(hb, wk_ref[...])
    v_ref[0] = _dot(hb, wv_ref[...])
    og = _dot(hb, wog_ref[...])
    gate_ref[0] = og * _sigmoid(og)

    a_low = _dot(hb, wa_ref[...])
    z = _dot(a_low.astype(BF16), wa2_ref[...]) + ba2_ref[...]
    log_sig = jnp.minimum(z, 0.0) - jnp.log(1.0 + jnp.exp(-jnp.abs(z)))
    la_ref[0] = log_sig * (1.0 / GATE_TAU)


def _in_proj(x, sc, sh, g, wcv, wcg, wq, wk, wv, wog, wa, wa2, ba2):
    b_sz, t_len, d = x.shape
    tm = TOKENS_IN
    assert t_len % tm == 0

    def tok(width):
        return pl.BlockSpec((1, tm, width), lambda b, t: (b, t, 0))

    def per_batch(width):
        return pl.BlockSpec((1, 1, width), lambda b, t: (b, 0, 0))

    def const(arr):
        return pl.BlockSpec(arr.shape, lambda b, t: (0,) * arr.ndim)

    def out(width):
        return jax.ShapeDtypeStruct((b_sz, t_len, width), F32)

    return pl.pallas_call(
        _in_proj_kernel,
        grid=(b_sz, t_len // tm),
        in_specs=[tok(d), per_batch(d), per_batch(d), const(g), const(wcv), const(wcg),
                  const(wq), const(wk), const(wv), const(wog), const(wa), const(wa2),
                  const(ba2)],
        out_specs=[tok(D_CONV), tok(D_GLA_K), tok(D_GLA_K), tok(D_GLA_V), tok(D_GLA_V),
                   tok(D_GLA_K)],
        out_shape=[out(D_CONV), out(D_GLA_K), out(D_GLA_K), out(D_GLA_V), out(D_GLA_V),
                   out(D_GLA_K)],
        compiler_params=pltpu.CompilerParams(
            dimension_semantics=("arbitrary", "arbitrary"), vmem_limit_bytes=VMEM_LIMIT_BYTES),
        name="in_proj",
    )(x, sc, sh, g, wcv, wcg, wq, wk, wv, wog, wa, wa2, ba2)


def _mixer_kernel(u_ref, q_ref, k_ref, la_ref, v_ref, gate_ref, cw_ref, cb_ref,
                  lng_ref, lnb_ref, gng_ref, y_ref, win_ref, pre_ref, st_ref, cum_ref):
    tb = u_ref.shape[1]
    lc = GLA_CHUNK

    @pl.when(pl.program_id(1) == 0)
    def _():
        win_ref[...] = jnp.zeros_like(win_ref)
        st_ref[...] = jnp.zeros_like(st_ref)

    for c in range(D_CONV // LANES):
        u_lanes = u_ref[0, :, c * LANES:(c + 1) * LANES]
        for phase in range(SUBLANES):
            win_ref[phase, c, CONV_HALO - phase:CONV_HALO - phase + tb, :] = u_lanes
    first_tap = CONV_HALO - (CONV_KERNEL - 1)

    for c in range(D_CONV // LANES):
        lanes = slice(c * LANES, (c + 1) * LANES)
        taps = []
        for phase in range(SUBLANES):
            for base in range(0, CONV_HALO + SUBLANES, SUBLANES):
                tap = base + phase - first_tap
                if 0 <= tap < CONV_KERNEL:
                    taps.append((cw_ref[tap, :, lanes], phase, base))
        bias = cb_ref[:, lanes]

        def conv_tile(i, carry, lanes=lanes, taps=taps, bias=bias):
            r0 = pl.multiple_of(i * CONV_ROWS, CONV_ROWS)
            parts = [jnp.broadcast_to(bias, (CONV_ROWS, LANES))] + [None] * (CONV_PARTIALS - 1)
            for n, (w8, phase, base) in enumerate(taps):
                w_rows = jnp.concatenate([w8] * (CONV_ROWS // SUBLANES), axis=0)
                term = w_rows * win_ref[phase, c, pl.ds(r0 + base, CONV_ROWS), :]
                p = n % CONV_PARTIALS
                parts[p] = term if parts[p] is None else parts[p] + term
            pre_ref[pl.ds(r0, CONV_ROWS), lanes] = functools.reduce(lambda a, b: a + b, parts)
            return carry

        lax.fori_loop(0, tb // CONV_ROWS, conv_tile, 0)

    def norm_tile(i, carry):
        r0 = pl.multiple_of(i * NORM_ROWS, NORM_ROWS)
        acc = pre_ref[pl.ds(r0, NORM_ROWS), :]
        mu = jnp.mean(acc, axis=-1, keepdims=True)
        cen = acc - mu
        var = jnp.mean(cen * cen, axis=-1, keepdims=True)
        yn = cen * lax.rsqrt(var + EPS) * lng_ref[...] + lnb_ref[...]
        y_ref[0, pl.ds(r0, NORM_ROWS), 0:D_CONV] = (yn * _sigmoid(yn)).astype(y_ref.dtype)
        return carry

    lax.fori_loop(0, tb // NORM_ROWS, norm_tile, 0)
    win_ref[:, :, 0:CONV_HALO, :] = win_ref[:, :, tb:tb + CONV_HALO, :]

    n_chunks = tb // lc
    row = lax.broadcasted_iota(jnp.int32, (lc, lc), 0)
    col = lax.broadcasted_iota(jnp.int32, (lc, lc), 1)
    tril = (col <= row).astype(BF16)
    klane = lax.broadcasted_iota(jnp.int32, (1, D_GLA_K), 1) // HEAD_K
    srow = lax.broadcasted_iota(jnp.int32, (GLA_HEADS * lc, lc), 0)
    scol = lax.broadcasted_iota(jnp.int32, (GLA_HEADS * lc, lc), 1)
    causal = scol <= (srow % lc)
    st_row = lax.broadcasted_iota(jnp.int32, (D_GLA_V, D_GLA_K), 0) // HEAD_V
    st_col = lax.broadcasted_iota(jnp.int32, (D_GLA_V, D_GLA_K), 1) // HEAD_K
    st_mask = (st_row == st_col).astype(F32)
    rows1 = lax.broadcasted_iota(jnp.int32, (lc, 1), 0)

    max_decay = None
    for ci in range(n_chunks):
        a_hi, a_mid, a_lo = _split3(la_ref[0, ci * lc:(ci + 1) * lc, :])
        cum = _dot(tril, a_hi) + _dot(tril, a_mid) + _dot(tril, a_lo)
        cum_ref[ci * lc:(ci + 1) * lc, :] = cum
        decay = jnp.max(-cum[lc - 1:lc, :])
        max_decay = decay if max_decay is None else jnp.maximum(max_decay, decay)

    def intra_factorised(ci, q, k, cum):
        ref = cum[lc // 2 - 1:lc // 2, :]
        qt = (q * jnp.exp(cum - ref)).astype(BF16)
        kt = (k * jnp.exp(ref - cum)).astype(BF16)
        qbd = jnp.concatenate(
            [jnp.where(klane == h, qt, jnp.zeros_like(qt)) for h in range(GLA_HEADS)], axis=0)
        s = _dot_nt(qbd, kt)
        s = jnp.where(causal, s, 0.0).astype(BF16)
        heads = []
        for h in range(GLA_HEADS):
            vh = v_ref[0, ci * lc:(ci + 1) * lc, h * HEAD_V:(h + 1) * HEAD_V].astype(BF16)
            heads.append(_dot(s[h * lc:(h + 1) * lc, :], vh))
        return jnp.concatenate(heads, axis=-1)

    def intra_exact(ci, q, k, cum):
        e_row = lax.broadcasted_iota(jnp.int32, (D_GLA_K, D_GLA_V), 0) // HEAD_K
        e_col = lax.broadcasted_iota(jnp.int32, (D_GLA_K, D_GLA_V), 1) // HEAD_V
        expand = (e_row == e_col).astype(BF16)

        def key_row(j, o):
            kj = k_ref[0, pl.ds(ci * lc + j, 1), :]
            cj = cum_ref[pl.ds(ci * lc + j, 1), :]
            vj = v_ref[0, pl.ds(ci * lc + j, 1), :]
            p = q * kj * jnp.exp(jnp.minimum(cum - cj, 0.0))
            p = jnp.where(rows1 >= j, p, 0.0).astype(BF16)
            return o + _dot(p, expand) * vj

        return lax.fori_loop(0, lc, key_row, jnp.zeros((lc, D_GLA_V), F32))

    def gla_tile(intra):
        for ci in range(n_chunks):
            rows = slice(ci * lc, (ci + 1) * lc)
            q = q_ref[0, rows, :]
            k = k_ref[0, rows, :]
            cum = cum_ref[rows, :]
            cum_last = cum[lc - 1:lc, :]
            o_intra = intra(ci, q, k, cum)

            st = st_ref[...]
            o_inter = _dot_nt((q * jnp.exp(cum)).astype(BF16), st.astype(BF16))
            kh = (k * jnp.exp(cum_last - cum)).astype(BF16)
            vb = v_ref[0, rows, :].astype(BF16)
            st_ref[...] = st * jnp.exp(cum_last) + _dot_tn(vb, kh) * st_mask

            o = o_inter + o_intra
            normed = []
            for h in range(GLA_HEADS):
                oh = o[:, h * HEAD_V:(h + 1) * HEAD_V]
                ms = jnp.mean(oh * oh, axis=-1, keepdims=True)
                normed.append(oh * lax.rsqrt(ms + EPS))
            on = jnp.concatenate(normed, axis=-1) * gng_ref[...]
            y_gla = on * gate_ref[0, rows, :]
            y_ref[0, rows, D_CONV:D_CONV + D_GLA_V] = y_gla.astype(y_ref.dtype)

    @pl.when(max_decay <= GLA_MAX_FACTORISED_DECAY)
    def _():
        gla_tile(intra_factorised)

    @pl.when(max_decay > GLA_MAX_FACTORISED_DECAY)
    def _():
        gla_tile(intra_exact)


def _mixer(u, q, k, la, v, gate, cw, cb, lng, lnb, gng):
    b_sz, t_len, _ = u.shape
    tb = TOKENS_MIX
    assert t_len % tb == 0 and tb % GLA_CHUNK == 0 and tb % CONV_ROWS == 0 and tb % NORM_ROWS == 0

    def tok(width):
        return pl.BlockSpec((1, tb, width), lambda b, t: (b, t, 0))

    def const(arr):
        return pl.BlockSpec(arr.shape, lambda b, t: (0,) * arr.ndim)

    return pl.pallas_call(
        _mixer_kernel,
        grid=(b_sz, t_len // tb),
        in_specs=[tok(D_CONV), tok(D_GLA_K), tok(D_GLA_K), tok(D_GLA_K), tok(D_GLA_V),
                  tok(D_GLA_V), const(cw), const(cb), const(lng), const(lnb), const(gng)],
        out_specs=tok(D_CONV + D_GLA_V),
        out_shape=jax.ShapeDtypeStruct((b_sz, t_len, D_CONV + D_GLA_V), BF16),
        scratch_shapes=[
            pltpu.VMEM((SUBLANES, D_CONV // LANES, CONV_HALO + tb, LANES), F32),
            pltpu.VMEM((tb, D_CONV), F32),
            pltpu.VMEM((D_GLA_V, D_GLA_K), F32),
            pltpu.VMEM((tb, D_GLA_K), F32),
        ],
        compiler_params=pltpu.CompilerParams(
            dimension_semantics=("arbitrary", "arbitrary"), vmem_limit_bytes=VMEM_LIMIT_BYTES),
        name="mixer",
    )(u, q, k, la, v, gate, cw, cb, lng, lnb, gng)


def _out_ffn_kernel(x_ref, y_ref, gt1_ref, sc2_ref, sh2_ref, gt2_ref, scf_ref, shf_ref,
                    gffn_ref, gfin_ref, wout_ref, wg_ref, wu_ref, wo_ref, o_ref):
    x = x_ref[0]
    x1 = x + gt1_ref[0] * _dot(y_ref[0], wout_ref[...])

    ms = jnp.mean(x1 * x1, axis=-1, keepdims=True)
    h = x1 * lax.rsqrt(ms + EPS) * gffn_ref[...]
    hb = (h * (1.0 + sc2_ref[0]) + sh2_ref[0]).astype(BF16)

    ffn = jnp.zeros_like(x1)
    for lo, hi in FFN_SLABS:
        gate = _dot(hb, wg_ref[:, lo:hi])
        up = _dot(hb, wu_ref[:, lo:hi])
        act = (gate * _sigmoid(gate) * up).astype(BF16)
        ffn = ffn + _dot(act, wo_ref[lo:hi, :])
    x2 = x1 + gt2_ref[0] * ffn

    ms = jnp.mean(x2 * x2, axis=-1, keepdims=True)
    xn = x2 * lax.rsqrt(ms + EPS) * gfin_ref[...]
    o_ref[0] = xn * (1.0 + scf_ref[0]) + shf_ref[0]


def _out_ffn(x, y, gt1, sc2, sh2, gt2, scf, shf, gffn, gfin, wout, wg, wu, wo):
    b_sz, t_len, d = x.shape
    tm = TOKENS_FFN
    assert t_len % tm == 0

    def tok(width):
        return pl.BlockSpec((1, tm, width), lambda b, t: (b, t, 0))

    def per_batch(width):
        return pl.BlockSpec((1, 1, width), lambda b, t: (b, 0, 0))

    def const(arr):
        return pl.BlockSpec(arr.shape, lambda b, t: (0,) * arr.ndim,
                            pipeline_mode=pl.Buffered(1))

    return pl.pallas_call(
        _out_ffn_kernel,
        grid=(b_sz, t_len // tm),
        in_specs=[tok(d), tok(d)] + [per_batch(d)] * 6
                 + [const(gffn), const(gfin), const(wout), const(wg), const(wu), const(wo)],
        out_specs=tok(d),
        out_shape=jax.ShapeDtypeStruct((b_sz, t_len, d), F32),
        compiler_params=pltpu.CompilerParams(
            dimension_semantics=("arbitrary", "arbitrary"), vmem_limit_bytes=VMEM_LIMIT_BYTES),
        name="out_ffn",
    )(x, y, gt1, sc2, sh2, gt2, scf, shf, gffn, gfin, wout, wg, wu, wo)


def kernel(x, c, w_ada, b_ada, g_mix, w_in, conv_w, conv_b, ln_g, ln_b, w_a2, b_a2,
           gla_norm_g, w_out, g_ffn, w_ffn_in, w_ffn_out, w_ada_final, b_ada_final, g_final):
    b_sz, t_len, d = x.shape
    assert w_ada.shape[0] == 1, "single-layer block"

    c_pad = jnp.pad(c, ((0, SUBLANES - b_sz), (0, 0)))
    mod = _modulation(c_pad, w_ada[0], b_ada[0])[:b_sz]
    fmod = _modulation(c_pad, w_ada_final, b_ada_final)[:b_sz]
    sh1, sc1, gt1, sh2, sc2, gt2 = [m.reshape(b_sz, 1, d) for m in jnp.split(mod, 6, axis=-1)]
    shf, scf = [m.reshape(b_sz, 1, d) for m in jnp.split(fmod, 2, axis=-1)]

    w = w_in[0]
    o_cg, o_q, o_k = D_CONV, 2 * D_CONV, 2 * D_CONV + D_GLA_K
    o_v = o_k + D_GLA_K
    o_og = o_v + D_GLA_V
    o_a = o_og + D_GLA_V
    wcv, wcg = w[:, :o_cg].astype(BF16), w[:, o_cg:o_q].astype(BF16)
    wq, wk = w[:, o_q:o_k].astype(BF16), w[:, o_k:o_v].astype(BF16)
    wv, wog = w[:, o_v:o_og].astype(BF16), w[:, o_og:o_a].astype(BF16)
    wa = jnp.pad(w[:, o_a:], ((0, 0), (0, LANES - GATE_RANK))).astype(BF16)
    wa2 = jnp.pad(w_a2[0], ((0, LANES - GATE_RANK), (0, 0))).astype(BF16)

    u, q, k, v, gate, la = _in_proj(
        x, sc1, sh1, g_mix[0].reshape(1, d), wcv, wcg, wq, wk, wv, wog, wa, wa2,
        b_a2[0].reshape(1, D_GLA_K))

    cw = jnp.broadcast_to(conv_w[0][:, None, :], (CONV_KERNEL, SUBLANES, D_CONV))
    y = _mixer(u, q, k, la, v, gate, cw, conv_b[0].reshape(1, D_CONV),
               ln_g[0].reshape(1, D_CONV), ln_b[0].reshape(1, D_CONV),
               jnp.tile(gla_norm_g[0], GLA_HEADS).reshape(1, D_GLA_V))

    wf = w_ffn_in[0]
    return _out_ffn(
        x, y, gt1, sc2, sh2, gt2, scf, shf, g_ffn[0].reshape(1, d), g_final.reshape(1, d),
        w_out[0].astype(BF16), wf[:, :D_FF].astype(BF16), wf[:, D_FF:].astype(BF16),
        w_ffn_out[0].astype(BF16))
```

```python
import functools

import jax
import jax.numpy as jnp
from jax import lax
from jax.experimental import pallas as pl
from jax.experimental.pallas import tpu as pltpu

F32 = jnp.float32
BF16 = jnp.bfloat16

D_MODEL = 1024
D_CONV = 512
CONV_KERNEL = 31
D_GLA_V = 512
D_GLA_K = 256
GLA_HEADS = 4
HEAD_K = D_GLA_K // GLA_HEADS
HEAD_V = D_GLA_V // GLA_HEADS
GATE_RANK = 16
GATE_TAU = 16.0
D_FF = 2816
EPS = 1e-6

LANES = 128
SUBLANES = 8
MXU_DIM = 256
VMEM_LIMIT_BYTES = 56 * 1024 * 1024

MOD_COLS = 1024
TOKENS_IN = 512
TOKENS_MIX = 512
TOKENS_FFN = 512
CONV_HALO = 32
CONV_ROWS = 64
CONV_PARTIALS = 2
NORM_ROWS = 128
GLA_CHUNK = 128
GLA_MAX_FACTORISED_DECAY = 40.0
FFN_ROWS = 256
FFN_SLABS = ((0, 6 * MXU_DIM), (6 * MXU_DIM, D_FF))


def _sigmoid(x):
    return 1.0 / (1.0 + jnp.exp(-x))


def _dot(a, b):
    return jnp.dot(a, b, preferred_element_type=F32)


def _dot_nt(a, b):
    return lax.dot_general(a, b, (((1,), (1,)), ((), ())), preferred_element_type=F32)


def _dot_tn(a, b):
    return lax.dot_general(a, b, (((0,), (0,)), ((), ())), preferred_element_type=F32)


def _split3(a):
    hi = a.astype(BF16)
    r1 = a - hi.astype(F32)
    mid = r1.astype(BF16)
    lo = (r1 - mid.astype(F32)).astype(BF16)
    return hi, mid, lo


def _modulation_kernel(n_layer_tiles, c_ref, w_ref, b_ref, wf_ref, bf_ref, o_ref):
    n_batch = c_ref.shape[0]
    c_act = []
    for b in range(n_batch):
        cb = c_ref[b]
        c_act.append(cb * _sigmoid(cb))

    def columns(w_ref, b_ref):
        outs = [[] for _ in range(n_batch)]
        for t in range(MOD_COLS // LANES):
            w = w_ref[:, t * LANES:(t + 1) * LANES]
            for b in range(n_batch):
                outs[b].append(jnp.sum(w * c_act[b], axis=0, keepdims=True))
        for b in range(n_batch):
            o_ref[b:b + 1, :] = jnp.concatenate(outs[b], axis=1) + b_ref[...]

    @pl.when(pl.program_id(0) < n_layer_tiles)
    def _():
        columns(w_ref, b_ref)

    @pl.when(pl.program_id(0) >= n_layer_tiles)
    def _():
        columns(wf_ref, bf_ref)


def _modulation(c_lanes, w, b, wf, bf):
    n_batch, d, _ = c_lanes.shape
    n, nf = w.shape[1], wf.shape[1]
    assert n % MOD_COLS == 0 and nf % MOD_COLS == 0
    nt, ntf = n // MOD_COLS, nf // MOD_COLS
    return pl.pallas_call(
        functools.partial(_modulation_kernel, nt),
        grid=(nt + ntf,),
        in_specs=[
            pl.BlockSpec((n_batch, d, LANES), lambda j: (0, 0, 0)),
            pl.BlockSpec((d, MOD_COLS), lambda j: (0, jnp.minimum(j, nt - 1))),
            pl.BlockSpec((1, MOD_COLS), lambda j: (0, jnp.minimum(j, nt - 1))),
            pl.BlockSpec((d, MOD_COLS), lambda j: (0, jnp.maximum(j - nt, 0))),
            pl.BlockSpec((1, MOD_COLS), lambda j: (0, jnp.maximum(j - nt, 0))),
        ],
        out_specs=pl.BlockSpec((n_batch, MOD_COLS), lambda j: (0, j)),
        out_shape=jax.ShapeDtypeStruct((n_batch, n + nf), F32),
        compiler_params=pltpu.CompilerParams(
            dimension_semantics=("arbitrary",), vmem_limit_bytes=VMEM_LIMIT_BYTES),
        name="modulation",
    )(c_lanes, w, b.reshape(1, n), wf, bf.reshape(1, nf))


def _in_proj_kernel(x_ref, sc_ref, sh_ref, g_ref, wcv_ref, wcg_ref, wq_ref, wk_ref,
                    wv_ref, wog_ref, wa_ref, wa2_ref, ba2_ref,
                    u_ref, q_ref, k_ref, v_ref, gate_ref, la_ref):
    x = x_ref[0]
    ms = jnp.mean(x * x, axis=-1, keepdims=True)
    h = x * lax.rsqrt(ms + EPS) * g_ref[...]
    h = h * (1.0 + sc_ref[0]) + sh_ref[0]
    hb = h.astype(BF16)

    cv = _dot(hb, wcv_ref[...])
    cg = _dot(hb, wcg_ref[...])
    u_ref[0] = cv * _sigmoid(cg)
    q_ref[0] = _dot(hb, wq_ref[...]) * (HEAD_K ** -0.5)
    k_ref[0] = _dot(hb, wk_ref[...])
    v_ref[0] = _dot(hb, wv_ref[...])
    og = _dot(hb, wog_ref[...])
    gate_ref[0] = og * _sigmoid(og)

    a_low = _dot(hb, wa_ref[...])
    z = _dot(a_low.astype(BF16), wa2_ref[...]) + ba2_ref[...]
    log_sig = jnp.minimum(z, 0.0) - jnp.log(1.0 + jnp.exp(-jnp.abs(z)))
    la_ref[0] = log_sig * (1.0 / GATE_TAU)


def _in_proj(x, sc, sh, g, wcv, wcg, wq, wk, wv, wog, wa, wa2, ba2):
    b_sz, t_len, d = x.shape
    tm = TOKENS_IN
    assert t_len % tm == 0

    def tok(width):
        return pl.BlockSpec((1, tm, width), lambda b, t: (b, t, 0))

    def per_batch(width):
        return pl.BlockSpec((1, 1, width), lambda b, t: (b, 0, 0))

    def const(arr):
        return pl.BlockSpec(arr.shape, lambda b, t: (0,) * arr.ndim)

    def out(width):
        return jax.ShapeDtypeStruct((b_sz, t_len, width), F32)

    return pl.pallas_call(
        _in_proj_kernel,
        grid=(b_sz, t_len // tm),
        in_specs=[tok(d), per_batch(d), per_batch(d), const(g), const(wcv), const(wcg),
                  const(wq), const(wk), const(wv), const(wog), const(wa), const(wa2),
                  const(ba2)],
        out_specs=[tok(D_CONV), tok(D_GLA_K), tok(D_GLA_K), tok(D_GLA_V), tok(D_GLA_V),
                   tok(D_GLA_K)],
        out_shape=[out(D_CONV), out(D_GLA_K), out(D_GLA_K), out(D_GLA_V), out(D_GLA_V),
                   out(D_GLA_K)],
        compiler_params=pltpu.CompilerParams(
            dimension_semantics=("arbitrary", "arbitrary"), vmem_limit_bytes=VMEM_LIMIT_BYTES),
        name="in_proj",
    )(x, sc, sh, g, wcv, wcg, wq, wk, wv, wog, wa, wa2, ba2)


def _mixer_kernel(u_ref, q_ref, k_ref, la_ref, v_ref, gate_ref, cw_ref, cb_ref,
                  lng_ref, lnb_ref, gng_ref, y_ref, win_ref, pre_ref, st_ref, cum_ref):
    tb = u_ref.shape[1]
    lc = GLA_CHUNK

    @pl.when(pl.program_id(1) == 0)
    def _():
        win_ref[...] = jnp.zeros_like(win_ref)
        st_ref[...] = jnp.zeros_like(st_ref)

    for c in range(D_CONV // LANES):
        u_lanes = u_ref[0, :, c * LANES:(c + 1) * LANES]
        for phase in range(SUBLANES):
            win_ref[phase, c, CONV_HALO - phase:CONV_HALO - phase + tb, :] = u_lanes
    first_tap = CONV_HALO - (CONV_KERNEL - 1)

    for c in range(D_CONV // LANES):
        lanes = slice(c * LANES, (c + 1) * LANES)
        taps = []
        for phase in range(SUBLANES):
            for base in range(0, CONV_HALO + SUBLANES, SUBLANES):
                tap = base + phase - first_tap
                if 0 <= tap < CONV_KERNEL:
                    taps.append((cw_ref[tap, :, lanes], phase, base))
        bias = cb_ref[:, lanes]

        def conv_tile(i, carry, lanes=lanes, taps=taps, bias=bias):
            r0 = pl.multiple_of(i * CONV_ROWS, CONV_ROWS)
            parts = [jnp.broadcast_to(bias, (CONV_ROWS, LANES))] + [None] * (CONV_PARTIALS - 1)
            for n, (w8, phase, base) in enumerate(taps):
                w_rows = jnp.concatenate([w8] * (CONV_ROWS // SUBLANES), axis=0)
                term = w_rows * win_ref[phase, c, pl.ds(r0 + base, CONV_ROWS), :]
                p = n % CONV_PARTIALS
                parts[p] = term if parts[p] is None else parts[p] + term
            pre_ref[pl.ds(r0, CONV_ROWS), lanes] = functools.reduce(lambda a, b: a + b, parts)
            return carry

        lax.fori_loop(0, tb // CONV_ROWS, conv_tile, 0)

    def norm_tile(i, carry):
        r0 = pl.multiple_of(i * NORM_ROWS, NORM_ROWS)
        acc = pre_ref[pl.ds(r0, NORM_ROWS), :]
        mu = jnp.mean(acc, axis=-1, keepdims=True)
        cen = acc - mu
        var = jnp.mean(cen * cen, axis=-1, keepdims=True)
        yn = cen * lax.rsqrt(var + EPS) * lng_ref[...] + lnb_ref[...]
        y_ref[0, pl.ds(r0, NORM_ROWS), 0:D_CONV] = (yn * _sigmoid(yn)).astype(y_ref.dtype)
        return carry

    lax.fori_loop(0, tb // NORM_ROWS, norm_tile, 0)
    win_ref[:, :, 0:CONV_HALO, :] = win_ref[:, :, tb:tb + CONV_HALO, :]

    n_chunks = tb // lc
    row = lax.broadcasted_iota(jnp.int32, (lc, lc), 0)
    col = lax.broadcasted_iota(jnp.int32, (lc, lc), 1)
    tril = (col <= row).astype(BF16)
    klane = lax.broadcasted_iota(jnp.int32, (1, D_GLA_K), 1) // HEAD_K
    srow = lax.broadcasted_iota(jnp.int32, (GLA_HEADS * lc, lc), 0)
    scol = lax.broadcasted_iota(jnp.int32, (GLA_HEADS * lc, lc), 1)
    causal = scol <= (srow % lc)
    st_row = lax.broadcasted_iota(jnp.int32, (D_GLA_V, D_GLA_K), 0) // HEAD_V
    st_col = lax.broadcasted_iota(jnp.int32, (D_GLA_V, D_GLA_K), 1) // HEAD_K
    st_mask = (st_row == st_col).astype(F32)
    rows1 = lax.broadcasted_iota(jnp.int32, (lc, 1), 0)

    decay = None
    for ci in range(n_chunks):
        a_hi, a_mid, a_lo = _split3(la_ref[0, ci * lc:(ci + 1) * lc, :])
        cum = _dot(tril, a_hi) + _dot(tril, a_mid) + _dot(tril, a_lo)
        cum_ref[ci * lc:(ci + 1) * lc, :] = cum
        total = -cum[lc - 1:lc, :]
        decay = total if decay is None else jnp.maximum(decay, total)
    max_decay = jnp.max(decay)

    def intra_factorised(ci, q, k, cum):
        ref = cum[lc // 2 - 1:lc // 2, :]
        qt = (q * jnp.exp(cum - ref)).astype(BF16)
        kt = (k * jnp.exp(ref - cum)).astype(BF16)
        qbd = jnp.concatenate(
            [jnp.where(klane == h, qt, jnp.zeros_like(qt)) for h in range(GLA_HEADS)], axis=0)
        s = _dot_nt(qbd, kt)
        s = jnp.where(causal, s, 0.0).astype(BF16)
        heads = []
        for h in range(GLA_HEADS):
            vh = v_ref[0, ci * lc:(ci + 1) * lc, h * HEAD_V:(h + 1) * HEAD_V].astype(BF16)
            heads.append(_dot(s[h * lc:(h + 1) * lc, :], vh))
        return jnp.concatenate(heads, axis=-1)

    def intra_exact(ci, q, k, cum):
        e_row = lax.broadcasted_iota(jnp.int32, (D_GLA_K, D_GLA_V), 0) // HEAD_K
        e_col = lax.broadcasted_iota(jnp.int32, (D_GLA_K, D_GLA_V), 1) // HEAD_V
        expand = (e_row == e_col).astype(BF16)

        def key_row(j, o):
            kj = k_ref[0, pl.ds(ci * lc + j, 1), :]
            cj = cum_ref[pl.ds(ci * lc + j, 1), :]
            vj = v_ref[0, pl.ds(ci * lc + j, 1), :]
            p = q * kj * jnp.exp(jnp.minimum(cum - cj, 0.0))
            p = jnp.where(rows1 >= j, p, 0.0).astype(BF16)
            return o + _dot(p, expand) * vj

        return lax.fori_loop(0, lc, key_row, jnp.zeros((lc, D_GLA_V), F32))

    def gla_tile(intra):
        for ci in range(n_chunks):
            rows = slice(ci * lc, (ci + 1) * lc)
            q = q_ref[0, rows, :]
            k = k_ref[0, rows, :]
            cum = cum_ref[rows, :]
            cum_last = cum[lc - 1:lc, :]
            o_intra = intra(ci, q, k, cum)

            st = st_ref[...]
            o_inter = _dot_nt((q * jnp.exp(cum)).astype(BF16), st.astype(BF16))
            kh = (k * jnp.exp(cum_last - cum)).astype(BF16)
            vb = v_ref[0, rows, :].astype(BF16)
            st_ref[...] = st * jnp.exp(cum_last) + _dot_tn(vb, kh) * st_mask

            o = o_inter + o_intra
            normed = []
            for h in range(GLA_HEADS):
                oh = o[:, h * HEAD_V:(h + 1) * HEAD_V]
                ms = jnp.mean(oh * oh, axis=-1, keepdims=True)
                normed.append(oh * lax.rsqrt(ms + EPS))
            on = jnp.concatenate(normed, axis=-1) * gng_ref[...]
            y_gla = on * gate_ref[0, rows, :]
            y_ref[0, rows, D_CONV:D_CONV + D_GLA_V] = y_gla.astype(y_ref.dtype)

    @pl.when(max_decay <= GLA_MAX_FACTORISED_DECAY)
    def _():
        gla_tile(intra_factorised)

    @pl.when(max_decay > GLA_MAX_FACTORISED_DECAY)
    def _():
        gla_tile(intra_exact)


def _mixer(u, q, k, la, v, gate, cw, cb, lng, lnb, gng):
    b_sz, t_len, _ = u.shape
    tb = TOKENS_MIX
    assert t_len % tb == 0 and tb % GLA_CHUNK == 0 and tb % CONV_ROWS == 0 and tb % NORM_ROWS == 0

    def tok(width):
        return pl.BlockSpec((1, tb, width), lambda b, t: (b, t, 0))

    def const(arr):
        return pl.BlockSpec(arr.shape, lambda b, t: (0,) * arr.ndim)

    return pl.pallas_call(
        _mixer_kernel,
        grid=(b_sz, t_len // tb),
        in_specs=[tok(D_CONV), tok(D_GLA_K), tok(D_GLA_K), tok(D_GLA_K), tok(D_GLA_V),
                  tok(D_GLA_V), const(cw), const(cb), const(lng), const(lnb), const(gng)],
        out_specs=tok(D_CONV + D_GLA_V),
        out_shape=jax.ShapeDtypeStruct((b_sz, t_len, D_CONV + D_GLA_V), BF16),
        scratch_shapes=[
            pltpu.VMEM((SUBLANES, D_CONV // LANES, CONV_HALO + tb, LANES), F32),
            pltpu.VMEM((tb, D_CONV), F32),
            pltpu.VMEM((D_GLA_V, D_GLA_K), F32),
            pltpu.VMEM((tb, D_GLA_K), F32),
        ],
        compiler_params=pltpu.CompilerParams(
            dimension_semantics=("arbitrary", "arbitrary"), vmem_limit_bytes=VMEM_LIMIT_BYTES),
        name="mixer",
    )(u, q, k, la, v, gate, cw, cb, lng, lnb, gng)


def _out_ffn_kernel(x_ref, y_ref, gt1_ref, sc2_ref, sh2_ref, gt2_ref, scf_ref, shf_ref,
                    gffn_ref, gfin_ref, wout_ref, wg_ref, wu_ref, wo_ref, o_ref):
    tm = x_ref.shape[1]
    groups = [slice(r, r + FFN_ROWS) for r in range(0, tm, FFN_ROWS)]

    def out_proj(rows):
        return x_ref[0, rows, :] + gt1_ref[0] * _dot(y_ref[0, rows, :], wout_ref[...])

    def ffn_input(x1):
        ms = jnp.mean(x1 * x1, axis=-1, keepdims=True)
        h = x1 * lax.rsqrt(ms + EPS) * gffn_ref[...]
        return (h * (1.0 + sc2_ref[0]) + sh2_ref[0]).astype(BF16)

    def ffn(x1, hb):
        acc = jnp.zeros_like(x1)
        for lo, hi in FFN_SLABS:
            gate = _dot(hb, wg_ref[:, lo:hi])
            up = _dot(hb, wu_ref[:, lo:hi])
            act = (gate * _sigmoid(gate) * up).astype(BF16)
            acc = acc + _dot(act, wo_ref[lo:hi, :])
        return x1 + gt2_ref[0] * acc

    def final_norm(rows, x2):
        ms = jnp.mean(x2 * x2, axis=-1, keepdims=True)
        xn = x2 * lax.rsqrt(ms + EPS) * gfin_ref[...]
        o_ref[0, rows, :] = xn * (1.0 + scf_ref[0]) + shf_ref[0]

    x1s = [out_proj(rows) for rows in groups]
    pending = None
    for rows, x1 in zip(groups, x1s):
        hb = ffn_input(x1)
        if pending is not None:
            final_norm(*pending)
        pending = (rows, ffn(x1, hb))
    final_norm(*pending)


def _out_ffn(x, y, gt1, sc2, sh2, gt2, scf, shf, gffn, gfin, wout, wg, wu, wo):
    b_sz, t_len, d = x.shape
    tm = TOKENS_FFN
    assert t_len % tm == 0

    def tok(width):
        return pl.BlockSpec((1, tm, width), lambda b, t: (b, t, 0))

    def per_batch(width):
        return pl.BlockSpec((1, 1, width), lambda b, t: (b, 0, 0))

    def const(arr):
        return pl.BlockSpec(arr.shape, lambda b, t: (0,) * arr.ndim,
                            pipeline_mode=pl.Buffered(1))

    return pl.pallas_call(
        _out_ffn_kernel,
        grid=(b_sz, t_len // tm),
        in_specs=[tok(d), tok(d)] + [per_batch(d)] * 6
                 + [const(gffn), const(gfin), const(wout), const(wg), const(wu), const(wo)],
        out_specs=tok(d),
        out_shape=jax.ShapeDtypeStruct((b_sz, t_len, d), F32),
        compiler_params=pltpu.CompilerParams(
            dimension_semantics=("arbitrary", "arbitrary"), vmem_limit_bytes=VMEM_LIMIT_BYTES),
        name="out_ffn",
    )(x, y, gt1, sc2, sh2, gt2, scf, shf, gffn, gfin, wout, wg, wu, wo)


def kernel(x, c, w_ada, b_ada, g_mix, w_in, conv_w, conv_b, ln_g, ln_b, w_a2, b_a2,
           gla_norm_g, w_out, g_ffn, w_ffn_in, w_ffn_out, w_ada_final, b_ada_final, g_final):
    b_sz, t_len, d = x.shape
    assert w_ada.shape[0] == 1, "single-layer block"

    c_lanes = jnp.broadcast_to(c[:, :, None], (b_sz, d, LANES))
    mod = _modulation(c_lanes, w_ada[0], b_ada[0], w_ada_final, b_ada_final)
    sh1, sc1, gt1, sh2, sc2, gt2, shf, scf = [
        m.reshape(b_sz, 1, d) for m in jnp.split(mod, 8, axis=-1)]

    w = w_in[0]
    o_cg, o_q, o_k = D_CONV, 2 * D_CONV, 2 * D_CONV + D_GLA_K
    o_v = o_k + D_GLA_K
    o_og = o_v + D_GLA_V
    o_a = o_og + D_GLA_V
    wcv, wcg = w[:, :o_cg].astype(BF16), w[:, o_cg:o_q].astype(BF16)
    wq, wk = w[:, o_q:o_k].astype(BF16), w[:, o_k:o_v].astype(BF16)
    wv, wog = w[:, o_v:o_og].astype(BF16), w[:, o_og:o_a].astype(BF16)
    wa = jnp.pad(w[:, o_a:], ((0, 0), (0, LANES - GATE_RANK))).astype(BF16)
    wa2 = jnp.pad(w_a2[0], ((0, LANES - GATE_RANK), (0, 0))).astype(BF16)

    u, q, k, v, gate, la = _in_proj(
        x, sc1, sh1, g_mix[0].reshape(1, d), wcv, wcg, wq, wk, wv, wog, wa, wa2,
        b_a2[0].reshape(1, D_GLA_K))

    cw = jnp.broadcast_to(conv_w[0][:, None, :], (CONV_KERNEL, SUBLANES, D_CONV))
    y = _mixer(u, q, k, la, v, gate, cw, conv_b[0].reshape(1, D_CONV),
               ln_g[0].reshape(1, D_CONV), ln_b[0].reshape(1, D_CONV),
               jnp.tile(gla_norm_g[0], GLA_HEADS).reshape(1, D_GLA_V))

    wf = w_ffn_in[0]
    return _out_ffn(
        x, y, gt1, sc2, sh2, gt2, scf, shf, g_ffn[0].reshape(1, d), g_final.reshape(1, d),
        w_out[0].astype(BF16), wf[:, :D_FF].astype(BF16), wf[:, D_FF:].astype(BF16),
        w_ffn_out[0].astype(BF16))
```

```python
import functools

import jax
import jax.numpy as jnp
from jax import lax
from jax.experimental import pallas as pl
from jax.experimental.pallas import tpu as pltpu

F32 = jnp.float32
BF16 = jnp.bfloat16

D_MODEL = 1024
D_CONV = 512
CONV_KERNEL = 31
D_GLA_V = 512
D_GLA_K = 256
GLA_HEADS = 4
HEAD_K = D_GLA_K // GLA_HEADS
HEAD_V = D_GLA_V // GLA_HEADS
GATE_RANK = 16
GATE_TAU = 16.0
D_FF = 2816
EPS = 1e-6

LANES = 128
SUBLANES = 8
MXU_DIM = 256
VMEM_LIMIT_BYTES = 56 * 1024 * 1024

MOD_COLS = 1024
TOKENS_IN = 512
TOKENS_MIX = 512
TOKENS_FFN = 512
CONV_HALO = 32
CONV_ROWS = 64
CONV_PARTIALS = 2
NORM_ROWS = 128
GLA_CHUNK = 128
GLA_MAX_FACTORISED_DECAY = 40.0
FFN_ROWS = 256
FFN_SLABS = ((0, 6 * MXU_DIM), (6 * MXU_DIM, D_FF))

MOD_SH1, MOD_SC1, MOD_GT1, MOD_SH2, MOD_SC2, MOD_GT2, MOD_SHF, MOD_SCF = range(8)


def _sigmoid(x):
    return 1.0 / (1.0 + jnp.exp(-x))


def _dot(a, b):
    return jnp.dot(a, b, preferred_element_type=F32)


def _dot_nt(a, b):
    return lax.dot_general(a, b, (((1,), (1,)), ((), ())), preferred_element_type=F32)


def _dot_tn(a, b):
    return lax.dot_general(a, b, (((0,), (0,)), ((), ())), preferred_element_type=F32)


def _split3(a):
    hi = a.astype(BF16)
    r1 = a - hi.astype(F32)
    mid = r1.astype(BF16)
    lo = (r1 - mid.astype(F32)).astype(BF16)
    return hi, mid, lo


def _modulation_kernel(n_layer_tiles, c_ref, w_ref, b_ref, wf_ref, bf_ref, o_ref):
    n_batch = c_ref.shape[0]
    c_act = []
    for b in range(n_batch):
        cb = c_ref[b]
        c_act.append(cb * _sigmoid(cb))

    def columns(w_ref, b_ref):
        outs = [[] for _ in range(n_batch)]
        for t in range(MOD_COLS // LANES):
            w = w_ref[:, t * LANES:(t + 1) * LANES]
            for b in range(n_batch):
                outs[b].append(jnp.sum(w * c_act[b], axis=0, keepdims=True))
        for b in range(n_batch):
            o_ref[b:b + 1, :] = jnp.concatenate(outs[b], axis=1) + b_ref[...]

    @pl.when(pl.program_id(0) < n_layer_tiles)
    def _():
        columns(w_ref, b_ref)

    @pl.when(pl.program_id(0) >= n_layer_tiles)
    def _():
        columns(wf_ref, bf_ref)


def _modulation(c_lanes, w, b, wf, bf):
    n_batch, d, _ = c_lanes.shape
    n, nf = w.shape[1], wf.shape[1]
    assert n % MOD_COLS == 0 and nf % MOD_COLS == 0
    nt, ntf = n // MOD_COLS, nf // MOD_COLS
    return pl.pallas_call(
        functools.partial(_modulation_kernel, nt),
        grid=(nt + ntf,),
        in_specs=[
            pl.BlockSpec((n_batch, d, LANES), lambda j: (0, 0, 0)),
            pl.BlockSpec((d, MOD_COLS), lambda j: (0, jnp.minimum(j, nt - 1))),
            pl.BlockSpec((1, MOD_COLS), lambda j: (0, jnp.minimum(j, nt - 1))),
            pl.BlockSpec((d, MOD_COLS), lambda j: (0, jnp.maximum(j - nt, 0))),
            pl.BlockSpec((1, MOD_COLS), lambda j: (0, jnp.maximum(j - nt, 0))),
        ],
        out_specs=pl.BlockSpec((n_batch, MOD_COLS), lambda j: (0, j)),
        out_shape=jax.ShapeDtypeStruct((n_batch, n + nf), F32),
        compiler_params=pltpu.CompilerParams(
            dimension_semantics=("arbitrary",), vmem_limit_bytes=VMEM_LIMIT_BYTES),
        name="modulation",
    )(c_lanes, w, b.reshape(1, n), wf, bf.reshape(1, nf))


def _mod_spec(which):
    return pl.BlockSpec((1, 1, D_MODEL), lambda b, t: (b, 0, which))


def _in_proj_kernel(x_ref, sc_ref, sh_ref, g_ref, wcv_ref, wcg_ref, wq_ref, wk_ref,
                    wv_ref, wog_ref, wa_ref, wa2_ref, ba2_ref,
                    u_ref, q_ref, k_ref, v_ref, gate_ref, la_ref):
    x = x_ref[0]
    ms = jnp.mean(x * x, axis=-1, keepdims=True)
    h = x * lax.rsqrt(ms + EPS) * g_ref[...]
    h = h * (1.0 + sc_ref[0]) + sh_ref[0]
    hb = h.astype(BF16)

    cv = _dot(hb, wcv_ref[...])
    cg = _dot(hb, wcg_ref[...])
    u_ref[0] = cv * _sigmoid(cg)
    q_ref[0] = _dot(hb, wq_ref[...]) * (HEAD_K ** -0.5)
    k_ref[0] = _dot(hb, wk_ref[...])
    v_ref[0] = _dot(hb, wv_ref[...])
    og = _dot(hb, wog_ref[...])
    gate_ref[0] = og * _sigmoid(og)

    a_low = _dot(hb, wa_ref[...])
    z = _dot(a_low.astype(BF16), wa2_ref[...]) + ba2_ref[...]
    log_sig = jnp.minimum(z, 0.0) - jnp.log(1.0 + jnp.exp(-jnp.abs(z)))
    la_ref[0] = log_sig * (1.0 / GATE_TAU)


def _in_proj(x, mod, g, wcv, wcg, wq, wk, wv, wog, wa, wa2, ba2):
    b_sz, t_len, d = x.shape
    tm = TOKENS_IN
    assert t_len % tm == 0

    def tok(width):
        return pl.BlockSpec((1, tm, width), lambda b, t: (b, t, 0))

    def const(arr):
        return pl.BlockSpec(arr.shape, lambda b, t: (0,) * arr.ndim)

    def out(width):
        return jax.ShapeDtypeStruct((b_sz, t_len, width), F32)

    return pl.pallas_call(
        _in_proj_kernel,
        grid=(b_sz, t_len // tm),
        in_specs=[tok(d), _mod_spec(MOD_SC1), _mod_spec(MOD_SH1), const(g), const(wcv),
                  const(wcg), const(wq), const(wk), const(wv), const(wog), const(wa),
                  const(wa2), const(ba2)],
        out_specs=[tok(D_CONV), tok(D_GLA_K), tok(D_GLA_K), tok(D_GLA_V), tok(D_GLA_V),
                   tok(D_GLA_K)],
        out_shape=[out(D_CONV), out(D_GLA_K), out(D_GLA_K), out(D_GLA_V), out(D_GLA_V),
                   out(D_GLA_K)],
        compiler_params=pltpu.CompilerParams(
            dimension_semantics=("arbitrary", "arbitrary"), vmem_limit_bytes=VMEM_LIMIT_BYTES),
        name="in_proj",
    )(x, mod, mod, g, wcv, wcg, wq, wk, wv, wog, wa, wa2, ba2)


def _mixer_kernel(u_ref, q_ref, k_ref, la_ref, v_ref, gate_ref, cw_ref, cb_ref,
                  lng_ref, lnb_ref, gng_ref, y_ref, win_ref, pre_ref, st_ref, cum_ref):
    tb = u_ref.shape[1]
    lc = GLA_CHUNK

    @pl.when(pl.program_id(1) == 0)
    def _():
        win_ref[...] = jnp.zeros_like(win_ref)
        st_ref[...] = jnp.zeros_like(st_ref)

    for c in range(D_CONV // LANES):
        u_lanes = u_ref[0, :, c * LANES:(c + 1) * LANES]
        for phase in range(SUBLANES):
            win_ref[phase, c, CONV_HALO - phase:CONV_HALO - phase + tb, :] = u_lanes
    first_tap = CONV_HALO - (CONV_KERNEL - 1)

    for c in range(D_CONV // LANES):
        lanes = slice(c * LANES, (c + 1) * LANES)
        taps = []
        for phase in range(SUBLANES):
            for base in range(0, CONV_HALO + SUBLANES, SUBLANES):
                tap = base + phase - first_tap
                if 0 <= tap < CONV_KERNEL:
                    taps.append((cw_ref[tap, :, lanes], phase, base))
        bias = cb_ref[:, lanes]

        def conv_tile(i, carry, c=c, lanes=lanes, taps=taps, bias=bias):
            r0 = pl.multiple_of(i * CONV_ROWS, CONV_ROWS)
            parts = [jnp.broadcast_to(bias, (CONV_ROWS, LANES))] + [None] * (CONV_PARTIALS - 1)
            for n, (w8, phase, base) in enumerate(taps):
                w_rows = jnp.concatenate([w8] * (CONV_ROWS // SUBLANES), axis=0)
                term = w_rows * win_ref[phase, c, pl.ds(r0 + base, CONV_ROWS), :]
                p = n % CONV_PARTIALS
                parts[p] = term if parts[p] is None else parts[p] + term
            pre_ref[pl.ds(r0, CONV_ROWS), lanes] = functools.reduce(lambda a, b: a + b, parts)
            return carry

        lax.fori_loop(0, tb // CONV_ROWS, conv_tile, 0)

    for r0 in range(0, tb, NORM_ROWS):
        acc = pre_ref[r0:r0 + NORM_ROWS, :]
        mu = jnp.mean(acc, axis=-1, keepdims=True)
        cen = acc - mu
        var = jnp.mean(cen * cen, axis=-1, keepdims=True)
        yn = cen * lax.rsqrt(var + EPS) * lng_ref[...] + lnb_ref[...]
        y_ref[0, r0:r0 + NORM_ROWS, 0:D_CONV] = (yn * _sigmoid(yn)).astype(y_ref.dtype)
    win_ref[:, :, 0:CONV_HALO, :] = win_ref[:, :, tb:tb + CONV_HALO, :]

    n_chunks = tb // lc
    row = lax.broadcasted_iota(jnp.int32, (lc, lc), 0)
    col = lax.broadcasted_iota(jnp.int32, (lc, lc), 1)
    tril = (col <= row).astype(BF16)
    klane = lax.broadcasted_iota(jnp.int32, (1, D_GLA_K), 1) // HEAD_K
    srow = lax.broadcasted_iota(jnp.int32, (GLA_HEADS * lc, lc), 0)
    scol = lax.broadcasted_iota(jnp.int32, (GLA_HEADS * lc, lc), 1)
    causal = scol <= (srow % lc)
    st_row = lax.broadcasted_iota(jnp.int32, (D_GLA_V, D_GLA_K), 0) // HEAD_V
    st_col = lax.broadcasted_iota(jnp.int32, (D_GLA_V, D_GLA_K), 1) // HEAD_K
    st_mask = (st_row == st_col).astype(F32)
    rows1 = lax.broadcasted_iota(jnp.int32, (lc, 1), 0)

    decay = None
    for ci in range(n_chunks):
        a_hi, a_mid, a_lo = _split3(la_ref[0, ci * lc:(ci + 1) * lc, :])
        cum = _dot(tril, a_hi) + _dot(tril, a_mid) + _dot(tril, a_lo)
        cum_ref[ci * lc:(ci + 1) * lc, :] = cum
        total = -cum[lc - 1:lc, :]
        decay = total if decay is None else jnp.maximum(decay, total)
    max_decay = jnp.max(decay)

    def intra_factorised(ci, q, k, cum):
        ref = cum[lc // 2 - 1:lc // 2, :]
        qt = (q * jnp.exp(cum - ref)).astype(BF16)
        kt = (k * jnp.exp(ref - cum)).astype(BF16)
        qbd = jnp.concatenate(
            [jnp.where(klane == h, qt, jnp.zeros_like(qt)) for h in range(GLA_HEADS)], axis=0)
        s = _dot_nt(qbd, kt)
        s = jnp.where(causal, s, 0.0).astype(BF16)
        heads = []
        for h in range(GLA_HEADS):
            vh = v_ref[0, ci * lc:(ci + 1) * lc, h * HEAD_V:(h + 1) * HEAD_V].astype(BF16)
            heads.append(_dot(s[h * lc:(h + 1) * lc, :], vh))
        return jnp.concatenate(heads, axis=-1)

    def intra_exact(ci, q, k, cum):
        e_row = lax.broadcasted_iota(jnp.int32, (D_GLA_K, D_GLA_V), 0) // HEAD_K
        e_col = lax.broadcasted_iota(jnp.int32, (D_GLA_K, D_GLA_V), 1) // HEAD_V
        expand = (e_row == e_col).astype(BF16)

        def key_row(j, o):
            kj = k_ref[0, pl.ds(ci * lc + j, 1), :]
            cj = cum_ref[pl.ds(ci * lc + j, 1), :]
            vj = v_ref[0, pl.ds(ci * lc + j, 1), :]
            p = q * kj * jnp.exp(jnp.minimum(cum - cj, 0.0))
            p = jnp.where(rows1 >= j, p, 0.0).astype(BF16)
            return o + _dot(p, expand) * vj

        return lax.fori_loop(0, lc, key_row, jnp.zeros((lc, D_GLA_V), F32))

    def gla_tile(intra):
        for ci in range(n_chunks):
            rows = slice(ci * lc, (ci + 1) * lc)
            q = q_ref[0, rows, :]
            k = k_ref[0, rows, :]
            cum = cum_ref[rows, :]
            cum_last = cum[lc - 1:lc, :]
            o_intra = intra(ci, q, k, cum)

            st = st_ref[...]
            o_inter = _dot_nt((q * jnp.exp(cum)).astype(BF16), st.astype(BF16))
            kh = (k * jnp.exp(cum_last - cum)).astype(BF16)
            vb = v_ref[0, rows, :].astype(BF16)
            st_ref[...] = st * jnp.exp(cum_last) + _dot_tn(vb, kh) * st_mask

            o = o_inter + o_intra
            normed = []
            for h in range(GLA_HEADS):
                oh = o[:, h * HEAD_V:(h + 1) * HEAD_V]
                ms = jnp.mean(oh * oh, axis=-1, keepdims=True)
                normed.append(oh * lax.rsqrt(ms + EPS))
            on = jnp.concatenate(normed, axis=-1) * gng_ref[...]
            y_gla = on * gate_ref[0, rows, :]
            y_ref[0, rows, D_CONV:D_CONV + D_GLA_V] = y_gla.astype(y_ref.dtype)

    @pl.when(max_decay <= GLA_MAX_FACTORISED_DECAY)
    def _():
        gla_tile(intra_factorised)

    @pl.when(max_decay > GLA_MAX_FACTORISED_DECAY)
    def _():
        gla_tile(intra_exact)


def _mixer(u, q, k, la, v, gate, cw, cb, lng, lnb, gng):
    b_sz, t_len, _ = u.shape
    tb = TOKENS_MIX
    assert t_len % tb == 0 and tb % GLA_CHUNK == 0 and tb % CONV_ROWS == 0 and tb % NORM_ROWS == 0

    def tok(width):
        return pl.BlockSpec((1, tb, width), lambda b, t: (b, t, 0))

    def const(arr):
        return pl.BlockSpec(arr.shape, lambda b, t: (0,) * arr.ndim)

    return pl.pallas_call(
        _mixer_kernel,
        grid=(b_sz, t_len // tb),
        in_specs=[tok(D_CONV), tok(D_GLA_K), tok(D_GLA_K), tok(D_GLA_K), tok(D_GLA_V),
                  tok(D_GLA_V), const(cw), const(cb), const(lng), const(lnb), const(gng)],
        out_specs=tok(D_CONV + D_GLA_V),
        out_shape=jax.ShapeDtypeStruct((b_sz, t_len, D_CONV + D_GLA_V), BF16),
        scratch_shapes=[
            pltpu.VMEM((SUBLANES, D_CONV // LANES, CONV_HALO + tb, LANES), F32),
            pltpu.VMEM((tb, D_CONV), F32),
            pltpu.VMEM((D_GLA_V, D_GLA_K), F32),
            pltpu.VMEM((tb, D_GLA_K), F32),
        ],
        compiler_params=pltpu.CompilerParams(
            dimension_semantics=("arbitrary", "arbitrary"), vmem_limit_bytes=VMEM_LIMIT_BYTES),
        name="mixer",
    )(u, q, k, la, v, gate, cw, cb, lng, lnb, gng)


def _out_ffn_kernel(x_ref, y_ref, gt1_ref, sc2_ref, sh2_ref, gt2_ref, scf_ref, shf_ref,
                    gffn_ref, gfin_ref, wout_ref, wg_ref, wu_ref, wo_ref, o_ref):
    tm = x_ref.shape[1]
    groups = [slice(r, r + FFN_ROWS) for r in range(0, tm, FFN_ROWS)]

    def out_proj(rows):
        return x_ref[0, rows, :] + gt1_ref[0] * _dot(y_ref[0, rows, :], wout_ref[...])

    def ffn_input(x1):
        ms = jnp.mean(x1 * x1, axis=-1, keepdims=True)
        h = x1 * lax.rsqrt(ms + EPS) * gffn_ref[...]
        return (h * (1.0 + sc2_ref[0]) + sh2_ref[0]).astype(BF16)

    def ffn(x1, hb):
        acc = jnp.zeros_like(x1)
        for lo, hi in FFN_SLABS:
            gate = _dot(hb, wg_ref[:, lo:hi])
            up = _dot(hb, wu_ref[:, lo:hi])
            act = (gate * _sigmoid(gate) * up).astype(BF16)
            acc = acc + _dot(act, wo_ref[lo:hi, :])
        return x1 + gt2_ref[0] * acc

    def final_norm(rows, x2):
        ms = jnp.mean(x2 * x2, axis=-1, keepdims=True)
        xn = x2 * lax.rsqrt(ms + EPS) * gfin_ref[...]
        o_ref[0, rows, :] = xn * (1.0 + scf_ref[0]) + shf_ref[0]

    x1s = [out_proj(rows) for rows in groups]
    pending = None
    for rows, x1 in zip(groups, x1s):
        hb = ffn_input(x1)
        if pending is not None:
            final_norm(*pending)
        pending = (rows, ffn(x1, hb))
    final_norm(*pending)


def _out_ffn(x, y, mod, gffn, gfin, wout, wg, wu, wo):
    b_sz, t_len, d = x.shape
    tm = TOKENS_FFN
    assert t_len % tm == 0 and tm % FFN_ROWS == 0

    def tok(width):
        return pl.BlockSpec((1, tm, width), lambda b, t: (b, t, 0))

    def const(arr):
        return pl.BlockSpec(arr.shape, lambda b, t: (0,) * arr.ndim,
                            pipeline_mode=pl.Buffered(1))

    mods = (MOD_GT1, MOD_SC2, MOD_SH2, MOD_GT2, MOD_SCF, MOD_SHF)
    return pl.pallas_call(
        _out_ffn_kernel,
        grid=(b_sz, t_len // tm),
        in_specs=[tok(d), tok(d)] + [_mod_spec(m) for m in mods]
                 + [const(gffn), const(gfin), const(wout), const(wg), const(wu), const(wo)],
        out_specs=tok(d),
        out_shape=jax.ShapeDtypeStruct((b_sz, t_len, d), F32),
        compiler_params=pltpu.CompilerParams(
            dimension_semantics=("arbitrary", "arbitrary"), vmem_limit_bytes=VMEM_LIMIT_BYTES),
        name="out_ffn",
    )(x, y, *([mod] * len(mods)), gffn, gfin, wout, wg, wu, wo)


def kernel(x, c, w_ada, b_ada, g_mix, w_in, conv_w, conv_b, ln_g, ln_b, w_a2, b_a2,
           gla_norm_g, w_out, g_ffn, w_ffn_in, w_ffn_out, w_ada_final, b_ada_final, g_final):
    b_sz, t_len, d = x.shape
    assert w_ada.shape[0] == 1, "single-layer block"
    assert d == D_MODEL

    c_lanes = jnp.broadcast_to(c[:, :, None], (b_sz, d, LANES))
    mod = _modulation(c_lanes, w_ada[0], b_ada[0], w_ada_final, b_ada_final)
    mod = mod.reshape(b_sz, 1, 8 * d)

    w = w_in[0]
    o_cg, o_q, o_k = D_CONV, 2 * D_CONV, 2 * D_CONV + D_GLA_K
    o_v = o_k + D_GLA_K
    o_og = o_v + D_GLA_V
    o_a = o_og + D_GLA_V
    wcv, wcg = w[:, :o_cg].astype(BF16), w[:, o_cg:o_q].astype(BF16)
    wq, wk = w[:, o_q:o_k].astype(BF16), w[:, o_k:o_v].astype(BF16)
    wv, wog = w[:, o_v:o_og].astype(BF16), w[:, o_og:o_a].astype(BF16)
    wa = jnp.pad(w[:, o_a:], ((0, 0), (0, LANES - GATE_RANK))).astype(BF16)
    wa2 = jnp.pad(w_a2[0], ((0, LANES - GATE_RANK), (0, 0))).astype(BF16)

    u, q, k, v, gate, la = _in_proj(
        x, mod, g_mix[0].reshape(1, d), wcv, wcg, wq, wk, wv, wog, wa, wa2,
        b_a2[0].reshape(1, D_GLA_K))

    cw = jnp.broadcast_to(conv_w[0][:, None, :], (CONV_KERNEL, SUBLANES, D_CONV))
    y = _mixer(u, q, k, la, v, gate, cw, conv_b[0].reshape(1, D_CONV),
               ln_g[0].reshape(1, D_CONV), ln_b[0].reshape(1, D_CONV),
               jnp.tile(gla_norm_g[0], GLA_HEADS).reshape(1, D_GLA_V))

    wf = w_ffn_in[0]
    return _out_ffn(
        x, y, mod, g_ffn[0].reshape(1, d), g_final.reshape(1, d),
        w_out[0].astype(BF16), wf[:, :D_FF].astype(BF16), wf[:, D_FF:].astype(BF16),
        w_ffn_out[0].astype(BF16))
```

```python
import functools

import jax
import jax.numpy as jnp
from jax import lax
from jax.experimental import pallas as pl
from jax.experimental.pallas import tpu as pltpu

F32 = jnp.float32
BF16 = jnp.bfloat16

D_MODEL = 1024
D_CONV = 512
CONV_KERNEL = 31
D_GLA_V = 512
D_GLA_K = 256
GLA_HEADS = 4
HEAD_K = D_GLA_K // GLA_HEADS
HEAD_V = D_GLA_V // GLA_HEADS
GATE_RANK = 16
GATE_TAU = 16.0
D_FF = 2816
EPS = 1e-6

LANES = 128
SUBLANES = 8
MXU_DIM = 256
VMEM_LIMIT_BYTES = 56 * 1024 * 1024
MOD_COLS = 1024
TOKENS_IN = 1024
TOKENS_MIX = 1024
TOKENS_FFN = 1024
CONV_HALO = 32
CONV_ROWS = 64
CONV_PARTIALS = 2
NORM_ROWS = 128
GLA_CHUNK = 128
GLA_MAX_FACTORISED_DECAY = 40.0
FFN_ROWS = 256
FFN_SLABS = ((0, 6 * MXU_DIM), (6 * MXU_DIM, D_FF))

IN_COLS = {}
_col = 0
for _name, _width in (("cv", D_CONV), ("cg", D_CONV), ("q", D_GLA_K), ("k", D_GLA_K),
                      ("v", D_GLA_V), ("og", D_GLA_V), ("a", LANES)):
    IN_COLS[_name] = (_col, _col + _width)
    _col += _width

MOD_SH1, MOD_SC1, MOD_GT1, MOD_SH2, MOD_SC2, MOD_GT2, MOD_SHF, MOD_SCF = range(8)


def _sigmoid(x):
    return 1.0 / (1.0 + jnp.exp(-x))


def _dot(a, b):
    return jnp.dot(a, b, preferred_element_type=F32)


def _dot_nt(a, b):
    return lax.dot_general(a, b, (((1,), (1,)), ((), ())), preferred_element_type=F32)


def _dot_tn(a, b):
    return lax.dot_general(a, b, (((0,), (0,)), ((), ())), preferred_element_type=F32)


def _split3(a):
    hi = a.astype(BF16)
    r1 = a - hi.astype(F32)
    mid = r1.astype(BF16)
    lo = (r1 - mid.astype(F32)).astype(BF16)
    return hi, mid, lo


def _modulation_kernel(n_layer_tiles, c_ref, w_ref, b_ref, wf_ref, bf_ref, o_ref):
    n_batch = c_ref.shape[0]
    c_act = []
    for b in range(n_batch):
        cb = c_ref[b]
        c_act.append(cb * _sigmoid(cb))

    def columns(w_ref, b_ref):
        outs = [[] for _ in range(n_batch)]
        for t in range(MOD_COLS // LANES):
            w = w_ref[:, t * LANES:(t + 1) * LANES]
            for b in range(n_batch):
                outs[b].append(jnp.sum(w * c_act[b], axis=0, keepdims=True))
        for b in range(n_batch):
            o_ref[b:b + 1, :] = jnp.concatenate(outs[b], axis=1) + b_ref[...]

    @pl.when(pl.program_id(0) < n_layer_tiles)
    def _():
        columns(w_ref, b_ref)

    @pl.when(pl.program_id(0) >= n_layer_tiles)
    def _():
        columns(wf_ref, bf_ref)


def _modulation(c_lanes, w, b, wf, bf):
    n_batch, d, _ = c_lanes.shape
    n, nf = w.shape[1], wf.shape[1]
    assert n % MOD_COLS == 0 and nf % MOD_COLS == 0
    nt, ntf = n // MOD_COLS, nf // MOD_COLS
    return pl.pallas_call(
        functools.partial(_modulation_kernel, nt),
        grid=(nt + ntf,),
        in_specs=[
            pl.BlockSpec((n_batch, d, LANES), lambda j: (0, 0, 0)),
            pl.BlockSpec((d, MOD_COLS), lambda j: (0, jnp.minimum(j, nt - 1))),
            pl.BlockSpec((1, MOD_COLS), lambda j: (0, jnp.minimum(j, nt - 1))),
            pl.BlockSpec((d, MOD_COLS), lambda j: (0, jnp.maximum(j - nt, 0))),
            pl.BlockSpec((1, MOD_COLS), lambda j: (0, jnp.maximum(j - nt, 0))),
        ],
        out_specs=pl.BlockSpec((n_batch, MOD_COLS), lambda j: (0, j)),
        out_shape=jax.ShapeDtypeStruct((n_batch, n + nf), F32),
        compiler_params=pltpu.CompilerParams(
            dimension_semantics=("arbitrary",), vmem_limit_bytes=VMEM_LIMIT_BYTES),
        name="modulation",
    )(c_lanes, w, b.reshape(1, n), wf, bf.reshape(1, nf))


def _mod_spec(which):
    return pl.BlockSpec((1, 1, D_MODEL), lambda b, t: (b, 0, which))


def _in_proj_kernel(x_ref, sc_ref, sh_ref, g_ref, w_ref, wa2_ref, ba2_ref,
                    u_ref, q_ref, k_ref, v_ref, gate_ref, la_ref):
    x = x_ref[0]
    ms = jnp.mean(x * x, axis=-1, keepdims=True)
    h = x * lax.rsqrt(ms + EPS) * g_ref[...]
    h = h * (1.0 + sc_ref[0]) + sh_ref[0]
    hb = h.astype(BF16)

    proj = _dot(hb, w_ref[...])

    def piece(name):
        lo, hi = IN_COLS[name]
        return proj[:, lo:hi]

    u_ref[0] = piece("cv") * _sigmoid(piece("cg"))
    q_ref[0] = piece("q") * (HEAD_K ** -0.5)
    k_ref[0] = piece("k")
    v_ref[0] = piece("v")
    og = piece("og")
    gate_ref[0] = og * _sigmoid(og)

    a_low = piece("a")
    z = _dot(a_low.astype(BF16), wa2_ref[...]) + ba2_ref[...]
    log_sig = jnp.minimum(z, 0.0) - jnp.log(1.0 + jnp.exp(-jnp.abs(z)))
    la_ref[0] = log_sig * (1.0 / GATE_TAU)


def _in_proj(x, mod, g, w, wa2, ba2):
    b_sz, t_len, d = x.shape
    tm = TOKENS_IN
    assert t_len % tm == 0

    def tok(width):
        return pl.BlockSpec((1, tm, width), lambda b, t: (b, t, 0))

    def const(arr):
        return pl.BlockSpec(arr.shape, lambda b, t: (0,) * arr.ndim)

    def out(width):
        return jax.ShapeDtypeStruct((b_sz, t_len, width), F32)

    return pl.pallas_call(
        _in_proj_kernel,
        grid=(b_sz, t_len // tm),
        in_specs=[tok(d), _mod_spec(MOD_SC1), _mod_spec(MOD_SH1), const(g), const(w),
                  const(wa2), const(ba2)],
        out_specs=[tok(D_CONV), tok(D_GLA_K), tok(D_GLA_K), tok(D_GLA_V), tok(D_GLA_V),
                   tok(D_GLA_K)],
        out_shape=[out(D_CONV), out(D_GLA_K), out(D_GLA_K), out(D_GLA_V), out(D_GLA_V),
                   out(D_GLA_K)],
        compiler_params=pltpu.CompilerParams(
            dimension_semantics=("arbitrary", "arbitrary"), vmem_limit_bytes=VMEM_LIMIT_BYTES),
        name="in_proj",
    )(x, mod, mod, g, w, wa2, ba2)


def _mixer_kernel(u_ref, q_ref, k_ref, la_ref, v_ref, gate_ref, cw_ref, cb_ref,
                  lng_ref, lnb_ref, gng_ref, y_ref, win_ref, pre_ref, st_ref, cum_ref):
    tb = u_ref.shape[1]
    lc = GLA_CHUNK

    @pl.when(pl.program_id(1) == 0)
    def _():
        win_ref[...] = jnp.zeros_like(win_ref)
        st_ref[...] = jnp.zeros_like(st_ref)

    for c in range(D_CONV // LANES):
        u_lanes = u_ref[0, :, c * LANES:(c + 1) * LANES]
        for phase in range(SUBLANES):
            win_ref[phase, c, CONV_HALO - phase:CONV_HALO - phase + tb, :] = u_lanes
    first_tap = CONV_HALO - (CONV_KERNEL - 1)

    for c in range(D_CONV // LANES):
        lanes = slice(c * LANES, (c + 1) * LANES)
        taps = []
        for phase in range(SUBLANES):
            for base in range(0, CONV_HALO + SUBLANES, SUBLANES):
                tap = base + phase - first_tap
                if 0 <= tap < CONV_KERNEL:
                    taps.append((cw_ref[tap, :, lanes], phase, base))
        bias = cb_ref[:, lanes]

        def conv_tile(i, carry, c=c, lanes=lanes, taps=taps, bias=bias):
            r0 = pl.multiple_of(i * CONV_ROWS, CONV_ROWS)
            parts = [jnp.broadcast_to(bias, (CONV_ROWS, LANES))] + [None] * (CONV_PARTIALS - 1)
            for n, (w8, phase, base) in enumerate(taps):
                w_rows = jnp.concatenate([w8] * (CONV_ROWS // SUBLANES), axis=0)
                term = w_rows * win_ref[phase, c, pl.ds(r0 + base, CONV_ROWS), :]
                p = n % CONV_PARTIALS
                parts[p] = term if parts[p] is None else parts[p] + term
            pre_ref[pl.ds(r0, CONV_ROWS), lanes] = functools.reduce(lambda a, b: a + b, parts)
            return carry

        lax.fori_loop(0, tb // CONV_ROWS, conv_tile, 0)

    for r0 in range(0, tb, NORM_ROWS):
        acc = pre_ref[r0:r0 + NORM_ROWS, :]
        mu = jnp.mean(acc, axis=-1, keepdims=True)
        cen = acc - mu
        var = jnp.mean(cen * cen, axis=-1, keepdims=True)
        yn = cen * lax.rsqrt(var + EPS) * lng_ref[...] + lnb_ref[...]
        y_ref[0, r0:r0 + NORM_ROWS, 0:D_CONV] = (yn * _sigmoid(yn)).astype(y_ref.dtype)
    win_ref[:, :, 0:CONV_HALO, :] = win_ref[:, :, tb:tb + CONV_HALO, :]

    n_chunks = tb // lc
    row = lax.broadcasted_iota(jnp.int32, (lc, lc), 0)
    col = lax.broadcasted_iota(jnp.int32, (lc, lc), 1)
    tril = (col <= row).astype(BF16)
    klane = lax.broadcasted_iota(jnp.int32, (1, D_GLA_K), 1) // HEAD_K
    srow = lax.broadcasted_iota(jnp.int32, (GLA_HEADS * lc, lc), 0)
    scol = lax.broadcasted_iota(jnp.int32, (GLA_HEADS * lc, lc), 1)
    causal = scol <= (srow % lc)
    st_row = lax.broadcasted_iota(jnp.int32, (D_GLA_V, D_GLA_K), 0) // HEAD_V
    st_col = lax.broadcasted_iota(jnp.int32, (D_GLA_V, D_GLA_K), 1) // HEAD_K
    st_mask = (st_row == st_col).astype(F32)
    rows1 = lax.broadcasted_iota(jnp.int32, (lc, 1), 0)

    splits = [_split3(la_ref[0, ci * lc:(ci + 1) * lc, :]) for ci in range(n_chunks)]
    cums = [_dot(tril, hi) + _dot(tril, mid) + _dot(tril, lo) for hi, mid, lo in splits]
    decay = None
    for ci, cum in enumerate(cums):
        cum_ref[ci * lc:(ci + 1) * lc, :] = cum
        total = -cum[lc - 1:lc, :]
        decay = total if decay is None else jnp.maximum(decay, total)
    max_decay = jnp.max(decay)

    def intra_factorised(ci, q, k, cum):
        ref = cum[lc // 2 - 1:lc // 2, :]
        qt = (q * jnp.exp(cum - ref)).astype(BF16)
        kt = (k * jnp.exp(ref - cum)).astype(BF16)
        qbd = jnp.concatenate(
            [jnp.where(klane == h, qt, jnp.zeros_like(qt)) for h in range(GLA_HEADS)], axis=0)
        s = _dot_nt(qbd, kt)
        s = jnp.where(causal, s, 0.0).astype(BF16)
        heads = []
        for h in range(GLA_HEADS):
            vh = v_ref[0, ci * lc:(ci + 1) * lc, h * HEAD_V:(h + 1) * HEAD_V].astype(BF16)
            heads.append(_dot(s[h * lc:(h + 1) * lc, :], vh))
        return jnp.concatenate(heads, axis=-1)

    def intra_exact(ci, q, k, cum):
        e_row = lax.broadcasted_iota(jnp.int32, (D_GLA_K, D_GLA_V), 0) // HEAD_K
        e_col = lax.broadcasted_iota(jnp.int32, (D_GLA_K, D_GLA_V), 1) // HEAD_V
        expand = (e_row == e_col).astype(BF16)

        def key_row(j, o):
            kj = k_ref[0, pl.ds(ci * lc + j, 1), :]
            cj = cum_ref[pl.ds(ci * lc + j, 1), :]
            vj = v_ref[0, pl.ds(ci * lc + j, 1), :]
            p = q * kj * jnp.exp(jnp.minimum(cum - cj, 0.0))
            p = jnp.where(rows1 >= j, p, 0.0).astype(BF16)
            return o + _dot(p, expand) * vj

        return lax.fori_loop(0, lc, key_row, jnp.zeros((lc, D_GLA_V), F32))

    def gla_tile(intra):
        for ci in range(n_chunks):
            rows = slice(ci * lc, (ci + 1) * lc)
            q = q_ref[0, rows, :]
            k = k_ref[0, rows, :]
            cum = cum_ref[rows, :]
            cum_last = cum[lc - 1:lc, :]
            o_intra = intra(ci, q, k, cum)

            st = st_ref[...]
            o_inter = _dot_nt((q * jnp.exp(cum)).astype(BF16), st.astype(BF16))
            kh = (k * jnp.exp(cum_last - cum)).astype(BF16)
            vb = v_ref[0, rows, :].astype(BF16)
            st_ref[...] = st * jnp.exp(cum_last) + _dot_tn(vb, kh) * st_mask

            o = o_inter + o_intra
            normed = []
            for h in range(GLA_HEADS):
                oh = o[:, h * HEAD_V:(h + 1) * HEAD_V]
                ms = jnp.mean(oh * oh, axis=-1, keepdims=True)
                normed.append(oh * lax.rsqrt(ms + EPS))
            on = jnp.concatenate(normed, axis=-1) * gng_ref[...]
            y_gla = on * gate_ref[0, rows, :]
            y_ref[0, rows, D_CONV:D_CONV + D_GLA_V] = y_gla.astype(y_ref.dtype)

    @pl.when(max_decay <= GLA_MAX_FACTORISED_DECAY)
    def _():
        gla_tile(intra_factorised)

    @pl.when(max_decay > GLA_MAX_FACTORISED_DECAY)
    def _():
        gla_tile(intra_exact)


def _mixer(u, q, k, la, v, gate, cw, cb, lng, lnb, gng):
    b_sz, t_len, _ = u.shape
    tb = TOKENS_MIX
    assert t_len % tb == 0 and tb % GLA_CHUNK == 0 and tb % CONV_ROWS == 0 and tb % NORM_ROWS == 0

    def tok(width):
        return pl.BlockSpec((1, tb, width), lambda b, t: (b, t, 0))

    def const(arr):
        return pl.BlockSpec(arr.shape, lambda b, t: (0,) * arr.ndim)

    return pl.pallas_call(
        _mixer_kernel,
        grid=(b_sz, t_len // tb),
        in_specs=[tok(D_CONV), tok(D_GLA_K), tok(D_GLA_K), tok(D_GLA_K), tok(D_GLA_V),
                  tok(D_GLA_V), const(cw), const(cb), const(lng), const(lnb), const(gng)],
        out_specs=tok(D_CONV + D_GLA_V),
        out_shape=jax.ShapeDtypeStruct((b_sz, t_len, D_CONV + D_GLA_V), BF16),
        scratch_shapes=[
            pltpu.VMEM((SUBLANES, D_CONV // LANES, CONV_HALO + tb, LANES), F32),
            pltpu.VMEM((tb, D_CONV), F32),
            pltpu.VMEM((D_GLA_V, D_GLA_K), F32),
            pltpu.VMEM((tb, D_GLA_K), F32),
        ],
        compiler_params=pltpu.CompilerParams(
            dimension_semantics=("arbitrary", "arbitrary"), vmem_limit_bytes=VMEM_LIMIT_BYTES),
        name="mixer",
    )(u, q, k, la, v, gate, cw, cb, lng, lnb, gng)


def _out_ffn_kernel(x_ref, y_ref, gt1_ref, sc2_ref, sh2_ref, gt2_ref, scf_ref, shf_ref,
                    gffn_ref, gfin_ref, wout_ref, wg_ref, wu_ref, wo_ref, o_ref):
    tm = x_ref.shape[1]
    groups = [slice(r, r + FFN_ROWS) for r in range(0, tm, FFN_ROWS)]

    def out_proj(rows):
        return x_ref[0, rows, :] + gt1_ref[0] * _dot(y_ref[0, rows, :], wout_ref[...])

    def ffn_input(x1):
        ms = jnp.mean(x1 * x1, axis=-1, keepdims=True)
        h = x1 * lax.rsqrt(ms + EPS) * gffn_ref[...]
        return (h * (1.0 + sc2_ref[0]) + sh2_ref[0]).astype(BF16)

    def ffn(x1, hb):
        acc = jnp.zeros_like(x1)
        for lo, hi in FFN_SLABS:
            gate = _dot(hb, wg_ref[:, lo:hi])
            up = _dot(hb, wu_ref[:, lo:hi])
            act = (gate * _sigmoid(gate) * up).astype(BF16)
            acc = acc + _dot(act, wo_ref[lo:hi, :])
        return x1 + gt2_ref[0] * acc

    def final_norm(rows, x2):
        ms = jnp.mean(x2 * x2, axis=-1, keepdims=True)
        xn = x2 * lax.rsqrt(ms + EPS) * gfin_ref[...]
        o_ref[0, rows, :] = xn * (1.0 + scf_ref[0]) + shf_ref[0]

    x1s = [out_proj(rows) for rows in groups]
    pending = None
    for rows, x1 in zip(groups, x1s):
        hb = ffn_input(x1)
        if pending is not None:
            final_norm(*pending)
        pending = (rows, ffn(x1, hb))
    final_norm(*pending)


def _out_ffn(x, y, mod, gffn, gfin, wout, wg, wu, wo):
    b_sz, t_len, d = x.shape
    tm = TOKENS_FFN
    assert t_len % tm == 0 and tm % FFN_ROWS == 0

    def tok(width):
        return pl.BlockSpec((1, tm, width), lambda b, t: (b, t, 0))

    def const(arr):
        return pl.BlockSpec(arr.shape, lambda b, t: (0,) * arr.ndim,
                            pipeline_mode=pl.Buffered(1))

    mods = (MOD_GT1, MOD_SC2, MOD_SH2, MOD_GT2, MOD_SCF, MOD_SHF)
    return pl.pallas_call(
        _out_ffn_kernel,
        grid=(b_sz, t_len // tm),
        in_specs=[tok(d), tok(d)] + [_mod_spec(m) for m in mods]
                 + [const(gffn), const(gfin), const(wout), const(wg), const(wu), const(wo)],
        out_specs=tok(d),
        out_shape=jax.ShapeDtypeStruct((b_sz, t_len, d), F32),
        compiler_params=pltpu.CompilerParams(
            dimension_semantics=("arbitrary", "arbitrary"), vmem_limit_bytes=VMEM_LIMIT_BYTES),
        name="out_ffn",
    )(x, y, *([mod] * len(mods)), gffn, gfin, wout, wg, wu, wo)


def kernel(x, c, w_ada, b_ada, g_mix, w_in, conv_w, conv_b, ln_g, ln_b, w_a2, b_a2,
           gla_norm_g, w_out, g_ffn, w_ffn_in, w_ffn_out, w_ada_final, b_ada_final, g_final):
    b_sz, t_len, d = x.shape
    assert w_ada.shape[0] == 1, "single-layer block"
    assert d == D_MODEL

    c_lanes = jnp.broadcast_to(c[:, :, None], (b_sz, d, LANES))
    mod = _modulation(c_lanes, w_ada[0], b_ada[0], w_ada_final, b_ada_final)
    mod = mod.reshape(b_sz, 1, 8 * d)

    w = jnp.pad(w_in[0], ((0, 0), (0, LANES - GATE_RANK))).astype(BF16)
    wa2 = jnp.pad(w_a2[0], ((0, LANES - GATE_RANK), (0, 0))).astype(BF16)

    u, q, k, v, gate, la = _in_proj(
        x, mod, g_mix[0].reshape(1, d), w, wa2, b_a2[0].reshape(1, D_GLA_K))

    cw = jnp.broadcast_to(conv_w[0][:, None, :], (CONV_KERNEL, SUBLANES, D_CONV))
    y = _mixer(u, q, k, la, v, gate, cw, conv_b[0].reshape(1, D_CONV),
               ln_g[0].reshape(1, D_CONV), ln_b[0].reshape(1, D_CONV),
               jnp.tile(gla_norm_g[0], GLA_HEADS).reshape(1, D_GLA_V))

    wf = w_ffn_in[0]
    return _out_ffn(
        x, y, mod, g_ffn[0].reshape(1, d), g_final.reshape(1, d),
        w_out[0].astype(BF16), wf[:, :D_FF].astype(BF16), wf[:, D_FF:].astype(BF16),
        w_ffn_out[0].astype(BF16))
```

```python
import functools

import jax
import jax.numpy as jnp
from jax import lax
from jax.experimental import pallas as pl
from jax.experimental.pallas import tpu as pltpu

F32 = jnp.float32
BF16 = jnp.bfloat16

D_MODEL = 1024
D_CONV = 512
CONV_KERNEL = 31
D_GLA_V = 512
D_GLA_K = 256
GLA_HEADS = 4
HEAD_K = D_GLA_K // GLA_HEADS
HEAD_V = D_GLA_V // GLA_HEADS
GATE_RANK = 16
GATE_TAU = 16.0
D_FF = 2816
EPS = 1e-6

LANES = 128
SUBLANES = 8
MXU_DIM = 256
VMEM_LIMIT_BYTES = 56 * 1024 * 1024
MOD_ROWS = 128
TOKENS_IN = 1024
TOKENS_MIX = 1024
TOKENS_FFN = 1024
CONV_HALO = 32
CONV_ROWS = 64
CONV_PARTIALS = 2
NORM_ROWS = 128
GLA_CHUNK = 128
GLA_MAX_FACTORISED_DECAY = 40.0
FFN_ROWS = 256
FFN_SLABS = ((0, 6 * MXU_DIM), (6 * MXU_DIM, D_FF))

IN_COLS = {}
_col = 0
for _name, _width in (("cv", D_CONV), ("cg", D_CONV), ("q", D_GLA_K), ("k", D_GLA_K),
                      ("v", D_GLA_V), ("og", D_GLA_V), ("a", LANES)):
    IN_COLS[_name] = (_col, _col + _width)
    _col += _width

MOD_SH1, MOD_SC1, MOD_GT1, MOD_SH2, MOD_SC2, MOD_GT2, MOD_SHF, MOD_SCF = range(8)


def _sigmoid(x):
    return 1.0 / (1.0 + jnp.exp(-x))


def _dot(a, b):
    return jnp.dot(a, b, preferred_element_type=F32)


def _dot_nt(a, b):
    return lax.dot_general(a, b, (((1,), (1,)), ((), ())), preferred_element_type=F32)


def _dot_tn(a, b):
    return lax.dot_general(a, b, (((0,), (0,)), ((), ())), preferred_element_type=F32)


def _split3(a):
    hi = a.astype(BF16)
    r1 = a - hi.astype(F32)
    mid = r1.astype(BF16)
    lo = (r1 - mid.astype(F32)).astype(BF16)
    return hi, mid, lo


def _modulation_kernel(c_ref, w_ref, b_ref, wf_ref, bf_ref, o_ref, acc_ref):
    j = pl.program_id(0)
    n_batch, k_rows, _ = c_ref.shape
    n = w_ref.shape[1]

    @pl.when(j == 0)
    def _():
        acc_ref[...] = jnp.zeros_like(acc_ref)

    c_act = []
    for b in range(n_batch):
        cb = c_ref[b]
        c_act.append(cb * _sigmoid(cb))
    for ref, offset in ((w_ref, 0), (wf_ref, n)):
        for t in range(ref.shape[1] // LANES):
            w = ref[:, t * LANES:(t + 1) * LANES]
            cols = slice(offset + t * LANES, offset + (t + 1) * LANES)
            for b in range(n_batch):
                prod = (w * c_act[b]).reshape(k_rows // SUBLANES, SUBLANES, LANES)
                acc_ref[b, :, cols] += jnp.sum(prod, axis=0)

    @pl.when(j == pl.num_programs(0) - 1)
    def _():
        bias = jnp.concatenate([b_ref[...], bf_ref[...]], axis=1)
        for b in range(n_batch):
            o_ref[b:b + 1, :] = jnp.sum(acc_ref[b], axis=0, keepdims=True) + bias


def _modulation(c_lanes, w, b, wf, bf):
    n_batch, d, _ = c_lanes.shape
    n, nf = w.shape[1], wf.shape[1]
    assert d % MOD_ROWS == 0 and n % LANES == 0 and nf % LANES == 0
    return pl.pallas_call(
        _modulation_kernel,
        grid=(d // MOD_ROWS,),
        in_specs=[
            pl.BlockSpec((n_batch, MOD_ROWS, LANES), lambda j: (0, j, 0)),
            pl.BlockSpec((MOD_ROWS, n), lambda j: (j, 0)),
            pl.BlockSpec((1, n), lambda j: (0, 0)),
            pl.BlockSpec((MOD_ROWS, nf), lambda j: (j, 0)),
            pl.BlockSpec((1, nf), lambda j: (0, 0)),
        ],
        out_specs=pl.BlockSpec((n_batch, n + nf), lambda j: (0, 0)),
        out_shape=jax.ShapeDtypeStruct((n_batch, n + nf), F32),
        scratch_shapes=[pltpu.VMEM((n_batch, SUBLANES, n + nf), F32)],
        compiler_params=pltpu.CompilerParams(
            dimension_semantics=("arbitrary",), vmem_limit_bytes=VMEM_LIMIT_BYTES),
        name="modulation",
    )(c_lanes, w, b.reshape(1, n), wf, bf.reshape(1, nf))


def _mod_spec(which):
    return pl.BlockSpec((1, 1, D_MODEL), lambda b, t: (b, 0, which))


def _in_proj_kernel(x_ref, sc_ref, sh_ref, g_ref, w_ref, wa2_ref, ba2_ref,
                    u_ref, q_ref, k_ref, v_ref, gate_ref, la_ref):
    x = x_ref[0]
    ms = jnp.mean(x * x, axis=-1, keepdims=True)
    h = x * lax.rsqrt(ms + EPS) * g_ref[...]
    h = h * (1.0 + sc_ref[0]) + sh_ref[0]
    hb = h.astype(BF16)

    proj = _dot(hb, w_ref[...])

    def piece(name):
        lo, hi = IN_COLS[name]
        return proj[:, lo:hi]

    u_ref[0] = piece("cv") * _sigmoid(piece("cg"))
    q_ref[0] = piece("q") * (HEAD_K ** -0.5)
    k_ref[0] = piece("k")
    v_ref[0] = piece("v")
    og = piece("og")
    gate_ref[0] = og * _sigmoid(og)

    a_low = piece("a")
    z = _dot(a_low.astype(BF16), wa2_ref[...]) + ba2_ref[...]
    log_sig = jnp.minimum(z, 0.0) - jnp.log(1.0 + jnp.exp(-jnp.abs(z)))
    la_ref[0] = log_sig * (1.0 / GATE_TAU)


def _in_proj(x, mod, g, w, wa2, ba2):
    b_sz, t_len, d = x.shape
    tm = TOKENS_IN
    assert t_len % tm == 0

    def tok(width):
        return pl.BlockSpec((1, tm, width), lambda b, t: (b, t, 0))

    def const(arr):
        return pl.BlockSpec(arr.shape, lambda b, t: (0,) * arr.ndim)

    def out(width):
        return jax.ShapeDtypeStruct((b_sz, t_len, width), F32)

    return pl.pallas_call(
        _in_proj_kernel,
        grid=(b_sz, t_len // tm),
        in_specs=[tok(d), _mod_spec(MOD_SC1), _mod_spec(MOD_SH1), const(g), const(w),
                  const(wa2), const(ba2)],
        out_specs=[tok(D_CONV), tok(D_GLA_K), tok(D_GLA_K), tok(D_GLA_V), tok(D_GLA_V),
                   tok(D_GLA_K)],
        out_shape=[out(D_CONV), out(D_GLA_K), out(D_GLA_K), out(D_GLA_V), out(D_GLA_V),
                   out(D_GLA_K)],
        compiler_params=pltpu.CompilerParams(
            dimension_semantics=("arbitrary", "arbitrary"), vmem_limit_bytes=VMEM_LIMIT_BYTES),
        name="in_proj",
    )(x, mod, mod, g, w, wa2, ba2)


def _mixer_kernel(u_ref, q_ref, k_ref, la_ref, v_ref, gate_ref, cw_ref, cb_ref,
                  lng_ref, lnb_ref, gng_ref, y_ref, win_ref, pre_ref, st_ref, cum_ref):
    tb = u_ref.shape[1]
    lc = GLA_CHUNK

    @pl.when(pl.program_id(1) == 0)
    def _():
        win_ref[...] = jnp.zeros_like(win_ref)
        st_ref[...] = jnp.zeros_like(st_ref)

    for c in range(D_CONV // LANES):
        u_lanes = u_ref[0, :, c * LANES:(c + 1) * LANES]
        for phase in range(SUBLANES):
            win_ref[phase, c, CONV_HALO - phase:CONV_HALO - phase + tb, :] = u_lanes
    first_tap = CONV_HALO - (CONV_KERNEL - 1)

    for c in range(D_CONV // LANES):
        lanes = slice(c * LANES, (c + 1) * LANES)
        taps = []
        for phase in range(SUBLANES):
            for base in range(0, CONV_HALO + SUBLANES, SUBLANES):
                tap = base + phase - first_tap
                if 0 <= tap < CONV_KERNEL:
                    taps.append((cw_ref[tap, :, lanes], phase, base))
        bias = cb_ref[:, lanes]

        def conv_tile(i, carry, c=c, lanes=lanes, taps=taps, bias=bias):
            r0 = pl.multiple_of(i * CONV_ROWS, CONV_ROWS)
            parts = [jnp.broadcast_to(bias, (CONV_ROWS, LANES))] + [None] * (CONV_PARTIALS - 1)
            for n, (w8, phase, base) in enumerate(taps):
                w_rows = jnp.concatenate([w8] * (CONV_ROWS // SUBLANES), axis=0)
                term = w_rows * win_ref[phase, c, pl.ds(r0 + base, CONV_ROWS), :]
                p = n % CONV_PARTIALS
                parts[p] = term if parts[p] is None else parts[p] + term
            pre_ref[pl.ds(r0, CONV_ROWS), lanes] = functools.reduce(lambda a, b: a + b, parts)
            return carry

        lax.fori_loop(0, tb // CONV_ROWS, conv_tile, 0)

    for r0 in range(0, tb, NORM_ROWS):
        acc = pre_ref[r0:r0 + NORM_ROWS, :]
        mu = jnp.mean(acc, axis=-1, keepdims=True)
        cen = acc - mu
        var = jnp.mean(cen * cen, axis=-1, keepdims=True)
        yn = cen * lax.rsqrt(var + EPS) * lng_ref[...] + lnb_ref[...]
        y_ref[0, r0:r0 + NORM_ROWS, 0:D_CONV] = (yn * _sigmoid(yn)).astype(y_ref.dtype)
    win_ref[:, :, 0:CONV_HALO, :] = win_ref[:, :, tb:tb + CONV_HALO, :]

    n_chunks = tb // lc
    row = lax.broadcasted_iota(jnp.int32, (lc, lc), 0)
    col = lax.broadcasted_iota(jnp.int32, (lc, lc), 1)
    tril = (col <= row).astype(BF16)
    klane = lax.broadcasted_iota(jnp.int32, (1, D_GLA_K), 1) // HEAD_K
    srow = lax.broadcasted_iota(jnp.int32, (GLA_HEADS * lc, lc), 0)
    scol = lax.broadcasted_iota(jnp.int32, (GLA_HEADS * lc, lc), 1)
    causal = scol <= (srow % lc)
    st_row = lax.broadcasted_iota(jnp.int32, (D_GLA_V, D_GLA_K), 0) // HEAD_V
    st_col = lax.broadcasted_iota(jnp.int32, (D_GLA_V, D_GLA_K), 1) // HEAD_K
    st_mask = (st_row == st_col).astype(F32)
    rows1 = lax.broadcasted_iota(jnp.int32, (lc, 1), 0)

    splits = [_split3(la_ref[0, ci * lc:(ci + 1) * lc, :]) for ci in range(n_chunks)]
    cums = [_dot(tril, hi) + _dot(tril, mid) + _dot(tril, lo) for hi, mid, lo in splits]
    decay = None
    for ci, cum in enumerate(cums):
        cum_ref[ci * lc:(ci + 1) * lc, :] = cum
        total = -cum[lc - 1:lc, :]
        decay = total if decay is None else jnp.maximum(decay, total)
    max_decay = jnp.max(decay)

    def intra_factorised(ci, q, k, cum):
        ref = cum[lc // 2 - 1:lc // 2, :]
        qt = (q * jnp.exp(cum - ref)).astype(BF16)
        kt = (k * jnp.exp(ref - cum)).astype(BF16)
        qbd = jnp.concatenate(
            [jnp.where(klane == h, qt, jnp.zeros_like(qt)) for h in range(GLA_HEADS)], axis=0)
        s = _dot_nt(qbd, kt)
        s = jnp.where(causal, s, 0.0).astype(BF16)
        heads = []
        for h in range(GLA_HEADS):
            vh = v_ref[0, ci * lc:(ci + 1) * lc, h * HEAD_V:(h + 1) * HEAD_V].astype(BF16)
            heads.append(_dot(s[h * lc:(h + 1) * lc, :], vh))
        return jnp.concatenate(heads, axis=-1)

    def intra_exact(ci, q, k, cum):
        e_row = lax.broadcasted_iota(jnp.int32, (D_GLA_K, D_GLA_V), 0) // HEAD_K
        e_col = lax.broadcasted_iota(jnp.int32, (D_GLA_K, D_GLA_V), 1) // HEAD_V
        expand = (e_row == e_col).astype(BF16)

        def key_row(j, o):
            kj = k_ref[0, pl.ds(ci * lc + j, 1), :]
            cj = cum_ref[pl.ds(ci * lc + j, 1), :]
            vj = v_ref[0, pl.ds(ci * lc + j, 1), :]
            p = q * kj * jnp.exp(jnp.minimum(cum - cj, 0.0))
            p = jnp.where(rows1 >= j, p, 0.0).astype(BF16)
            return o + _dot(p, expand) * vj

        return lax.fori_loop(0, lc, key_row, jnp.zeros((lc, D_GLA_V), F32))

    def gla_tile(intra):
        for ci in range(n_chunks):
            rows = slice(ci * lc, (ci + 1) * lc)
            q = q_ref[0, rows, :]
            k = k_ref[0, rows, :]
            cum = cum_ref[rows, :]
            cum_last = cum[lc - 1:lc, :]
            o_intra = intra(ci, q, k, cum)

            st = st_ref[...]
            o_inter = _dot_nt((q * jnp.exp(cum)).astype(BF16), st.astype(BF16))
            kh = (k * jnp.exp(cum_last - cum)).astype(BF16)
            vb = v_ref[0, rows, :].astype(BF16)
            st_ref[...] = st * jnp.exp(cum_last) + _dot_tn(vb, kh) * st_mask

            o = o_inter + o_intra
            normed = []
            for h in range(GLA_HEADS):
                oh = o[:, h * HEAD_V:(h + 1) * HEAD_V]
                ms = jnp.mean(oh * oh, axis=-1, keepdims=True)
                normed.append(oh * lax.rsqrt(ms + EPS))
            on = jnp.concatenate(normed, axis=-1) * gng_ref[...]
            y_gla = on * gate_ref[0, rows, :]
            y_ref[0, rows, D_CONV:D_CONV + D_GLA_V] = y_gla.astype(y_ref.dtype)

    @pl.when(max_decay <= GLA_MAX_FACTORISED_DECAY)
    def _():
        gla_tile(intra_factorised)

    @pl.when(max_decay > GLA_MAX_FACTORISED_DECAY)
    def _():
        gla_tile(intra_exact)


def _mixer(u, q, k, la, v, gate, cw, cb, lng, lnb, gng):
    b_sz, t_len, _ = u.shape
    tb = TOKENS_MIX
    assert t_len % tb == 0 and tb % GLA_CHUNK == 0 and tb % CONV_ROWS == 0 and tb % NORM_ROWS == 0

    def tok(width):
        return pl.BlockSpec((1, tb, width), lambda b, t: (b, t, 0))

    def const(arr):
        return pl.BlockSpec(arr.shape, lambda b, t: (0,) * arr.ndim)

    return pl.pallas_call(
        _mixer_kernel,
        grid=(b_sz, t_len // tb),
        in_specs=[tok(D_CONV), tok(D_GLA_K), tok(D_GLA_K), tok(D_GLA_K), tok(D_GLA_V),
                  tok(D_GLA_V), const(cw), const(cb), const(lng), const(lnb), const(gng)],
        out_specs=tok(D_CONV + D_GLA_V),
        out_shape=jax.ShapeDtypeStruct((b_sz, t_len, D_CONV + D_GLA_V), BF16),
        scratch_shapes=[
            pltpu.VMEM((SUBLANES, D_CONV // LANES, CONV_HALO + tb, LANES), F32),
            pltpu.VMEM((tb, D_CONV), F32),
            pltpu.VMEM((D_GLA_V, D_GLA_K), F32),
            pltpu.VMEM((tb, D_GLA_K), F32),
        ],
        compiler_params=pltpu.CompilerParams(
            dimension_semantics=("arbitrary", "arbitrary"), vmem_limit_bytes=VMEM_LIMIT_BYTES),
        name="mixer",
    )(u, q, k, la, v, gate, cw, cb, lng, lnb, gng)


def _out_ffn_kernel(x_ref, y_ref, gt1_ref, sc2_ref, sh2_ref, gt2_ref, scf_ref, shf_ref,
                    gffn_ref, gfin_ref, wout_ref, wgu_ref, wo_ref, o_ref):
    tm = x_ref.shape[1]
    groups = [slice(r, r + FFN_ROWS) for r in range(0, tm, FFN_ROWS)]

    def out_proj(rows):
        return x_ref[0, rows, :] + gt1_ref[0] * _dot(y_ref[0, rows, :], wout_ref[...])

    def ffn_input(x1):
        ms = jnp.mean(x1 * x1, axis=-1, keepdims=True)
        h = x1 * lax.rsqrt(ms + EPS) * gffn_ref[...]
        return (h * (1.0 + sc2_ref[0]) + sh2_ref[0]).astype(BF16)

    def ffn(x1, hb):
        acc = jnp.zeros_like(x1)
        for lo, hi in FFN_SLABS:
            gate = _dot(hb, wgu_ref[:, lo:hi])
            up = _dot(hb, wgu_ref[:, D_FF + lo:D_FF + hi])
            act = (gate * _sigmoid(gate) * up).astype(BF16)
            acc = acc + _dot(act, wo_ref[lo:hi, :])
        return x1 + gt2_ref[0] * acc

    def final_norm(rows, x2):
        ms = jnp.mean(x2 * x2, axis=-1, keepdims=True)
        xn = x2 * lax.rsqrt(ms + EPS) * gfin_ref[...]
        o_ref[0, rows, :] = xn * (1.0 + scf_ref[0]) + shf_ref[0]

    x1s = [out_proj(rows) for rows in groups]
    pending = None
    for rows, x1 in zip(groups, x1s):
        hb = ffn_input(x1)
        if pending is not None:
            final_norm(*pending)
        pending = (rows, ffn(x1, hb))
    final_norm(*pending)


def _out_ffn(x, y, mod, gffn, gfin, wout, wgu, wo):
    b_sz, t_len, d = x.shape
    tm = TOKENS_FFN
    assert t_len % tm == 0 and tm % FFN_ROWS == 0

    def tok(width):
        return pl.BlockSpec((1, tm, width), lambda b, t: (b, t, 0))

    def const(arr):
        return pl.BlockSpec(arr.shape, lambda b, t: (0,) * arr.ndim,
                            pipeline_mode=pl.Buffered(1))

    mods = (MOD_GT1, MOD_SC2, MOD_SH2, MOD_GT2, MOD_SCF, MOD_SHF)
    return pl.pallas_call(
        _out_ffn_kernel,
        grid=(b_sz, t_len // tm),
        in_specs=[tok(d), tok(d)] + [_mod_spec(m) for m in mods]
                 + [const(gffn), const(gfin), const(wout), const(wgu), const(wo)],
        out_specs=tok(d),
        out_shape=jax.ShapeDtypeStruct((b_sz, t_len, d), F32),
        compiler_params=pltpu.CompilerParams(
            dimension_semantics=("arbitrary", "arbitrary"), vmem_limit_bytes=VMEM_LIMIT_BYTES),
        name="out_ffn",
    )(x, y, *([mod] * len(mods)), gffn, gfin, wout, wgu, wo)


def kernel(x, c, w_ada, b_ada, g_mix, w_in, conv_w, conv_b, ln_g, ln_b, w_a2, b_a2,
           gla_norm_g, w_out, g_ffn, w_ffn_in, w_ffn_out, w_ada_final, b_ada_final, g_final):
    b_sz, t_len, d = x.shape
    assert w_ada.shape[0] == 1, "single-layer block"
    assert d == D_MODEL

    c_lanes = jnp.broadcast_to(c[:, :, None], (b_sz, d, LANES))
    mod = _modulation(c_lanes, w_ada[0], b_ada[0], w_ada_final, b_ada_final)
    mod = mod.reshape(b_sz, 1, 8 * d)

    w = jnp.pad(w_in[0], ((0, 0), (0, LANES - GATE_RANK))).astype(BF16)
    wa2 = jnp.pad(w_a2[0], ((0, LANES - GATE_RANK), (0, 0))).astype(BF16)

    u, q, k, v, gate, la = _in_proj(
        x, mod, g_mix[0].reshape(1, d), w, wa2, b_a2[0].reshape(1, D_GLA_K))

    cw = jnp.broadcast_to(conv_w[0][:, None, :], (CONV_KERNEL, SUBLANES, D_CONV))
    y = _mixer(u, q, k, la, v, gate, cw, conv_b[0].reshape(1, D_CONV),
               ln_g[0].reshape(1, D_CONV), ln_b[0].reshape(1, D_CONV),
               jnp.tile(gla_norm_g[0], GLA_HEADS).reshape(1, D_GLA_V))

    return _out_ffn(
        x, y, mod, g_ffn[0].reshape(1, d), g_final.reshape(1, d),
        w_out[0].astype(BF16), w_ffn_in[0].astype(BF16), w_ffn_out[0].astype(BF16))
```

```python
import functools

import jax
import jax.numpy as jnp
from jax import lax
from jax.experimental import pallas as pl
from jax.experimental.pallas import tpu as pltpu

F32 = jnp.float32
BF16 = jnp.bfloat16

D_MODEL = 1024
D_CONV = 512
CONV_KERNEL = 31
D_GLA_V = 512
D_GLA_K = 256
GLA_HEADS = 4
HEAD_K = D_GLA_K // GLA_HEADS
HEAD_V = D_GLA_V // GLA_HEADS
GATE_RANK = 16
GATE_TAU = 16.0
D_FF = 2816
EPS = 1e-6

LANES = 128
SUBLANES = 8
MXU_DIM = 256
VMEM_LIMIT_BYTES = 56 * 1024 * 1024
MOD_ROWS = 128
TOKENS_IN = 1024
TOKENS_MIX = 512
TOKENS_FFN = 512
STAGE_ROWS_WIDE = 128
STAGE_ROWS = 512
CONV_HALO = 32
CONV_ROWS = 64
CONV_PARTIALS = 2
NORM_ROWS = 128
GLA_CHUNK = 128
GLA_MAX_FACTORISED_DECAY = 40.0
FFN_ROWS = 256
FFN_SLABS = ((0, 6 * MXU_DIM), (6 * MXU_DIM, D_FF))

IN_COLS = {}
_col = 0
for _name, _width in (("cv", D_CONV), ("cg", D_CONV), ("q", D_GLA_K), ("k", D_GLA_K),
                      ("v", D_GLA_V), ("og", D_GLA_V), ("a", LANES)):
    IN_COLS[_name] = (_col, _col + _width)
    _col += _width

MOD_SH1, MOD_SC1, MOD_GT1, MOD_SH2, MOD_SC2, MOD_GT2, MOD_SHF, MOD_SCF = range(8)


def _sigmoid(x):
    return 1.0 / (1.0 + jnp.exp(-x))


def _dot(a, b):
    return jnp.dot(a, b, preferred_element_type=F32)


def _dot_nt(a, b):
    return lax.dot_general(a, b, (((1,), (1,)), ((), ())), preferred_element_type=F32)


def _dot_tn(a, b):
    return lax.dot_general(a, b, (((0,), (0,)), ((), ())), preferred_element_type=F32)


def _split3(a):
    hi = a.astype(BF16)
    r1 = a - hi.astype(F32)
    mid = r1.astype(BF16)
    lo = (r1 - mid.astype(F32)).astype(BF16)
    return hi, mid, lo


def _modulation_kernel(c_ref, w_ref, b_ref, wf_ref, bf_ref, o_ref, acc_ref):
    j = pl.program_id(0)
    n_batch, k_rows, _ = c_ref.shape
    n = w_ref.shape[1]

    @pl.when(j == 0)
    def _():
        acc_ref[...] = jnp.zeros_like(acc_ref)

    c_act = []
    for b in range(n_batch):
        cb = c_ref[b]
        c_act.append(cb * _sigmoid(cb))
    for ref, offset in ((w_ref, 0), (wf_ref, n)):
        for t in range(ref.shape[1] // LANES):
            w = ref[:, t * LANES:(t + 1) * LANES]
            cols = slice(offset + t * LANES, offset + (t + 1) * LANES)
            for b in range(n_batch):
                prod = (w * c_act[b]).reshape(k_rows // SUBLANES, SUBLANES, LANES)
                acc_ref[b, :, cols] += jnp.sum(prod, axis=0)

    @pl.when(j == pl.num_programs(0) - 1)
    def _():
        bias = jnp.concatenate([b_ref[...], bf_ref[...]], axis=1)
        for b in range(n_batch):
            o_ref[b:b + 1, :] = jnp.sum(acc_ref[b], axis=0, keepdims=True) + bias


def _modulation(c_lanes, w, b, wf, bf):
    n_batch, d, _ = c_lanes.shape
    n, nf = w.shape[1], wf.shape[1]
    assert d % MOD_ROWS == 0 and n % LANES == 0 and nf % LANES == 0
    return pl.pallas_call(
        _modulation_kernel,
        grid=(d // MOD_ROWS,),
        in_specs=[
            pl.BlockSpec((n_batch, MOD_ROWS, LANES), lambda j: (0, j, 0)),
            pl.BlockSpec((MOD_ROWS, n), lambda j: (j, 0)),
            pl.BlockSpec((1, n), lambda j: (0, 0)),
            pl.BlockSpec((MOD_ROWS, nf), lambda j: (j, 0)),
            pl.BlockSpec((1, nf), lambda j: (0, 0)),
        ],
        out_specs=pl.BlockSpec((n_batch, n + nf), lambda j: (0, 0)),
        out_shape=jax.ShapeDtypeStruct((n_batch, n + nf), F32),
        scratch_shapes=[pltpu.VMEM((n_batch, SUBLANES, n + nf), F32)],
        compiler_params=pltpu.CompilerParams(
            dimension_semantics=("arbitrary",), vmem_limit_bytes=VMEM_LIMIT_BYTES),
        name="modulation",
    )(c_lanes, w, b.reshape(1, n), wf, bf.reshape(1, nf))


def _mod_spec(which):
    return pl.BlockSpec((1, 1, D_MODEL), lambda b, t: (b, 0, which))


def _in_proj_kernel(x_ref, sc_ref, sh_ref, g_ref, w_ref, wa2_ref, ba2_ref,
                    u_ref, q_ref, k_ref, v_ref, gate_ref, la_ref):
    x = x_ref[0]
    ms = jnp.mean(x * x, axis=-1, keepdims=True)
    h = x * lax.rsqrt(ms + EPS) * g_ref[...]
    h = h * (1.0 + sc_ref[0]) + sh_ref[0]
    hb = h.astype(BF16)

    proj = _dot(hb, w_ref[...])

    def piece(name):
        lo, hi = IN_COLS[name]
        return proj[:, lo:hi]

    u_ref[0] = piece("cv") * _sigmoid(piece("cg"))
    q_ref[0] = piece("q") * (HEAD_K ** -0.5)
    k_ref[0] = piece("k")
    v_ref[0] = piece("v")
    og = piece("og")
    gate_ref[0] = og * _sigmoid(og)

    a_low = piece("a")
    z = _dot(a_low.astype(BF16), wa2_ref[...]) + ba2_ref[...]
    log_sig = jnp.minimum(z, 0.0) - jnp.log(1.0 + jnp.exp(-jnp.abs(z)))
    la_ref[0] = log_sig * (1.0 / GATE_TAU)


def _in_proj(x, mod, g, w, wa2, ba2):
    b_sz, t_len, d = x.shape
    tm = TOKENS_IN
    assert t_len % tm == 0

    def tok(width):
        return pl.BlockSpec((1, tm, width), lambda b, t: (b, t, 0))

    def const(arr):
        return pl.BlockSpec(arr.shape, lambda b, t: (0,) * arr.ndim)

    def out(width):
        return jax.ShapeDtypeStruct((b_sz, t_len, width), F32)

    return pl.pallas_call(
        _in_proj_kernel,
        grid=(b_sz, t_len // tm),
        in_specs=[tok(d), _mod_spec(MOD_SC1), _mod_spec(MOD_SH1), const(g), const(w),
                  const(wa2), const(ba2)],
        out_specs=[tok(D_CONV), tok(D_GLA_K), tok(D_GLA_K), tok(D_GLA_V), tok(D_GLA_V),
                   tok(D_GLA_K)],
        out_shape=[out(D_CONV), out(D_GLA_K), out(D_GLA_K), out(D_GLA_V), out(D_GLA_V),
                   out(D_GLA_K)],
        compiler_params=pltpu.CompilerParams(
            dimension_semantics=("arbitrary", "arbitrary"), vmem_limit_bytes=VMEM_LIMIT_BYTES),
        name="in_proj",
    )(x, mod, mod, g, w, wa2, ba2)


def _mixer_kernel(u_ref, q_ref, k_ref, la_ref, v_ref, gate_ref, cw_ref, cb_ref,
                  lng_ref, lnb_ref, gng_ref, y_ref, win_ref, pre_ref, st_ref, cum_ref):
    tb = u_ref.shape[1]
    lc = GLA_CHUNK

    @pl.when(pl.program_id(1) == 0)
    def _():
        win_ref[...] = jnp.zeros_like(win_ref)
        st_ref[...] = jnp.zeros_like(st_ref)

    for c in range(D_CONV // LANES):
        u_lanes = u_ref[0, :, c * LANES:(c + 1) * LANES]
        for phase in range(SUBLANES):
            win_ref[phase, c, CONV_HALO - phase:CONV_HALO - phase + tb, :] = u_lanes
    first_tap = CONV_HALO - (CONV_KERNEL - 1)

    for c in range(D_CONV // LANES):
        lanes = slice(c * LANES, (c + 1) * LANES)
        taps = []
        for phase in range(SUBLANES):
            for base in range(0, CONV_HALO + SUBLANES, SUBLANES):
                tap = base + phase - first_tap
                if 0 <= tap < CONV_KERNEL:
                    taps.append((cw_ref[tap, :, lanes], phase, base))
        bias = cb_ref[:, lanes]

        def conv_tile(i, carry, c=c, lanes=lanes, taps=taps, bias=bias):
            r0 = pl.multiple_of(i * CONV_ROWS, CONV_ROWS)
            parts = [jnp.broadcast_to(bias, (CONV_ROWS, LANES))] + [None] * (CONV_PARTIALS - 1)
            for n, (w8, phase, base) in enumerate(taps):
                w_rows = jnp.concatenate([w8] * (CONV_ROWS // SUBLANES), axis=0)
                term = w_rows * win_ref[phase, c, pl.ds(r0 + base, CONV_ROWS), :]
                p = n % CONV_PARTIALS
                parts[p] = term if parts[p] is None else parts[p] + term
            pre_ref[pl.ds(r0, CONV_ROWS), lanes] = functools.reduce(lambda a, b: a + b, parts)
            return carry

        lax.fori_loop(0, tb // CONV_ROWS, conv_tile, 0)

    for r0 in range(0, tb, NORM_ROWS):
        acc = pre_ref[r0:r0 + NORM_ROWS, :]
        mu = jnp.mean(acc, axis=-1, keepdims=True)
        cen = acc - mu
        var = jnp.mean(cen * cen, axis=-1, keepdims=True)
        yn = cen * lax.rsqrt(var + EPS) * lng_ref[...] + lnb_ref[...]
        y_ref[0, r0:r0 + NORM_ROWS, 0:D_CONV] = (yn * _sigmoid(yn)).astype(y_ref.dtype)
    win_ref[:, :, 0:CONV_HALO, :] = win_ref[:, :, tb:tb + CONV_HALO, :]

    n_chunks = tb // lc
    row = lax.broadcasted_iota(jnp.int32, (lc, lc), 0)
    col = lax.broadcasted_iota(jnp.int32, (lc, lc), 1)
    tril = (col <= row).astype(BF16)
    klane = lax.broadcasted_iota(jnp.int32, (1, D_GLA_K), 1) // HEAD_K
    srow = lax.broadcasted_iota(jnp.int32, (GLA_HEADS * lc, lc), 0)
    scol = lax.broadcasted_iota(jnp.int32, (GLA_HEADS * lc, lc), 1)
    causal = scol <= (srow % lc)
    st_row = lax.broadcasted_iota(jnp.int32, (D_GLA_V, D_GLA_K), 0) // HEAD_V
    st_col = lax.broadcasted_iota(jnp.int32, (D_GLA_V, D_GLA_K), 1) // HEAD_K
    st_mask = (st_row == st_col).astype(F32)
    rows1 = lax.broadcasted_iota(jnp.int32, (lc, 1), 0)

    splits = [_split3(la_ref[0, ci * lc:(ci + 1) * lc, :]) for ci in range(n_chunks)]
    cums = [_dot(tril, hi) + _dot(tril, mid) + _dot(tril, lo) for hi, mid, lo in splits]
    decay = None
    for ci, cum in enumerate(cums):
        cum_ref[ci * lc:(ci + 1) * lc, :] = cum
        total = -cum[lc - 1:lc, :]
        decay = total if decay is None else jnp.maximum(decay, total)
    max_decay = jnp.max(decay)

    def intra_factorised(ci, q, k, cum):
        ref = cum[lc // 2 - 1:lc // 2, :]
        qt = (q * jnp.exp(cum - ref)).astype(BF16)
        kt = (k * jnp.exp(ref - cum)).astype(BF16)
        qbd = jnp.concatenate(
            [jnp.where(klane == h, qt, jnp.zeros_like(qt)) for h in range(GLA_HEADS)], axis=0)
        s = _dot_nt(qbd, kt)
        s = jnp.where(causal, s, 0.0).astype(BF16)
        heads = []
        for h in range(GLA_HEADS):
            vh = v_ref[0, ci * lc:(ci + 1) * lc, h * HEAD_V:(h + 1) * HEAD_V].astype(BF16)
            heads.append(_dot(s[h * lc:(h + 1) * lc, :], vh))
        return jnp.concatenate(heads, axis=-1)

    def intra_exact(ci, q, k, cum):
        e_row = lax.broadcasted_iota(jnp.int32, (D_GLA_K, D_GLA_V), 0) // HEAD_K
        e_col = lax.broadcasted_iota(jnp.int32, (D_GLA_K, D_GLA_V), 1) // HEAD_V
        expand = (e_row == e_col).astype(BF16)

        def key_row(j, o):
            kj = k_ref[0, pl.ds(ci * lc + j, 1), :]
            cj = cum_ref[pl.ds(ci * lc + j, 1), :]
            vj = v_ref[0, pl.ds(ci * lc + j, 1), :]
            p = q * kj * jnp.exp(jnp.minimum(cum - cj, 0.0))
            p = jnp.where(rows1 >= j, p, 0.0).astype(BF16)
            return o + _dot(p, expand) * vj

        return lax.fori_loop(0, lc, key_row, jnp.zeros((lc, D_GLA_V), F32))

    def gla_tile(intra):
        for ci in range(n_chunks):
            rows = slice(ci * lc, (ci + 1) * lc)
            q = q_ref[0, rows, :]
            k = k_ref[0, rows, :]
            cum = cum_ref[rows, :]
            cum_last = cum[lc - 1:lc, :]
            o_intra = intra(ci, q, k, cum)

            st = st_ref[...]
            o_inter = _dot_nt((q * jnp.exp(cum)).astype(BF16), st.astype(BF16))
            kh = (k * jnp.exp(cum_last - cum)).astype(BF16)
            vb = v_ref[0, rows, :].astype(BF16)
            st_ref[...] = st * jnp.exp(cum_last) + _dot_tn(vb, kh) * st_mask

            o = o_inter + o_intra
            normed = []
            for h in range(GLA_HEADS):
                oh = o[:, h * HEAD_V:(h + 1) * HEAD_V]
                ms = jnp.mean(oh * oh, axis=-1, keepdims=True)
                normed.append(oh * lax.rsqrt(ms + EPS))
            on = jnp.concatenate(normed, axis=-1) * gng_ref[...]
            y_gla = on * gate_ref[0, rows, :]
            y_ref[0, rows, D_CONV:D_CONV + D_GLA_V] = y_gla.astype(y_ref.dtype)

    @pl.when(max_decay <= GLA_MAX_FACTORISED_DECAY)
    def _():
        gla_tile(intra_factorised)

    @pl.when(max_decay > GLA_MAX_FACTORISED_DECAY)
    def _():
        gla_tile(intra_exact)


def _mixer(u, q, k, la, v, gate, cw, cb, lng, lnb, gng):
    b_sz, t_len, _ = u.shape
    tb = TOKENS_MIX
    assert t_len % tb == 0 and tb % GLA_CHUNK == 0 and tb % CONV_ROWS == 0 and tb % NORM_ROWS == 0

    def tok(width):
        return pl.BlockSpec((1, tb, width), lambda b, t: (b, t, 0))

    def const(arr):
        return pl.BlockSpec(arr.shape, lambda b, t: (0,) * arr.ndim)

    return pl.pallas_call(
        _mixer_kernel,
        grid=(b_sz, t_len // tb),
        in_specs=[tok(D_CONV), tok(D_GLA_K), tok(D_GLA_K), tok(D_GLA_K), tok(D_GLA_V),
                  tok(D_GLA_V), const(cw), const(cb), const(lng), const(lnb), const(gng)],
        out_specs=tok(D_CONV + D_GLA_V),
        out_shape=jax.ShapeDtypeStruct((b_sz, t_len, D_CONV + D_GLA_V), BF16),
        scratch_shapes=[
            pltpu.VMEM((SUBLANES, D_CONV // LANES, CONV_HALO + tb, LANES), F32),
            pltpu.VMEM((tb, D_CONV), F32),
            pltpu.VMEM((D_GLA_V, D_GLA_K), F32),
            pltpu.VMEM((tb, D_GLA_K), F32),
        ],
        compiler_params=pltpu.CompilerParams(
            dimension_semantics=("arbitrary", "arbitrary"), vmem_limit_bytes=VMEM_LIMIT_BYTES),
        name="mixer",
    )(u, q, k, la, v, gate, cw, cb, lng, lnb, gng)


def _stage_bf16(src_hbm, dst_ref, stage_ref, sem):
    n_rows = src_hbm.shape[0]
    chunk = stage_ref.shape[1]
    starts = list(range(0, n_rows, chunk))

    def copy(i):
        r, n = starts[i], min(chunk, n_rows - starts[i])
        return pltpu.make_async_copy(
            src_hbm.at[pl.ds(r, n), :], stage_ref.at[i % 2, pl.ds(0, n), :], sem.at[i % 2])

    copy(0).start()
    for i, r in enumerate(starts):
        if i + 1 < len(starts):
            copy(i + 1).start()
        copy(i).wait()
        n = min(chunk, n_rows - r)
        dst_ref[r:r + n, :] = stage_ref[i % 2, 0:n, :].astype(dst_ref.dtype)


def _out_ffn_kernel(x_ref, y_ref, gt1_ref, sc2_ref, sh2_ref, gt2_ref, scf_ref, shf_ref,
                    gffn_ref, gfin_ref, wout_hbm, wgu_hbm, wo_hbm, o_ref,
                    wout_ref, wgu_ref, wo_ref, stage_wide_ref, stage_ref, sem_wide, sem):
    @pl.when((pl.program_id(0) == 0) & (pl.program_id(1) == 0))
    def _():
        _stage_bf16(wout_hbm, wout_ref, stage_ref, sem)
        _stage_bf16(wgu_hbm, wgu_ref, stage_wide_ref, sem_wide)
        _stage_bf16(wo_hbm, wo_ref, stage_ref, sem)

    tm = x_ref.shape[1]
    groups = [slice(r, r + FFN_ROWS) for r in range(0, tm, FFN_ROWS)]

    def out_proj(rows):
        return x_ref[0, rows, :] + gt1_ref[0] * _dot(y_ref[0, rows, :], wout_ref[...])

    def ffn_input(x1):
        ms = jnp.mean(x1 * x1, axis=-1, keepdims=True)
        h = x1 * lax.rsqrt(ms + EPS) * gffn_ref[...]
        return (h * (1.0 + sc2_ref[0]) + sh2_ref[0]).astype(BF16)

    def ffn(x1, hb):
        acc = jnp.zeros_like(x1)
        for lo, hi in FFN_SLABS:
            gate = _dot(hb, wgu_ref[:, lo:hi])
            up = _dot(hb, wgu_ref[:, D_FF + lo:D_FF + hi])
            act = (gate * _sigmoid(gate) * up).astype(BF16)
            acc = acc + _dot(act, wo_ref[lo:hi, :])
        return x1 + gt2_ref[0] * acc

    def final_norm(rows, x2):
        ms = jnp.mean(x2 * x2, axis=-1, keepdims=True)
        xn = x2 * lax.rsqrt(ms + EPS) * gfin_ref[...]
        o_ref[0, rows, :] = xn * (1.0 + scf_ref[0]) + shf_ref[0]

    x1s = [out_proj(rows) for rows in groups]
    pending = None
    for rows, x1 in zip(groups, x1s):
        hb = ffn_input(x1)
        if pending is not None:
            final_norm(*pending)
        pending = (rows, ffn(x1, hb))
    final_norm(*pending)


def _out_ffn(x, y, mod, gffn, gfin, wout, wgu, wo):
    b_sz, t_len, d = x.shape
    tm = TOKENS_FFN
    assert t_len % tm == 0 and tm % FFN_ROWS == 0

    def tok(width):
        return pl.BlockSpec((1, tm, width), lambda b, t: (b, t, 0))

    def const(arr):
        return pl.BlockSpec(arr.shape, lambda b, t: (0,) * arr.ndim,
                            pipeline_mode=pl.Buffered(1))

    mods = (MOD_GT1, MOD_SC2, MOD_SH2, MOD_GT2, MOD_SCF, MOD_SHF)
    in_hbm = pl.BlockSpec(memory_space=pl.ANY)
    return pl.pallas_call(
        _out_ffn_kernel,
        grid=(b_sz, t_len // tm),
        in_specs=[tok(d), tok(d)] + [_mod_spec(m) for m in mods]
                 + [const(gffn), const(gfin), in_hbm, in_hbm, in_hbm],
        out_specs=tok(d),
        out_shape=jax.ShapeDtypeStruct((b_sz, t_len, d), F32),
        scratch_shapes=[
            pltpu.VMEM(wout.shape, BF16),
            pltpu.VMEM(wgu.shape, BF16),
            pltpu.VMEM(wo.shape, BF16),
            pltpu.VMEM((2, STAGE_ROWS_WIDE, wgu.shape[1]), F32),
            pltpu.VMEM((2, STAGE_ROWS, d), F32),
            pltpu.SemaphoreType.DMA((2,)),
            pltpu.SemaphoreType.DMA((2,)),
        ],
        compiler_params=pltpu.CompilerParams(
            dimension_semantics=("arbitrary", "arbitrary"), vmem_limit_bytes=VMEM_LIMIT_BYTES),
        name="out_ffn",
    )(x, y, *([mod] * len(mods)), gffn, gfin, wout, wgu, wo)


def kernel(x, c, w_ada, b_ada, g_mix, w_in, conv_w, conv_b, ln_g, ln_b, w_a2, b_a2,
           gla_norm_g, w_out, g_ffn, w_ffn_in, w_ffn_out, w_ada_final, b_ada_final, g_final):
    b_sz, t_len, d = x.shape
    assert w_ada.shape[0] == 1, "single-layer block"
    assert d == D_MODEL

    c_lanes = jnp.broadcast_to(c[:, :, None], (b_sz, d, LANES))
    mod = _modulation(c_lanes, w_ada[0], b_ada[0], w_ada_final, b_ada_final)
    mod = mod.reshape(b_sz, 1, 8 * d)

    w = jnp.pad(w_in[0], ((0, 0), (0, LANES - GATE_RANK))).astype(BF16)
    wa2 = jnp.pad(w_a2[0], ((0, LANES - GATE_RANK), (0, 0))).astype(BF16)

    u, q, k, v, gate, la = _in_proj(
        x, mod, g_mix[0].reshape(1, d), w, wa2, b_a2[0].reshape(1, D_GLA_K))

    cw = jnp.broadcast_to(conv_w[0][:, None, :], (CONV_KERNEL, SUBLANES, D_CONV))
    y = _mixer(u, q, k, la, v, gate, cw, conv_b[0].reshape(1, D_CONV),
               ln_g[0].reshape(1, D_CONV), ln_b[0].reshape(1, D_CONV),
               jnp.tile(gla_norm_g[0], GLA_HEADS).reshape(1, D_GLA_V))

    return _out_ffn(
        x, y, mod, g_ffn[0].reshape(1, d), g_final.reshape(1, d),
        w_out[0], w_ffn_in[0], w_ffn_out[0])
```

```python
import functools

import jax
import jax.numpy as jnp
from jax import lax
from jax.experimental import pallas as pl
from jax.experimental.pallas import tpu as pltpu

F32 = jnp.float32
BF16 = jnp.bfloat16

D_MODEL = 1024
D_CONV = 512
CONV_KERNEL = 31
D_GLA_V = 512
D_GLA_K = 256
GLA_HEADS = 4
HEAD_K = D_GLA_K // GLA_HEADS
HEAD_V = D_GLA_V // GLA_HEADS
GATE_RANK = 16
GATE_TAU = 16.0
D_FF = 2816
EPS = 1e-6

LANES = 128
SUBLANES = 8
MXU_DIM = 256
VMEM_LIMIT_BYTES = 56 * 1024 * 1024
MOD_ROWS = 128
TOKENS_IN = 1024
TOKENS_MIX = 512
TOKENS_FFN = 512
STAGE_ROWS_IN = 256
STAGE_ROWS_WIDE = 128
STAGE_ROWS = 512
CONV_HALO = 32
CONV_ROWS = 64
CONV_PARTIALS = 2
NORM_ROWS = 128
GLA_CHUNK = 128
GLA_MAX_FACTORISED_DECAY = 40.0
FFN_ROWS = 256
FFN_SLABS = ((0, 6 * MXU_DIM), (6 * MXU_DIM, D_FF))

IN_COLS = {}
_col = 0
for _name, _width in (("cv", D_CONV), ("cg", D_CONV), ("q", D_GLA_K), ("k", D_GLA_K),
                      ("v", D_GLA_V), ("og", D_GLA_V), ("a", LANES)):
    IN_COLS[_name] = (_col, _col + _width)
    _col += _width

MOD_SH1, MOD_SC1, MOD_GT1, MOD_SH2, MOD_SC2, MOD_GT2, MOD_SHF, MOD_SCF = range(8)


def _sigmoid(x):
    return 1.0 / (1.0 + jnp.exp(-x))


def _dot(a, b):
    return jnp.dot(a, b, preferred_element_type=F32)


def _dot_nt(a, b):
    return lax.dot_general(a, b, (((1,), (1,)), ((), ())), preferred_element_type=F32)


def _dot_tn(a, b):
    return lax.dot_general(a, b, (((0,), (0,)), ((), ())), preferred_element_type=F32)


def _split3(a):
    hi = a.astype(BF16)
    r1 = a - hi.astype(F32)
    mid = r1.astype(BF16)
    lo = (r1 - mid.astype(F32)).astype(BF16)
    return hi, mid, lo


def _stage_bf16(src_hbm, dst_ref, stage_ref, sem):
    n_rows, n_cols = src_hbm.shape
    chunk = stage_ref.shape[1]
    starts = list(range(0, n_rows, chunk))

    def copy(i):
        r, n = starts[i], min(chunk, n_rows - starts[i])
        return pltpu.make_async_copy(
            src_hbm.at[pl.ds(r, n), :], stage_ref.at[i % 2, pl.ds(0, n), :], sem.at[i % 2])

    copy(0).start()
    for i, r in enumerate(starts):
        if i + 1 < len(starts):
            copy(i + 1).start()
        copy(i).wait()
        n = min(chunk, n_rows - r)
        dst_ref[r:r + n, 0:n_cols] = stage_ref[i % 2, 0:n, :].astype(dst_ref.dtype)


def _modulation_kernel(c_ref, w_ref, b_ref, wf_ref, bf_ref, o_ref, acc_ref):
    j = pl.program_id(0)
    n_batch, k_rows, _ = c_ref.shape
    n = w_ref.shape[1]

    @pl.when(j == 0)
    def _():
        acc_ref[...] = jnp.zeros_like(acc_ref)

    c_act = []
    for b in range(n_batch):
        cb = c_ref[b]
        c_act.append(cb * _sigmoid(cb))
    for ref, offset in ((w_ref, 0), (wf_ref, n)):
        for t in range(ref.shape[1] // LANES):
            w = ref[:, t * LANES:(t + 1) * LANES]
            cols = slice(offset + t * LANES, offset + (t + 1) * LANES)
            for b in range(n_batch):
                prod = (w * c_act[b]).reshape(k_rows // SUBLANES, SUBLANES, LANES)
                acc_ref[b, :, cols] += jnp.sum(prod, axis=0)

    @pl.when(j == pl.num_programs(0) - 1)
    def _():
        bias = jnp.concatenate([b_ref[...], bf_ref[...]], axis=1)
        for b in range(n_batch):
            o_ref[b:b + 1, :] = jnp.sum(acc_ref[b], axis=0, keepdims=True) + bias


def _modulation(c_lanes, w, b, wf, bf):
    n_batch, d, _ = c_lanes.shape
    n, nf = w.shape[1], wf.shape[1]
    assert d % MOD_ROWS == 0 and n % LANES == 0 and nf % LANES == 0
    return pl.pallas_call(
        _modulation_kernel,
        grid=(d // MOD_ROWS,),
        in_specs=[
            pl.BlockSpec((n_batch, MOD_ROWS, LANES), lambda j: (0, j, 0)),
            pl.BlockSpec((MOD_ROWS, n), lambda j: (j, 0)),
            pl.BlockSpec((1, n), lambda j: (0, 0)),
            pl.BlockSpec((MOD_ROWS, nf), lambda j: (j, 0)),
            pl.BlockSpec((1, nf), lambda j: (0, 0)),
        ],
        out_specs=pl.BlockSpec((n_batch, n + nf), lambda j: (0, 0)),
        out_shape=jax.ShapeDtypeStruct((n_batch, n + nf), F32),
        scratch_shapes=[pltpu.VMEM((n_batch, SUBLANES, n + nf), F32)],
        compiler_params=pltpu.CompilerParams(
            dimension_semantics=("arbitrary",), vmem_limit_bytes=VMEM_LIMIT_BYTES),
        name="modulation",
    )(c_lanes, w, b.reshape(1, n), wf, bf.reshape(1, nf))


def _mod_spec(which):
    return pl.BlockSpec((1, 1, D_MODEL), lambda b, t: (b, 0, which))


def _in_proj_kernel(x_ref, sc_ref, sh_ref, g_ref, w_hbm, wa2_ref, ba2_ref,
                    u_ref, q_ref, k_ref, v_ref, gate_ref, la_ref, w_ref, stage_ref, sem):
    @pl.when((pl.program_id(0) == 0) & (pl.program_id(1) == 0))
    def _():
        lo, hi = IN_COLS["a"]
        w_ref[:, lo:hi] = jnp.zeros((w_ref.shape[0], hi - lo), w_ref.dtype)
        _stage_bf16(w_hbm, w_ref, stage_ref, sem)

    x = x_ref[0]
    ms = jnp.mean(x * x, axis=-1, keepdims=True)
    h = x * lax.rsqrt(ms + EPS) * g_ref[...]
    h = h * (1.0 + sc_ref[0]) + sh_ref[0]
    hb = h.astype(BF16)

    proj = _dot(hb, w_ref[...])

    def piece(name):
        lo, hi = IN_COLS[name]
        return proj[:, lo:hi]

    u_ref[0] = piece("cv") * _sigmoid(piece("cg"))
    q_ref[0] = piece("q") * (HEAD_K ** -0.5)
    k_ref[0] = piece("k")
    v_ref[0] = piece("v")
    og = piece("og")
    gate_ref[0] = og * _sigmoid(og)

    a_low = piece("a")
    z = _dot(a_low.astype(BF16), wa2_ref[...]) + ba2_ref[...]
    log_sig = jnp.minimum(z, 0.0) - jnp.log(1.0 + jnp.exp(-jnp.abs(z)))
    la_ref[0] = log_sig * (1.0 / GATE_TAU)


def _in_proj(x, mod, g, w, wa2, ba2):
    b_sz, t_len, d = x.shape
    tm = TOKENS_IN
    assert t_len % tm == 0

    def tok(width):
        return pl.BlockSpec((1, tm, width), lambda b, t: (b, t, 0))

    def const(arr):
        return pl.BlockSpec(arr.shape, lambda b, t: (0,) * arr.ndim)

    def out(width):
        return jax.ShapeDtypeStruct((b_sz, t_len, width), F32)

    return pl.pallas_call(
        _in_proj_kernel,
        grid=(b_sz, t_len // tm),
        in_specs=[tok(d), _mod_spec(MOD_SC1), _mod_spec(MOD_SH1), const(g),
                  pl.BlockSpec(memory_space=pl.ANY), const(wa2), const(ba2)],
        out_specs=[tok(D_CONV), tok(D_GLA_K), tok(D_GLA_K), tok(D_GLA_V), tok(D_GLA_V),
                   tok(D_GLA_K)],
        out_shape=[out(D_CONV), out(D_GLA_K), out(D_GLA_K), out(D_GLA_V), out(D_GLA_V),
                   out(D_GLA_K)],
        scratch_shapes=[
            pltpu.VMEM((d, IN_COLS["a"][1]), BF16),
            pltpu.VMEM((2, STAGE_ROWS_IN, w.shape[1]), F32),
            pltpu.SemaphoreType.DMA((2,)),
        ],
        compiler_params=pltpu.CompilerParams(
            dimension_semantics=("arbitrary", "arbitrary"), vmem_limit_bytes=VMEM_LIMIT_BYTES),
        name="in_proj",
    )(x, mod, mod, g, w, wa2, ba2)


def _mixer_kernel(u_ref, q_ref, k_ref, la_ref, v_ref, gate_ref, cw_ref, cb_ref,
                  lng_ref, lnb_ref, gng_ref, y_ref, win_ref, pre_ref, st_ref, cum_ref):
    tb = u_ref.shape[1]
    lc = GLA_CHUNK

    @pl.when(pl.program_id(1) == 0)
    def _():
        win_ref[...] = jnp.zeros_like(win_ref)
        st_ref[...] = jnp.zeros_like(st_ref)

    for c in range(D_CONV // LANES):
        u_lanes = u_ref[0, :, c * LANES:(c + 1) * LANES]
        for phase in range(SUBLANES):
            win_ref[phase, c, CONV_HALO - phase:CONV_HALO - phase + tb, :] = u_lanes
    first_tap = CONV_HALO - (CONV_KERNEL - 1)

    for c in range(D_CONV // LANES):
        lanes = slice(c * LANES, (c + 1) * LANES)
        taps = []
        for phase in range(SUBLANES):
            for base in range(0, CONV_HALO + SUBLANES, SUBLANES):
                tap = base + phase - first_tap
                if 0 <= tap < CONV_KERNEL:
                    taps.append((cw_ref[tap, :, lanes], phase, base))
        bias = cb_ref[:, lanes]

        def conv_tile(i, carry, c=c, lanes=lanes, taps=taps, bias=bias):
            r0 = pl.multiple_of(i * CONV_ROWS, CONV_ROWS)
            parts = [jnp.broadcast_to(bias, (CONV_ROWS, LANES))] + [None] * (CONV_PARTIALS - 1)
            for n, (w8, phase, base) in enumerate(taps):
                w_rows = jnp.concatenate([w8] * (CONV_ROWS // SUBLANES), axis=0)
                term = w_rows * win_ref[phase, c, pl.ds(r0 + base, CONV_ROWS), :]
                p = n % CONV_PARTIALS
                parts[p] = term if parts[p] is None else parts[p] + term
            pre_ref[pl.ds(r0, CONV_ROWS), lanes] = functools.reduce(lambda a, b: a + b, parts)
            return carry

        lax.fori_loop(0, tb // CONV_ROWS, conv_tile, 0)

    for r0 in range(0, tb, NORM_ROWS):
        acc = pre_ref[r0:r0 + NORM_ROWS, :]
        mu = jnp.mean(acc, axis=-1, keepdims=True)
        cen = acc - mu
        var = jnp.mean(cen * cen, axis=-1, keepdims=True)
        yn = cen * lax.rsqrt(var + EPS) * lng_ref[...] + lnb_ref[...]
        y_ref[0, r0:r0 + NORM_ROWS, 0:D_CONV] = (yn * _sigmoid(yn)).astype(y_ref.dtype)
    win_ref[:, :, 0:CONV_HALO, :] = win_ref[:, :, tb:tb + CONV_HALO, :]

    n_chunks = tb // lc
    row = lax.broadcasted_iota(jnp.int32, (lc, lc), 0)
    col = lax.broadcasted_iota(jnp.int32, (lc, lc), 1)
    tril = (col <= row).astype(BF16)
    klane = lax.broadcasted_iota(jnp.int32, (1, D_GLA_K), 1) // HEAD_K
    srow = lax.broadcasted_iota(jnp.int32, (GLA_HEADS * lc, lc), 0)
    scol = lax.broadcasted_iota(jnp.int32, (GLA_HEADS * lc, lc), 1)
    causal = scol <= (srow % lc)
    st_row = lax.broadcasted_iota(jnp.int32, (D_GLA_V, D_GLA_K), 0) // HEAD_V
    st_col = lax.broadcasted_iota(jnp.int32, (D_GLA_V, D_GLA_K), 1) // HEAD_K
    st_mask = (st_row == st_col).astype(F32)
    rows1 = lax.broadcasted_iota(jnp.int32, (lc, 1), 0)

    splits = [_split3(la_ref[0, ci * lc:(ci + 1) * lc, :]) for ci in range(n_chunks)]
    cums = [_dot(tril, hi) + _dot(tril, mid) + _dot(tril, lo) for hi, mid, lo in splits]
    decay = None
    for ci, cum in enumerate(cums):
        cum_ref[ci * lc:(ci + 1) * lc, :] = cum
        total = -cum[lc - 1:lc, :]
        decay = total if decay is None else jnp.maximum(decay, total)
    max_decay = jnp.max(decay)

    def intra_factorised(ci, q, k, cum):
        ref = cum[lc // 2 - 1:lc // 2, :]
        qt = (q * jnp.exp(cum - ref)).astype(BF16)
        kt = (k * jnp.exp(ref - cum)).astype(BF16)
        qbd = jnp.concatenate(
            [jnp.where(klane == h, qt, jnp.zeros_like(qt)) for h in range(GLA_HEADS)], axis=0)
        s = _dot_nt(qbd, kt)
        s = jnp.where(causal, s, 0.0).astype(BF16)
        heads = []
        for h in range(GLA_HEADS):
            vh = v_ref[0, ci * lc:(ci + 1) * lc, h * HEAD_V:(h + 1) * HEAD_V].astype(BF16)
            heads.append(_dot(s[h * lc:(h + 1) * lc, :], vh))
        return jnp.concatenate(heads, axis=-1)

    def intra_exact(ci, q, k, cum):
        e_row = lax.broadcasted_iota(jnp.int32, (D_GLA_K, D_GLA_V), 0) // HEAD_K
        e_col = lax.broadcasted_iota(jnp.int32, (D_GLA_K, D_GLA_V), 1) // HEAD_V
        expand = (e_row == e_col).astype(BF16)

        def key_row(j, o):
            kj = k_ref[0, pl.ds(ci * lc + j, 1), :]
            cj = cum_ref[pl.ds(ci * lc + j, 1), :]
            vj = v_ref[0, pl.ds(ci * lc + j, 1), :]
            p = q * kj * jnp.exp(jnp.minimum(cum - cj, 0.0))
            p = jnp.where(rows1 >= j, p, 0.0).astype(BF16)
            return o + _dot(p, expand) * vj

        return lax.fori_loop(0, lc, key_row, jnp.zeros((lc, D_GLA_V), F32))

    def gla_tile(intra):
        for ci in range(n_chunks):
            rows = slice(ci * lc, (ci + 1) * lc)
            q = q_ref[0, rows, :]
            k = k_ref[0, rows, :]
            cum = cum_ref[rows, :]
            cum_last = cum[lc - 1:lc, :]
            o_intra = intra(ci, q, k, cum)

            st = st_ref[...]
            o_inter = _dot_nt((q * jnp.exp(cum)).astype(BF16), st.astype(BF16))
            kh = (k * jnp.exp(cum_last - cum)).astype(BF16)
            vb = v_ref[0, rows, :].astype(BF16)
            st_ref[...] = st * jnp.exp(cum_last) + _dot_tn(vb, kh) * st_mask

            o = o_inter + o_intra
            normed = []
            for h in range(GLA_HEADS):
                oh = o[:, h * HEAD_V:(h + 1) * HEAD_V]
                ms = jnp.mean(oh * oh, axis=-1, keepdims=True)
                normed.append(oh * lax.rsqrt(ms + EPS))
            on = jnp.concatenate(normed, axis=-1) * gng_ref[...]
            y_gla = on * gate_ref[0, rows, :]
            y_ref[0, rows, D_CONV:D_CONV + D_GLA_V] = y_gla.astype(y_ref.dtype)

    @pl.when(max_decay <= GLA_MAX_FACTORISED_DECAY)
    def _():
        gla_tile(intra_factorised)

    @pl.when(max_decay > GLA_MAX_FACTORISED_DECAY)
    def _():
        gla_tile(intra_exact)


def _mixer(u, q, k, la, v, gate, cw, cb, lng, lnb, gng):
    b_sz, t_len, _ = u.shape
    tb = TOKENS_MIX
    assert t_len % tb == 0 and tb % GLA_CHUNK == 0 and tb % CONV_ROWS == 0 and tb % NORM_ROWS == 0

    def tok(width):
        return pl.BlockSpec((1, tb, width), lambda b, t: (b, t, 0))

    def const(arr):
        return pl.BlockSpec(arr.shape, lambda b, t: (0,) * arr.ndim)

    return pl.pallas_call(
        _mixer_kernel,
        grid=(b_sz, t_len // tb),
        in_specs=[tok(D_CONV), tok(D_GLA_K), tok(D_GLA_K), tok(D_GLA_K), tok(D_GLA_V),
                  tok(D_GLA_V), const(cw), const(cb), const(lng), const(lnb), const(gng)],
        out_specs=tok(D_CONV + D_GLA_V),
        out_shape=jax.ShapeDtypeStruct((b_sz, t_len, D_CONV + D_GLA_V), BF16),
        scratch_shapes=[
            pltpu.VMEM((SUBLANES, D_CONV // LANES, CONV_HALO + tb, LANES), F32),
            pltpu.VMEM((tb, D_CONV), F32),
            pltpu.VMEM((D_GLA_V, D_GLA_K), F32),
            pltpu.VMEM((tb, D_GLA_K), F32),
        ],
        compiler_params=pltpu.CompilerParams(
            dimension_semantics=("arbitrary", "arbitrary"), vmem_limit_bytes=VMEM_LIMIT_BYTES),
        name="mixer",
    )(u, q, k, la, v, gate, cw, cb, lng, lnb, gng)


def _out_ffn_kernel(x_ref, y_ref, gt1_ref, sc2_ref, sh2_ref, gt2_ref, scf_ref, shf_ref,
                    gffn_ref, gfin_ref, wout_hbm, wgu_hbm, wo_hbm, o_ref,
                    wout_ref, wgu_ref, wo_ref, stage_wide_ref, stage_ref, sem_wide, sem):
    @pl.when((pl.program_id(0) == 0) & (pl.program_id(1) == 0))
    def _():
        _stage_bf16(wout_hbm, wout_ref, stage_ref, sem)
        _stage_bf16(wgu_hbm, wgu_ref, stage_wide_ref, sem_wide)
        _stage_bf16(wo_hbm, wo_ref, stage_ref, sem)

    tm = x_ref.shape[1]
    groups = [slice(r, r + FFN_ROWS) for r in range(0, tm, FFN_ROWS)]

    def out_proj(rows):
        return x_ref[0, rows, :] + gt1_ref[0] * _dot(y_ref[0, rows, :], wout_ref[...])

    def ffn_input(x1):
        ms = jnp.mean(x1 * x1, axis=-1, keepdims=True)
        h = x1 * lax.rsqrt(ms + EPS) * gffn_ref[...]
        return (h * (1.0 + sc2_ref[0]) + sh2_ref[0]).astype(BF16)

    def ffn(x1, hb):
        acc = jnp.zeros_like(x1)
        for lo, hi in FFN_SLABS:
            gate = _dot(hb, wgu_ref[:, lo:hi])
            up = _dot(hb, wgu_ref[:, D_FF + lo:D_FF + hi])
            act = (gate * _sigmoid(gate) * up).astype(BF16)
            acc = acc + _dot(act, wo_ref[lo:hi, :])
        return x1 + gt2_ref[0] * acc

    def final_norm(rows, x2):
        ms = jnp.mean(x2 * x2, axis=-1, keepdims=True)
        xn = x2 * lax.rsqrt(ms + EPS) * gfin_ref[...]
        o_ref[0, rows, :] = xn * (1.0 + scf_ref[0]) + shf_ref[0]

    x1s = [out_proj(rows) for rows in groups]
    pending = None
    for rows, x1 in zip(groups, x1s):
        hb = ffn_input(x1)
        if pending is not None:
            final_norm(*pending)
        pending = (rows, ffn(x1, hb))
    final_norm(*pending)


def _out_ffn(x, y, mod, gffn, gfin, wout, wgu, wo):
    b_sz, t_len, d = x.shape
    tm = TOKENS_FFN
    assert t_len % tm == 0 and tm % FFN_ROWS == 0

    def tok(width):
        return pl.BlockSpec((1, tm, width), lambda b, t: (b, t, 0))

    def const(arr):
        return pl.BlockSpec(arr.shape, lambda b, t: (0,) * arr.ndim,
                            pipeline_mode=pl.Buffered(1))

    mods = (MOD_GT1, MOD_SC2, MOD_SH2, MOD_GT2, MOD_SCF, MOD_SHF)
    in_hbm = pl.BlockSpec(memory_space=pl.ANY)
    return pl.pallas_call(
        _out_ffn_kernel,
        grid=(b_sz, t_len // tm),
        in_specs=[tok(d), tok(d)] + [_mod_spec(m) for m in mods]
                 + [const(gffn), const(gfin), in_hbm, in_hbm, in_hbm],
        out_specs=tok(d),
        out_shape=jax.ShapeDtypeStruct((b_sz, t_len, d), F32),
        scratch_shapes=[
            pltpu.VMEM(wout.shape, BF16),
            pltpu.VMEM(wgu.shape, BF16),
            pltpu.VMEM(wo.shape, BF16),
            pltpu.VMEM((2, STAGE_ROWS_WIDE, wgu.shape[1]), F32),
            pltpu.VMEM((2, STAGE_ROWS, d), F32),
            pltpu.SemaphoreType.DMA((2,)),
            pltpu.SemaphoreType.DMA((2,)),
        ],
        compiler_params=pltpu.CompilerParams(
            dimension_semantics=("arbitrary", "arbitrary"), vmem_limit_bytes=VMEM_LIMIT_BYTES),
        name="out_ffn",
    )(x, y, *([mod] * len(mods)), gffn, gfin, wout, wgu, wo)


def kernel(x, c, w_ada, b_ada, g_mix, w_in, conv_w, conv_b, ln_g, ln_b, w_a2, b_a2,
           gla_norm_g, w_out, g_ffn, w_ffn_in, w_ffn_out, w_ada_final, b_ada_final, g_final):
    b_sz, t_len, d = x.shape
    assert w_ada.shape[0] == 1, "single-layer block"
    assert d == D_MODEL

    c_lanes = jnp.broadcast_to(c[:, :, None], (b_sz, d, LANES))
    mod = _modulation(c_lanes, w_ada[0], b_ada[0], w_ada_final, b_ada_final)
    mod = mod.reshape(b_sz, 1, 8 * d)

    wa2 = jnp.pad(w_a2[0], ((0, LANES - GATE_RANK), (0, 0))).astype(BF16)

    u, q, k, v, gate, la = _in_proj(
        x, mod, g_mix[0].reshape(1, d), w_in[0], wa2, b_a2[0].reshape(1, D_GLA_K))

    cw = jnp.broadcast_to(conv_w[0][:, None, :], (CONV_KERNEL, SUBLANES, D_CONV))
    y = _mixer(u, q, k, la, v, gate, cw, conv_b[0].reshape(1, D_CONV),
               ln_g[0].reshape(1, D_CONV), ln_b[0].reshape(1, D_CONV),
               jnp.tile(gla_norm_g[0], GLA_HEADS).reshape(1, D_GLA_V))

    return _out_ffn(
        x, y, mod, g_ffn[0].reshape(1, d), g_final.reshape(1, d),
        w_out[0], w_ffn_in[0], w_ffn_out[0])
```

```python
import functools

import jax
import jax.numpy as jnp
from jax import lax
from jax.experimental import pallas as pl
from jax.experimental.pallas import tpu as pltpu

F32 = jnp.float32
BF16 = jnp.bfloat16

D_MODEL = 1024
D_CONV = 512
CONV_KERNEL = 31
D_GLA_V = 512
D_GLA_K = 256
GLA_HEADS = 4
HEAD_K = D_GLA_K // GLA_HEADS
HEAD_V = D_GLA_V // GLA_HEADS
GATE_RANK = 16
GATE_TAU = 16.0
D_FF = 2816
EPS = 1e-6

LANES = 128
SUBLANES = 8
MXU_DIM = 256
VMEM_LIMIT_BYTES = 56 * 1024 * 1024
MOD_ROWS = 128
TOKENS_IN = 1024
TOKENS_MIX = 512
TOKENS_FFN = 512
STAGE_ROWS_WIDE = 128
STAGE_ROWS = 512
CONV_HALO = 32
CONV_ROWS = 64
CONV_PARTIALS = 2
NORM_ROWS = 128
GLA_CHUNK = 128
GLA_MAX_FACTORISED_DECAY = 40.0
FFN_ROWS = 256
FFN_SLABS = ((0, 6 * MXU_DIM), (6 * MXU_DIM, D_FF))

IN_COLS = {}
_col = 0
for _name, _width in (("cv", D_CONV), ("cg", D_CONV), ("q", D_GLA_K), ("k", D_GLA_K),
                      ("v", D_GLA_V), ("og", D_GLA_V), ("a", LANES)):
    IN_COLS[_name] = (_col, _col + _width)
    _col += _width

MOD_SH1, MOD_SC1, MOD_GT1, MOD_SH2, MOD_SC2, MOD_GT2, MOD_SHF, MOD_SCF = range(8)


def _sigmoid(x):
    return 1.0 / (1.0 + jnp.exp(-x))


def _dot(a, b):
    return jnp.dot(a, b, preferred_element_type=F32)


def _dot_nt(a, b):
    return lax.dot_general(a, b, (((1,), (1,)), ((), ())), preferred_element_type=F32)


def _dot_tn(a, b):
    return lax.dot_general(a, b, (((0,), (0,)), ((), ())), preferred_element_type=F32)


def _split3(a):
    hi = a.astype(BF16)
    r1 = a - hi.astype(F32)
    mid = r1.astype(BF16)
    lo = (r1 - mid.astype(F32)).astype(BF16)
    return hi, mid, lo


def _stage_bf16(src_hbm, dst_ref, stage_ref, sem):
    n_rows, n_cols = src_hbm.shape
    chunk = stage_ref.shape[1]
    starts = list(range(0, n_rows, chunk))

    def copy(i):
        r, n = starts[i], min(chunk, n_rows - starts[i])
        return pltpu.make_async_copy(
            src_hbm.at[pl.ds(r, n), :], stage_ref.at[i % 2, pl.ds(0, n), :], sem.at[i % 2])

    copy(0).start()
    for i, r in enumerate(starts):
        if i + 1 < len(starts):
            copy(i + 1).start()
        copy(i).wait()
        n = min(chunk, n_rows - r)
        dst_ref[r:r + n, 0:n_cols] = stage_ref[i % 2, 0:n, :].astype(dst_ref.dtype)


def _modulation_kernel(c_ref, w_ref, b_ref, wf_ref, bf_ref, o_ref, acc_ref):
    j = pl.program_id(0)
    n_batch, k_rows, _ = c_ref.shape
    n = w_ref.shape[1]

    @pl.when(j == 0)
    def _():
        acc_ref[...] = jnp.zeros_like(acc_ref)

    c_act = []
    for b in range(n_batch):
        cb = c_ref[b]
        c_act.append(cb * _sigmoid(cb))
    for ref, offset in ((w_ref, 0), (wf_ref, n)):
        for t in range(ref.shape[1] // LANES):
            w = ref[:, t * LANES:(t + 1) * LANES]
            cols = slice(offset + t * LANES, offset + (t + 1) * LANES)
            for b in range(n_batch):
                prod = (w * c_act[b]).reshape(k_rows // SUBLANES, SUBLANES, LANES)
                acc_ref[b, :, cols] += jnp.sum(prod, axis=0)

    @pl.when(j == pl.num_programs(0) - 1)
    def _():
        bias = jnp.concatenate([b_ref[...], bf_ref[...]], axis=1)
        for b in range(n_batch):
            o_ref[b] = jnp.sum(acc_ref[b], axis=0, keepdims=True) + bias


def _modulation(c_lanes, w, b, wf, bf):
    n_batch, d, _ = c_lanes.shape
    n, nf = w.shape[1], wf.shape[1]
    assert d % MOD_ROWS == 0 and n % LANES == 0 and nf % LANES == 0
    return pl.pallas_call(
        _modulation_kernel,
        grid=(d // MOD_ROWS,),
        in_specs=[
            pl.BlockSpec((n_batch, MOD_ROWS, LANES), lambda j: (0, j, 0)),
            pl.BlockSpec((MOD_ROWS, n), lambda j: (j, 0)),
            pl.BlockSpec((1, n), lambda j: (0, 0)),
            pl.BlockSpec((MOD_ROWS, nf), lambda j: (j, 0)),
            pl.BlockSpec((1, nf), lambda j: (0, 0)),
        ],
        out_specs=pl.BlockSpec((n_batch, 1, n + nf), lambda j: (0, 0, 0)),
        out_shape=jax.ShapeDtypeStruct((n_batch, 1, n + nf), F32),
        scratch_shapes=[pltpu.VMEM((n_batch, SUBLANES, n + nf), F32)],
        compiler_params=pltpu.CompilerParams(
            dimension_semantics=("arbitrary",), vmem_limit_bytes=VMEM_LIMIT_BYTES),
        name="modulation",
    )(c_lanes, w, b.reshape(1, n), wf, bf.reshape(1, nf))


def _mod_spec(which):
    return pl.BlockSpec((1, 1, D_MODEL), lambda b, t: (b, 0, which))


def _in_proj_kernel(x_ref, sc_ref, sh_ref, g_ref, w_ref, wa2_ref, ba2_ref,
                    u_ref, q_ref, k_ref, v_ref, gate_ref, la_ref):
    x = x_ref[0]
    ms = jnp.mean(x * x, axis=-1, keepdims=True)
    h = x * lax.rsqrt(ms + EPS) * g_ref[...]
    h = h * (1.0 + sc_ref[0]) + sh_ref[0]
    hb = h.astype(BF16)

    proj = _dot(hb, w_ref[...])

    def piece(name):
        lo, hi = IN_COLS[name]
        return proj[:, lo:hi]

    u_ref[0] = piece("cv") * _sigmoid(piece("cg"))
    q_ref[0] = piece("q") * (HEAD_K ** -0.5)
    k_ref[0] = piece("k")
    v_ref[0] = piece("v")
    og = piece("og")
    gate_ref[0] = og * _sigmoid(og)

    a_low = piece("a")
    z = _dot(a_low.astype(BF16), wa2_ref[...]) + ba2_ref[...]
    log_sig = jnp.minimum(z, 0.0) - jnp.log(1.0 + jnp.exp(-jnp.abs(z)))
    la_ref[0] = log_sig * (1.0 / GATE_TAU)


def _in_proj(x, mod, g, w, wa2, ba2):
    b_sz, t_len, d = x.shape
    tm = TOKENS_IN
    assert t_len % tm == 0

    def tok(width):
        return pl.BlockSpec((1, tm, width), lambda b, t: (b, t, 0))

    def const(arr):
        return pl.BlockSpec(arr.shape, lambda b, t: (0,) * arr.ndim)

    def out(width):
        return jax.ShapeDtypeStruct((b_sz, t_len, width), F32)

    return pl.pallas_call(
        _in_proj_kernel,
        grid=(b_sz, t_len // tm),
        in_specs=[tok(d), _mod_spec(MOD_SC1), _mod_spec(MOD_SH1), const(g), const(w),
                  const(wa2), const(ba2)],
        out_specs=[tok(D_CONV), tok(D_GLA_K), tok(D_GLA_K), tok(D_GLA_V), tok(D_GLA_V),
                   tok(D_GLA_K)],
        out_shape=[out(D_CONV), out(D_GLA_K), out(D_GLA_K), out(D_GLA_V), out(D_GLA_V),
                   out(D_GLA_K)],
        compiler_params=pltpu.CompilerParams(
            dimension_semantics=("arbitrary", "arbitrary"), vmem_limit_bytes=VMEM_LIMIT_BYTES),
        name="in_proj",
    )(x, mod, mod, g, w, wa2, ba2)


def _mixer_kernel(u_ref, q_ref, k_ref, la_ref, v_ref, gate_ref, cw_ref, cb_ref,
                  lng_ref, lnb_ref, gng_ref, y_ref, win_ref, pre_ref, st_ref, cum_ref):
    tb = u_ref.shape[1]
    lc = GLA_CHUNK

    @pl.when(pl.program_id(1) == 0)
    def _():
        win_ref[...] = jnp.zeros_like(win_ref)
        st_ref[...] = jnp.zeros_like(st_ref)

    for c in range(D_CONV // LANES):
        u_lanes = u_ref[0, :, c * LANES:(c + 1) * LANES]
        for phase in range(SUBLANES):
            win_ref[phase, c, CONV_HALO - phase:CONV_HALO - phase + tb, :] = u_lanes
    first_tap = CONV_HALO - (CONV_KERNEL - 1)

    for c in range(D_CONV // LANES):
        lanes = slice(c * LANES, (c + 1) * LANES)
        taps = []
        for phase in range(SUBLANES):
            for base in range(0, CONV_HALO + SUBLANES, SUBLANES):
                tap = base + phase - first_tap
                if 0 <= tap < CONV_KERNEL:
                    taps.append((cw_ref[tap, :, lanes], phase, base))
        bias = cb_ref[:, lanes]

        def conv_tile(i, carry, c=c, lanes=lanes, taps=taps, bias=bias):
            r0 = pl.multiple_of(i * CONV_ROWS, CONV_ROWS)
            parts = [jnp.broadcast_to(bias, (CONV_ROWS, LANES))] + [None] * (CONV_PARTIALS - 1)
            for n, (w8, phase, base) in enumerate(taps):
                w_rows = jnp.concatenate([w8] * (CONV_ROWS // SUBLANES), axis=0)
                term = w_rows * win_ref[phase, c, pl.ds(r0 + base, CONV_ROWS), :]
                p = n % CONV_PARTIALS
                parts[p] = term if parts[p] is None else parts[p] + term
            pre_ref[pl.ds(r0, CONV_ROWS), lanes] = functools.reduce(lambda a, b: a + b, parts)
            return carry

        lax.fori_loop(0, tb // CONV_ROWS, conv_tile, 0)

    for r0 in range(0, tb, NORM_ROWS):
        acc = pre_ref[r0:r0 + NORM_ROWS, :]
        mu = jnp.mean(acc, axis=-1, keepdims=True)
        cen = acc - mu
        var = jnp.mean(cen * cen, axis=-1, keepdims=True)
        yn = cen * lax.rsqrt(var + EPS) * lng_ref[...] + lnb_ref[...]
        y_ref[0, r0:r0 + NORM_ROWS, 0:D_CONV] = (yn * _sigmoid(yn)).astype(y_ref.dtype)
    win_ref[:, :, 0:CONV_HALO, :] = win_ref[:, :, tb:tb + CONV_HALO, :]

    n_chunks = tb // lc
    row = lax.broadcasted_iota(jnp.int32, (lc, lc), 0)
    col = lax.broadcasted_iota(jnp.int32, (lc, lc), 1)
    tril = (col <= row).astype(BF16)
    klane = lax.broadcasted_iota(jnp.int32, (1, D_GLA_K), 1) // HEAD_K
    srow = lax.broadcasted_iota(jnp.int32, (GLA_HEADS * lc, lc), 0)
    scol = lax.broadcasted_iota(jnp.int32, (GLA_HEADS * lc, lc), 1)
    causal = scol <= (srow % lc)
    st_row = lax.broadcasted_iota(jnp.int32, (D_GLA_V, D_GLA_K), 0) // HEAD_V
    st_col = lax.broadcasted_iota(jnp.int32, (D_GLA_V, D_GLA_K), 1) // HEAD_K
    st_mask = (st_row == st_col).astype(F32)
    rows1 = lax.broadcasted_iota(jnp.int32, (lc, 1), 0)

    splits = [_split3(la_ref[0, ci * lc:(ci + 1) * lc, :]) for ci in range(n_chunks)]
    cums = [_dot(tril, hi) + _dot(tril, mid) + _dot(tril, lo) for hi, mid, lo in splits]
    decay = None
    for ci, cum in enumerate(cums):
        cum_ref[ci * lc:(ci + 1) * lc, :] = cum
        total = -cum[lc - 1:lc, :]
        decay = total if decay is None else jnp.maximum(decay, total)
    max_decay = jnp.max(decay)

    def intra_factorised(ci, q, k, cum):
        ref = cum[lc // 2 - 1:lc // 2, :]
        qt = (q * jnp.exp(cum - ref)).astype(BF16)
        kt = (k * jnp.exp(ref - cum)).astype(BF16)
        qbd = jnp.concatenate(
            [jnp.where(klane == h, qt, jnp.zeros_like(qt)) for h in range(GLA_HEADS)], axis=0)
        s = _dot_nt(qbd, kt)
        s = jnp.where(causal, s, 0.0).astype(BF16)
        heads = []
        for h in range(GLA_HEADS):
            vh = v_ref[0, ci * lc:(ci + 1) * lc, h * HEAD_V:(h + 1) * HEAD_V].astype(BF16)
            heads.append(_dot(s[h * lc:(h + 1) * lc, :], vh))
        return jnp.concatenate(heads, axis=-1)

    def intra_exact(ci, q, k, cum):
        e_row = lax.broadcasted_iota(jnp.int32, (D_GLA_K, D_GLA_V), 0) // HEAD_K
        e_col = lax.broadcasted_iota(jnp.int32, (D_GLA_K, D_GLA_V), 1) // HEAD_V
        expand = (e_row == e_col).astype(BF16)

        def key_row(j, o):
            kj = k_ref[0, pl.ds(ci * lc + j, 1), :]
            cj = cum_ref[pl.ds(ci * lc + j, 1), :]
            vj = v_ref[0, pl.ds(ci * lc + j, 1), :]
            p = q * kj * jnp.exp(jnp.minimum(cum - cj, 0.0))
            p = jnp.where(rows1 >= j, p, 0.0).astype(BF16)
            return o + _dot(p, expand) * vj

        return lax.fori_loop(0, lc, key_row, jnp.zeros((lc, D_GLA_V), F32))

    def gla_tile(intra):
        for ci in range(n_chunks):
            rows = slice(ci * lc, (ci + 1) * lc)
            q = q_ref[0, rows, :]
            k = k_ref[0, rows, :]
            cum = cum_ref[rows, :]
            cum_last = cum[lc - 1:lc, :]
            o_intra = intra(ci, q, k, cum)

            st = st_ref[...]
            o_inter = _dot_nt((q * jnp.exp(cum)).astype(BF16), st.astype(BF16))
            kh = (k * jnp.exp(cum_last - cum)).astype(BF16)
            vb = v_ref[0, rows, :].astype(BF16)
            st_ref[...] = st * jnp.exp(cum_last) + _dot_tn(vb, kh) * st_mask

            o = o_inter + o_intra
            normed = []
            for h in range(GLA_HEADS):
                oh = o[:, h * HEAD_V:(h + 1) * HEAD_V]
                ms = jnp.mean(oh * oh, axis=-1, keepdims=True)
                normed.append(oh * lax.rsqrt(ms + EPS))
            on = jnp.concatenate(normed, axis=-1) * gng_ref[...]
            y_gla = on * gate_ref[0, rows, :]
            y_ref[0, rows, D_CONV:D_CONV + D_GLA_V] = y_gla.astype(y_ref.dtype)

    @pl.when(max_decay <= GLA_MAX_FACTORISED_DECAY)
    def _():
        gla_tile(intra_factorised)

    @pl.when(max_decay > GLA_MAX_FACTORISED_DECAY)
    def _():
        gla_tile(intra_exact)


def _mixer(u, q, k, la, v, gate, cw, cb, lng, lnb, gng):
    b_sz, t_len, _ = u.shape
    tb = TOKENS_MIX
    assert t_len % tb == 0 and tb % GLA_CHUNK == 0 and tb % CONV_ROWS == 0 and tb % NORM_ROWS == 0

    def tok(width):
        return pl.BlockSpec((1, tb, width), lambda b, t: (b, t, 0))

    def const(arr):
        return pl.BlockSpec(arr.shape, lambda b, t: (0,) * arr.ndim)

    return pl.pallas_call(
        _mixer_kernel,
        grid=(b_sz, t_len // tb),
        in_specs=[tok(D_CONV), tok(D_GLA_K), tok(D_GLA_K), tok(D_GLA_K), tok(D_GLA_V),
                  tok(D_GLA_V), const(cw), const(cb), const(lng), const(lnb), const(gng)],
        out_specs=tok(D_CONV + D_GLA_V),
        out_shape=jax.ShapeDtypeStruct((b_sz, t_len, D_CONV + D_GLA_V), BF16),
        scratch_shapes=[
            pltpu.VMEM((SUBLANES, D_CONV // LANES, CONV_HALO + tb, LANES), F32),
            pltpu.VMEM((tb, D_CONV), F32),
            pltpu.VMEM((D_GLA_V, D_GLA_K), F32),
            pltpu.VMEM((tb, D_GLA_K), F32),
        ],
        compiler_params=pltpu.CompilerParams(
            dimension_semantics=("arbitrary", "arbitrary"), vmem_limit_bytes=VMEM_LIMIT_BYTES),
        name="mixer",
    )(u, q, k, la, v, gate, cw, cb, lng, lnb, gng)


def _out_ffn_kernel(x_ref, y_ref, gt1_ref, sc2_ref, sh2_ref, gt2_ref, scf_ref, shf_ref,
                    gffn_ref, gfin_ref, wout_hbm, wgu_hbm, wo_hbm, o_ref,
                    wout_ref, wgu_ref, wo_ref, stage_wide_ref, stage_ref, sem_wide, sem):
    @pl.when((pl.program_id(0) == 0) & (pl.program_id(1) == 0))
    def _():
        _stage_bf16(wout_hbm, wout_ref, stage_ref, sem)
        _stage_bf16(wgu_hbm, wgu_ref, stage_wide_ref, sem_wide)
        _stage_bf16(wo_hbm, wo_ref, stage_ref, sem)

    tm = x_ref.shape[1]
    groups = [slice(r, r + FFN_ROWS) for r in range(0, tm, FFN_ROWS)]

    def out_proj(rows):
        return x_ref[0, rows, :] + gt1_ref[0] * _dot(y_ref[0, rows, :], wout_ref[...])

    def ffn_input(x1):
        ms = jnp.mean(x1 * x1, axis=-1, keepdims=True)
        h = x1 * lax.rsqrt(ms + EPS) * gffn_ref[...]
        return (h * (1.0 + sc2_ref[0]) + sh2_ref[0]).astype(BF16)

    def ffn(x1, hb):
        acc = jnp.zeros_like(x1)
        for lo, hi in FFN_SLABS:
            gate = _dot(hb, wgu_ref[:, lo:hi])
            up = _dot(hb, wgu_ref[:, D_FF + lo:D_FF + hi])
            act = (gate * _sigmoid(gate) * up).astype(BF16)
            acc = acc + _dot(act, wo_ref[lo:hi, :])
        return x1 + gt2_ref[0] * acc

    def final_norm(rows, x2):
        ms = jnp.mean(x2 * x2, axis=-1, keepdims=True)
        xn = x2 * lax.rsqrt(ms + EPS) * gfin_ref[...]
        o_ref[0, rows, :] = xn * (1.0 + scf_ref[0]) + shf_ref[0]

    x1s = [out_proj(rows) for rows in groups]
    pending = None
    for rows, x1 in zip(groups, x1s):
        hb = ffn_input(x1)
        if pending is not None:
            final_norm(*pending)
        pending = (rows, ffn(x1, hb))
    final_norm(*pending)


def _out_ffn(x, y, mod, gffn, gfin, wout, wgu, wo):
    b_sz, t_len, d = x.shape
    tm = TOKENS_FFN
    assert t_len % tm == 0 and tm % FFN_ROWS == 0

    def tok(width):
        return pl.BlockSpec((1, tm, width), lambda b, t: (b, t, 0))

    def const(arr):
        return pl.BlockSpec(arr.shape, lambda b, t: (0,) * arr.ndim,
                            pipeline_mode=pl.Buffered(1))

    mods = (MOD_GT1, MOD_SC2, MOD_SH2, MOD_GT2, MOD_SCF, MOD_SHF)
    in_hbm = pl.BlockSpec(memory_space=pl.ANY)
    return pl.pallas_call(
        _out_ffn_kernel,
        grid=(b_sz, t_len // tm),
        in_specs=[tok(d), tok(d)] + [_mod_spec(m) for m in mods]
                 + [const(gffn), const(gfin), in_hbm, in_hbm, in_hbm],
        out_specs=tok(d),
        out_shape=jax.ShapeDtypeStruct((b_sz, t_len, d), F32),
        scratch_shapes=[
            pltpu.VMEM(wout.shape, BF16),
            pltpu.VMEM(wgu.shape, BF16),
            pltpu.VMEM(wo.shape, BF16),
            pltpu.VMEM((2, STAGE_ROWS_WIDE, wgu.shape[1]), F32),
            pltpu.VMEM((2, STAGE_ROWS, d), F32),
            pltpu.SemaphoreType.DMA((2,)),
            pltpu.SemaphoreType.DMA((2,)),
        ],
        compiler_params=pltpu.CompilerParams(
            dimension_semantics=("arbitrary", "arbitrary"), vmem_limit_bytes=VMEM_LIMIT_BYTES),
        name="out_ffn",
    )(x, y, *([mod] * len(mods)), gffn, gfin, wout, wgu, wo)


def kernel(x, c, w_ada, b_ada, g_mix, w_in, conv_w, conv_b, ln_g, ln_b, w_a2, b_a2,
           gla_norm_g, w_out, g_ffn, w_ffn_in, w_ffn_out, w_ada_final, b_ada_final, g_final):
    b_sz, t_len, d = x.shape
    assert w_ada.shape[0] == 1, "single-layer block"
    assert d == D_MODEL

    c_lanes = jnp.broadcast_to(c[:, :, None], (b_sz, d, LANES))
    mod = _modulation(c_lanes, w_ada[0], b_ada[0], w_ada_final, b_ada_final)

    w = jnp.concatenate(
        [w_in[0].astype(BF16), jnp.zeros((d, LANES - GATE_RANK), BF16)], axis=1)
    wa2 = jnp.pad(w_a2[0], ((0, LANES - GATE_RANK), (0, 0))).astype(BF16)

    u, q, k, v, gate, la = _in_proj(
        x, mod, g_mix[0].reshape(1, d), w, wa2, b_a2[0].reshape(1, D_GLA_K))

    cw = jnp.broadcast_to(conv_w[0][:, None, :], (CONV_KERNEL, SUBLANES, D_CONV))
    y = _mixer(u, q, k, la, v, gate, cw, conv_b[0].reshape(1, D_CONV),
               ln_g[0].reshape(1, D_CONV), ln_b[0].reshape(1, D_CONV),
               jnp.tile(gla_norm_g[0], GLA_HEADS).reshape(1, D_GLA_V))

    return _out_ffn(
        x, y, mod, g_ffn[0].reshape(1, d), g_final.reshape(1, d),
        w_out[0], w_ffn_in[0], w_ffn_out[0])
```

```python
import functools

import jax
import jax.numpy as jnp
from jax import lax
from jax.experimental import pallas as pl
from jax.experimental.pallas import tpu as pltpu

F32 = jnp.float32
BF16 = jnp.bfloat16

D_MODEL = 1024
D_CONV = 512
CONV_KERNEL = 31
D_GLA_V = 512
D_GLA_K = 256
GLA_HEADS = 4
HEAD_K = D_GLA_K // GLA_HEADS
HEAD_V = D_GLA_V // GLA_HEADS
GATE_RANK = 16
GATE_TAU = 16.0
D_FF = 2816
EPS = 1e-6

LANES = 128
SUBLANES = 8
MXU_DIM = 256
VMEM_LIMIT_BYTES = 56 * 1024 * 1024
MOD_ROWS = 128
TOKENS_IN = 1024
TOKENS_MIX = 512
TOKENS_FFN = 512
STAGE_ROWS_WIDE = 128
STAGE_ROWS = 512
CONV_HALO = 32
CONV_ROWS = 64
CONV_PARTIALS = 2
NORM_ROWS = 128
GLA_CHUNK = 128
GLA_MAX_FACTORISED_DECAY = 40.0
FFN_ROWS = 256
FFN_SLABS = ((0, 6 * MXU_DIM), (6 * MXU_DIM, D_FF))

IN_COLS = {}
_col = 0
for _name, _width in (("cv", D_CONV), ("cg", D_CONV), ("q", D_GLA_K), ("k", D_GLA_K),
                      ("v", D_GLA_V), ("og", D_GLA_V), ("a", LANES)):
    IN_COLS[_name] = (_col, _col + _width)
    _col += _width

MOD_SH1, MOD_SC1, MOD_GT1, MOD_SH2, MOD_SC2, MOD_GT2, MOD_SHF, MOD_SCF = range(8)


def _sigmoid(x):
    return 1.0 / (1.0 + jnp.exp(-x))


def _dot(a, b):
    return jnp.dot(a, b, preferred_element_type=F32)


def _dot_nt(a, b):
    return lax.dot_general(a, b, (((1,), (1,)), ((), ())), preferred_element_type=F32)


def _dot_tn(a, b):
    return lax.dot_general(a, b, (((0,), (0,)), ((), ())), preferred_element_type=F32)


def _split3(a):
    hi = a.astype(BF16)
    r1 = a - hi.astype(F32)
    mid = r1.astype(BF16)
    lo = (r1 - mid.astype(F32)).astype(BF16)
    return hi, mid, lo


def _stage_bf16(src_hbm, dst_ref, stage_ref, sem):
    n_rows, n_cols = src_hbm.shape
    chunk = stage_ref.shape[1]
    starts = list(range(0, n_rows, chunk))

    def copy(i):
        r, n = starts[i], min(chunk, n_rows - starts[i])
        return pltpu.make_async_copy(
            src_hbm.at[pl.ds(r, n), :], stage_ref.at[i % 2, pl.ds(0, n), :], sem.at[i % 2])

    copy(0).start()
    for i, r in enumerate(starts):
        if i + 1 < len(starts):
            copy(i + 1).start()
        copy(i).wait()
        n = min(chunk, n_rows - r)
        dst_ref[r:r + n, 0:n_cols] = stage_ref[i % 2, 0:n, :].astype(dst_ref.dtype)


def _modulation_kernel(c_ref, w_ref, b_ref, wf_ref, bf_ref, o_ref, acc_ref):
    j = pl.program_id(0)
    n_batch, k_rows, _ = c_ref.shape
    n = w_ref.shape[1]

    @pl.when(j == 0)
    def _():
        acc_ref[...] = jnp.zeros_like(acc_ref)

    c_act = []
    for b in range(n_batch):
        cb = c_ref[b]
        c_act.append(cb * _sigmoid(cb))
    for ref, offset in ((w_ref, 0), (wf_ref, n)):
        for t in range(ref.shape[1] // LANES):
            w = ref[:, t * LANES:(t + 1) * LANES]
            cols = slice(offset + t * LANES, offset + (t + 1) * LANES)
            for b in range(n_batch):
                prod = (w * c_act[b]).reshape(k_rows // SUBLANES, SUBLANES, LANES)
                acc_ref[b, :, cols] += jnp.sum(prod, axis=0)

    @pl.when(j == pl.num_programs(0) - 1)
    def _():
        bias = jnp.concatenate([b_ref[...], bf_ref[...]], axis=1)
        for b in range(n_batch):
            o_ref[b] = jnp.sum(acc_ref[b], axis=0, keepdims=True) + bias


def _modulation(c_lanes, w, b, wf, bf):
    n_batch, d, _ = c_lanes.shape
    n, nf = w.shape[1], wf.shape[1]
    assert d % MOD_ROWS == 0 and n % LANES == 0 and nf % LANES == 0
    return pl.pallas_call(
        _modulation_kernel,
        grid=(d // MOD_ROWS,),
        in_specs=[
            pl.BlockSpec((n_batch, MOD_ROWS, LANES), lambda j: (0, j, 0)),
            pl.BlockSpec((MOD_ROWS, n), lambda j: (j, 0)),
            pl.BlockSpec((1, n), lambda j: (0, 0)),
            pl.BlockSpec((MOD_ROWS, nf), lambda j: (j, 0)),
            pl.BlockSpec((1, nf), lambda j: (0, 0)),
        ],
        out_specs=pl.BlockSpec((n_batch, 1, n + nf), lambda j: (0, 0, 0)),
        out_shape=jax.ShapeDtypeStruct((n_batch, 1, n + nf), F32),
        scratch_shapes=[pltpu.VMEM((n_batch, SUBLANES, n + nf), F32)],
        compiler_params=pltpu.CompilerParams(
            dimension_semantics=("arbitrary",), vmem_limit_bytes=VMEM_LIMIT_BYTES),
        name="modulation",
    )(c_lanes, w, b.reshape(1, n), wf, bf.reshape(1, nf))


def _mod_spec(which):
    return pl.BlockSpec((1, 1, D_MODEL), lambda b, t: (b, 0, which))


def _in_proj_kernel(x_ref, sc_ref, sh_ref, g_ref, w_ref, wa2_ref, ba2_ref,
                    u_ref, q_ref, k_ref, v_ref, gate_ref, la_ref):
    x = x_ref[0]
    ms = jnp.mean(x * x, axis=-1, keepdims=True)
    h = x * lax.rsqrt(ms + EPS) * g_ref[...]
    h = h * (1.0 + sc_ref[0]) + sh_ref[0]
    hb = h.astype(BF16)

    proj = _dot(hb, w_ref[...])

    def piece(name):
        lo, hi = IN_COLS[name]
        return proj[:, lo:hi]

    u_ref[0] = piece("cv") * _sigmoid(piece("cg"))
    q_ref[0] = piece("q") * (HEAD_K ** -0.5)
    k_ref[0] = piece("k")
    v_ref[0] = piece("v")
    og = piece("og")
    gate_ref[0] = og * _sigmoid(og)

    a_low = piece("a")
    z = _dot(a_low.astype(BF16), wa2_ref[...]) + ba2_ref[...]
    log_sig = jnp.minimum(z, 0.0) - jnp.log(1.0 + jnp.exp(-jnp.abs(z)))
    la_ref[0] = log_sig * (1.0 / GATE_TAU)


def _in_proj(x, mod, g, w, wa2, ba2):
    b_sz, t_len, d = x.shape
    tm = TOKENS_IN
    assert t_len % tm == 0

    def tok(width):
        return pl.BlockSpec((1, tm, width), lambda b, t: (b, t, 0))

    def const(arr):
        return pl.BlockSpec(arr.shape, lambda b, t: (0,) * arr.ndim)

    def out(width):
        return jax.ShapeDtypeStruct((b_sz, t_len, width), F32)

    return pl.pallas_call(
        _in_proj_kernel,
        grid=(b_sz, t_len // tm),
        in_specs=[tok(d), _mod_spec(MOD_SC1), _mod_spec(MOD_SH1), const(g), const(w),
                  const(wa2), const(ba2)],
        out_specs=[tok(D_CONV), tok(D_GLA_K), tok(D_GLA_K), tok(D_GLA_V), tok(D_GLA_V),
                   tok(D_GLA_K)],
        out_shape=[out(D_CONV), out(D_GLA_K), out(D_GLA_K), out(D_GLA_V), out(D_GLA_V),
                   out(D_GLA_K)],
        compiler_params=pltpu.CompilerParams(
            dimension_semantics=("arbitrary", "arbitrary"), vmem_limit_bytes=VMEM_LIMIT_BYTES),
        name="in_proj",
    )(x, mod, mod, g, w, wa2, ba2)


def _mixer_kernel(u_ref, q_ref, k_ref, la_ref, v_ref, gate_ref, cw_ref, cb_ref,
                  lng_ref, lnb_ref, gng_ref, y_ref, win_ref, pre_ref, st_ref, cum_ref):
    tb = u_ref.shape[1]
    lc = GLA_CHUNK

    @pl.when(pl.program_id(1) == 0)
    def _():
        win_ref[...] = jnp.zeros_like(win_ref)
        st_ref[...] = jnp.zeros_like(st_ref)

    for c in range(D_CONV // LANES):
        u_lanes = u_ref[0, :, c * LANES:(c + 1) * LANES]
        for phase in range(SUBLANES):
            win_ref[phase, c, CONV_HALO - phase:CONV_HALO - phase + tb, :] = u_lanes
    first_tap = CONV_HALO - (CONV_KERNEL - 1)

    for c in range(D_CONV // LANES):
        lanes = slice(c * LANES, (c + 1) * LANES)
        taps = []
        for phase in range(SUBLANES):
            for base in range(0, CONV_HALO + SUBLANES, SUBLANES):
                tap = base + phase - first_tap
                if 0 <= tap < CONV_KERNEL:
                    taps.append((cw_ref[tap, :, lanes], phase, base))
        bias = cb_ref[:, lanes]

        def conv_tile(i, carry, c=c, lanes=lanes, taps=taps, bias=bias):
            r0 = pl.multiple_of(i * CONV_ROWS, CONV_ROWS)
            parts = [jnp.broadcast_to(bias, (CONV_ROWS, LANES))] + [None] * (CONV_PARTIALS - 1)
            for n, (w8, phase, base) in enumerate(taps):
                w_rows = jnp.concatenate([w8] * (CONV_ROWS // SUBLANES), axis=0)
                term = w_rows * win_ref[phase, c, pl.ds(r0 + base, CONV_ROWS), :]
                p = n % CONV_PARTIALS
                parts[p] = term if parts[p] is None else parts[p] + term
            pre_ref[pl.ds(r0, CONV_ROWS), lanes] = functools.reduce(lambda a, b: a + b, parts)
            return carry

        lax.fori_loop(0, tb // CONV_ROWS, conv_tile, 0)

    for r0 in range(0, tb, NORM_ROWS):
        acc = pre_ref[r0:r0 + NORM_ROWS, :]
        mu = jnp.mean(acc, axis=-1, keepdims=True)
        cen = acc - mu
        var = jnp.mean(cen * cen, axis=-1, keepdims=True)
        yn = cen * lax.rsqrt(var + EPS) * lng_ref[...] + lnb_ref[...]
        y_ref[0, r0:r0 + NORM_ROWS, 0:D_CONV] = (yn * _sigmoid(yn)).astype(y_ref.dtype)
    win_ref[:, :, 0:CONV_HALO, :] = win_ref[:, :, tb:tb + CONV_HALO, :]

    n_chunks = tb // lc
    row = lax.broadcasted_iota(jnp.int32, (lc, lc), 0)
    col = lax.broadcasted_iota(jnp.int32, (lc, lc), 1)
    tril = (col <= row).astype(BF16)
    klane = lax.broadcasted_iota(jnp.int32, (1, D_GLA_K), 1) // HEAD_K
    srow = lax.broadcasted_iota(jnp.int32, (GLA_HEADS * lc, lc), 0)
    scol = lax.broadcasted_iota(jnp.int32, (GLA_HEADS * lc, lc), 1)
    causal = scol <= (srow % lc)
    rows1 = lax.broadcasted_iota(jnp.int32, (lc, 1), 0)

    splits = [_split3(la_ref[0, ci * lc:(ci + 1) * lc, :]) for ci in range(n_chunks)]
    cums = [_dot(tril, hi) + _dot(tril, mid) + _dot(tril, lo) for hi, mid, lo in splits]
    decay = None
    for ci, cum in enumerate(cums):
        cum_ref[ci * lc:(ci + 1) * lc, :] = cum
        total = -cum[lc - 1:lc, :]
        decay = total if decay is None else jnp.maximum(decay, total)
    max_decay = jnp.max(decay)

    def chunk_factorised(ci, q, k, cum, cum_last):
        ref = cum[lc // 2 - 1:lc // 2, :]
        q_dec = q * jnp.exp(cum)
        k_inc = k * jnp.exp(-cum)
        qt = (q_dec * jnp.exp(-ref)).astype(BF16)
        kt = (k_inc * jnp.exp(ref)).astype(BF16)
        qbd = jnp.concatenate(
            [jnp.where(klane == h, qt, jnp.zeros_like(qt)) for h in range(GLA_HEADS)], axis=0)
        s = _dot_nt(qbd, kt)
        s = jnp.where(causal, s, 0.0).astype(BF16)
        heads = []
        for h in range(GLA_HEADS):
            vh = v_ref[0, ci * lc:(ci + 1) * lc, h * HEAD_V:(h + 1) * HEAD_V].astype(BF16)
            heads.append(_dot(s[h * lc:(h + 1) * lc, :], vh))
        o_intra = jnp.concatenate(heads, axis=-1)
        return o_intra, q_dec.astype(BF16), (k_inc * jnp.exp(cum_last)).astype(BF16)

    def chunk_exact(ci, q, k, cum, cum_last):
        e_row = lax.broadcasted_iota(jnp.int32, (D_GLA_K, D_GLA_V), 0) // HEAD_K
        e_col = lax.broadcasted_iota(jnp.int32, (D_GLA_K, D_GLA_V), 1) // HEAD_V
        expand = (e_row == e_col).astype(BF16)

        def key_row(j, o):
            kj = k_ref[0, pl.ds(ci * lc + j, 1), :]
            cj = cum_ref[pl.ds(ci * lc + j, 1), :]
            vj = v_ref[0, pl.ds(ci * lc + j, 1), :]
            p = q * kj * jnp.exp(jnp.minimum(cum - cj, 0.0))
            p = jnp.where(rows1 >= j, p, 0.0).astype(BF16)
            return o + _dot(p, expand) * vj

        o_intra = lax.fori_loop(0, lc, key_row, jnp.zeros((lc, D_GLA_V), F32))
        return (o_intra, (q * jnp.exp(cum)).astype(BF16),
                (k * jnp.exp(cum_last - cum)).astype(BF16))

    def gla_tile(chunk_terms):
        for ci in range(n_chunks):
            rows = slice(ci * lc, (ci + 1) * lc)
            q = q_ref[0, rows, :]
            k = k_ref[0, rows, :]
            cum = cum_ref[rows, :]
            cum_last = cum[lc - 1:lc, :]
            o_intra, qh, kh = chunk_terms(ci, q, k, cum, cum_last)

            st = st_ref[...]
            o_inter = _dot_nt(qh, st.astype(BF16))
            vb = v_ref[0, rows, :].astype(BF16)
            update = jnp.concatenate(
                [_dot_tn(vb[:, h * HEAD_V:(h + 1) * HEAD_V],
                         jnp.where(klane == h, kh, jnp.zeros_like(kh)))
                 for h in range(GLA_HEADS)], axis=0)
            st_ref[...] = st * jnp.exp(cum_last) + update

            o = o_inter + o_intra
            normed = []
            for h in range(GLA_HEADS):
                oh = o[:, h * HEAD_V:(h + 1) * HEAD_V]
                ms = jnp.mean(oh * oh, axis=-1, keepdims=True)
                normed.append(oh * lax.rsqrt(ms + EPS))
            on = jnp.concatenate(normed, axis=-1) * gng_ref[...]
            y_gla = on * gate_ref[0, rows, :]
            y_ref[0, rows, D_CONV:D_CONV + D_GLA_V] = y_gla.astype(y_ref.dtype)

    @pl.when(max_decay <= GLA_MAX_FACTORISED_DECAY)
    def _():
        gla_tile(chunk_factorised)

    @pl.when(max_decay > GLA_MAX_FACTORISED_DECAY)
    def _():
        gla_tile(chunk_exact)


def _mixer(u, q, k, la, v, gate, cw, cb, lng, lnb, gng):
    b_sz, t_len, _ = u.shape
    tb = TOKENS_MIX
    assert t_len % tb == 0 and tb % GLA_CHUNK == 0 and tb % CONV_ROWS == 0 and tb % NORM_ROWS == 0

    def tok(width):
        return pl.BlockSpec((1, tb, width), lambda b, t: (b, t, 0))

    def const(arr):
        return pl.BlockSpec(arr.shape, lambda b, t: (0,) * arr.ndim)

    return pl.pallas_call(
        _mixer_kernel,
        grid=(b_sz, t_len // tb),
        in_specs=[tok(D_CONV), tok(D_GLA_K), tok(D_GLA_K), tok(D_GLA_K), tok(D_GLA_V),
                  tok(D_GLA_V), const(cw), const(cb), const(lng), const(lnb), const(gng)],
        out_specs=tok(D_CONV + D_GLA_V),
        out_shape=jax.ShapeDtypeStruct((b_sz, t_len, D_CONV + D_GLA_V), BF16),
        scratch_shapes=[
            pltpu.VMEM((SUBLANES, D_CONV // LANES, CONV_HALO + tb, LANES), F32),
            pltpu.VMEM((tb, D_CONV), F32),
            pltpu.VMEM((D_GLA_V, D_GLA_K), F32),
            pltpu.VMEM((tb, D_GLA_K), F32),
        ],
        compiler_params=pltpu.CompilerParams(
            dimension_semantics=("arbitrary", "arbitrary"), vmem_limit_bytes=VMEM_LIMIT_BYTES),
        name="mixer",
    )(u, q, k, la, v, gate, cw, cb, lng, lnb, gng)


def _out_ffn_kernel(x_ref, y_ref, gt1_ref, sc2_ref, sh2_ref, gt2_ref, scf_ref, shf_ref,
                    gffn_ref, gfin_ref, wout_hbm, wgu_hbm, wo_hbm, o_ref,
                    wout_ref, wgu_ref, wo_ref, stage_wide_ref, stage_ref, sem_wide, sem):
    @pl.when((pl.program_id(0) == 0) & (pl.program_id(1) == 0))
    def _():
        _stage_bf16(wout_hbm, wout_ref, stage_ref, sem)
        _stage_bf16(wgu_hbm, wgu_ref, stage_wide_ref, sem_wide)
        _stage_bf16(wo_hbm, wo_ref, stage_ref, sem)

    tm = x_ref.shape[1]
    groups = [slice(r, r + FFN_ROWS) for r in range(0, tm, FFN_ROWS)]

    def out_proj(rows):
        return x_ref[0, rows, :] + gt1_ref[0] * _dot(y_ref[0, rows, :], wout_ref[...])

    def ffn_input(x1):
        ms = jnp.mean(x1 * x1, axis=-1, keepdims=True)
        h = x1 * lax.rsqrt(ms + EPS) * gffn_ref[...]
        return (h * (1.0 + sc2_ref[0]) + sh2_ref[0]).astype(BF16)

    def ffn(x1, hb):
        acc = jnp.zeros_like(x1)
        for lo, hi in FFN_SLABS:
            gate = _dot(hb, wgu_ref[:, lo:hi])
            up = _dot(hb, wgu_ref[:, D_FF + lo:D_FF + hi])
            act = (gate * _sigmoid(gate) * up).astype(BF16)
            acc = acc + _dot(act, wo_ref[lo:hi, :])
        return x1 + gt2_ref[0] * acc

    def final_norm(rows, x2):
        ms = jnp.mean(x2 * x2, axis=-1, keepdims=True)
        xn = x2 * lax.rsqrt(ms + EPS) * gfin_ref[...]
        o_ref[0, rows, :] = xn * (1.0 + scf_ref[0]) + shf_ref[0]

    x1s = [out_proj(rows) for rows in groups]
    pending = None
    for rows, x1 in zip(groups, x1s):
        hb = ffn_input(x1)
        if pending is not None:
            final_norm(*pending)
        pending = (rows, ffn(x1, hb))
    final_norm(*pending)


def _out_ffn(x, y, mod, gffn, gfin, wout, wgu, wo):
    b_sz, t_len, d = x.shape
    tm = TOKENS_FFN
    assert t_len % tm == 0 and tm % FFN_ROWS == 0

    def tok(width):
        return pl.BlockSpec((1, tm, width), lambda b, t: (b, t, 0))

    def const(arr):
        return pl.BlockSpec(arr.shape, lambda b, t: (0,) * arr.ndim,
                            pipeline_mode=pl.Buffered(1))

    mods = (MOD_GT1, MOD_SC2, MOD_SH2, MOD_GT2, MOD_SCF, MOD_SHF)
    in_hbm = pl.BlockSpec(memory_space=pl.ANY)
    return pl.pallas_call(
        _out_ffn_kernel,
        grid=(b_sz, t_len // tm),
        in_specs=[tok(d), tok(d)] + [_mod_spec(m) for m in mods]
                 + [const(gffn), const(gfin), in_hbm, in_hbm, in_hbm],
        out_specs=tok(d),
        out_shape=jax.ShapeDtypeStruct((b_sz, t_len, d), F32),
        scratch_shapes=[
            pltpu.VMEM(wout.shape, BF16),
            pltpu.VMEM(wgu.shape, BF16),
            pltpu.VMEM(wo.shape, BF16),
            pltpu.VMEM((2, STAGE_ROWS_WIDE, wgu.shape[1]), F32),
            pltpu.VMEM((2, STAGE_ROWS, d), F32),
            pltpu.SemaphoreType.DMA((2,)),
            pltpu.SemaphoreType.DMA((2,)),
        ],
        compiler_params=pltpu.CompilerParams(
            dimension_semantics=("arbitrary", "arbitrary"), vmem_limit_bytes=VMEM_LIMIT_BYTES),
        name="out_ffn",
    )(x, y, *([mod] * len(mods)), gffn, gfin, wout, wgu, wo)


def kernel(x, c, w_ada, b_ada, g_mix, w_in, conv_w, conv_b, ln_g, ln_b, w_a2, b_a2,
           gla_norm_g, w_out, g_ffn, w_ffn_in, w_ffn_out, w_ada_final, b_ada_final, g_final):
    b_sz, t_len, d = x.shape
    assert w_ada.shape[0] == 1, "single-layer block"
    assert d == D_MODEL

    c_lanes = jnp.broadcast_to(c[:, :, None], (b_sz, d, LANES))
    mod = _modulation(c_lanes, w_ada[0], b_ada[0], w_ada_final, b_ada_final)

    w = jnp.concatenate(
        [w_in[0].astype(BF16), jnp.zeros((d, LANES - GATE_RANK), BF16)], axis=1)
    wa2 = jnp.pad(w_a2[0], ((0, LANES - GATE_RANK), (0, 0))).astype(BF16)

    u, q, k, v, gate, la = _in_proj(
        x, mod, g_mix[0].reshape(1, d), w, wa2, b_a2[0].reshape(1, D_GLA_K))

    cw = jnp.broadcast_to(conv_w[0][:, None, :], (CONV_KERNEL, SUBLANES, D_CONV))
    y = _mixer(u, q, k, la, v, gate, cw, conv_b[0].reshape(1, D_CONV),
               ln_g[0].reshape(1, D_CONV), ln_b[0].reshape(1, D_CONV),
               jnp.tile(gla_norm_g[0], GLA_HEADS).reshape(1, D_GLA_V))

    return _out_ffn(
        x, y, mod, g_ffn[0].reshape(1, d), g_final.reshape(1, d),
        w_out[0], w_ffn_in[0], w_ffn_out[0])
```

```python
import functools

import jax
import jax.numpy as jnp
from jax import lax
from jax.experimental import pallas as pl
from jax.experimental.pallas import tpu as pltpu

F32 = jnp.float32
BF16 = jnp.bfloat16

D_MODEL = 1024
D_CONV = 512
CONV_KERNEL = 31
D_GLA_V = 512
D_GLA_K = 256
GLA_HEADS = 4
HEAD_K = D_GLA_K // GLA_HEADS
HEAD_V = D_GLA_V // GLA_HEADS
GATE_RANK = 16
GATE_TAU = 16.0
D_FF = 2816
EPS = 1e-6

LANES = 128
SUBLANES = 8
MXU_DIM = 256
VMEM_LIMIT_BYTES = 56 * 1024 * 1024
MOD_ROWS = 128
TOKENS_IN = 1024
TOKENS_MIX = 512
TOKENS_FFN = 512
STAGE_ROWS_WIDE = 128
STAGE_ROWS = 512
CONV_HALO = 32
CONV_ROWS = 64
CONV_PARTIALS = 2
NORM_ROWS = 128
GLA_CHUNK = 128
GLA_MAX_FACTORISED_DECAY = 40.0
FFN_ROWS = 256
FFN_SLABS = ((0, 6 * MXU_DIM), (6 * MXU_DIM, D_FF))

IN_COLS = {}
_col = 0
for _name, _width in (("cv", D_CONV), ("cg", D_CONV), ("q", D_GLA_K), ("k", D_GLA_K),
                      ("v", D_GLA_V), ("og", D_GLA_V), ("a", LANES)):
    IN_COLS[_name] = (_col, _col + _width)
    _col += _width

MOD_SH1, MOD_SC1, MOD_GT1, MOD_SH2, MOD_SC2, MOD_GT2, MOD_SHF, MOD_SCF = range(8)


def _sigmoid(x):
    return 1.0 / (1.0 + jnp.exp(-x))


def _dot(a, b):
    return jnp.dot(a, b, preferred_element_type=F32)


def _dot_nt(a, b):
    return lax.dot_general(a, b, (((1,), (1,)), ((), ())), preferred_element_type=F32)


def _dot_tn(a, b):
    return lax.dot_general(a, b, (((0,), (0,)), ((), ())), preferred_element_type=F32)


def _split3(a):
    hi = a.astype(BF16)
    r1 = a - hi.astype(F32)
    mid = r1.astype(BF16)
    lo = (r1 - mid.astype(F32)).astype(BF16)
    return hi, mid, lo


def _stage_bf16(src_hbm, dst_ref, stage_ref, sem):
    n_rows, n_cols = src_hbm.shape
    chunk = stage_ref.shape[1]
    starts = list(range(0, n_rows, chunk))

    def copy(i):
        r, n = starts[i], min(chunk, n_rows - starts[i])
        return pltpu.make_async_copy(
            src_hbm.at[pl.ds(r, n), :], stage_ref.at[i % 2, pl.ds(0, n), :], sem.at[i % 2])

    copy(0).start()
    for i, r in enumerate(starts):
        if i + 1 < len(starts):
            copy(i + 1).start()
        copy(i).wait()
        n = min(chunk, n_rows - r)
        dst_ref[r:r + n, 0:n_cols] = stage_ref[i % 2, 0:n, :].astype(dst_ref.dtype)


def _modulation_kernel(c_ref, w_ref, b_ref, wf_ref, bf_ref, o_ref, acc_ref):
    j = pl.program_id(0)
    n_batch, k_rows, _ = c_ref.shape
    n = w_ref.shape[1]

    @pl.when(j == 0)
    def _():
        acc_ref[...] = jnp.zeros_like(acc_ref)

    c_act = []
    for b in range(n_batch):
        cb = c_ref[b]
        c_act.append(cb * _sigmoid(cb))
    for ref, offset in ((w_ref, 0), (wf_ref, n)):
        for t in range(ref.shape[1] // LANES):
            w = ref[:, t * LANES:(t + 1) * LANES]
            cols = slice(offset + t * LANES, offset + (t + 1) * LANES)
            for b in range(n_batch):
                prod = (w * c_act[b]).reshape(k_rows // SUBLANES, SUBLANES, LANES)
                acc_ref[b, :, cols] += jnp.sum(prod, axis=0)

    @pl.when(j == pl.num_programs(0) - 1)
    def _():
        bias = jnp.concatenate([b_ref[...], bf_ref[...]], axis=1)
        for b in range(n_batch):
            o_ref[b] = jnp.sum(acc_ref[b], axis=0, keepdims=True) + bias


def _modulation(c_lanes, w, b, wf, bf):
    n_batch, d, _ = c_lanes.shape
    n, nf = w.shape[1], wf.shape[1]
    assert d % MOD_ROWS == 0 and n % LANES == 0 and nf % LANES == 0
    return pl.pallas_call(
        _modulation_kernel,
        grid=(d // MOD_ROWS,),
        in_specs=[
            pl.BlockSpec((n_batch, MOD_ROWS, LANES), lambda j: (0, j, 0)),
            pl.BlockSpec((MOD_ROWS, n), lambda j: (j, 0)),
            pl.BlockSpec((1, n), lambda j: (0, 0)),
            pl.BlockSpec((MOD_ROWS, nf), lambda j: (j, 0)),
            pl.BlockSpec((1, nf), lambda j: (0, 0)),
        ],
        out_specs=pl.BlockSpec((n_batch, 1, n + nf), lambda j: (0, 0, 0)),
        out_shape=jax.ShapeDtypeStruct((n_batch, 1, n + nf), F32),
        scratch_shapes=[pltpu.VMEM((n_batch, SUBLANES, n + nf), F32)],
        compiler_params=pltpu.CompilerParams(
            dimension_semantics=("arbitrary",), vmem_limit_bytes=VMEM_LIMIT_BYTES),
        name="modulation",
    )(c_lanes, w, b.reshape(1, n), wf, bf.reshape(1, nf))


def _mod_spec(which):
    return pl.BlockSpec((1, 1, D_MODEL), lambda b, t: (b, 0, which))


def _in_proj_kernel(x_ref, sc_ref, sh_ref, g_ref, w_ref, wa2_ref, ba2_ref,
                    u_ref, q_ref, k_ref, v_ref, gate_ref, la_ref):
    x = x_ref[0]
    ms = jnp.mean(x * x, axis=-1, keepdims=True)
    h = x * lax.rsqrt(ms + EPS) * g_ref[...]
    h = h * (1.0 + sc_ref[0]) + sh_ref[0]
    hb = h.astype(BF16)

    proj = _dot(hb, w_ref[...])

    def piece(name):
        lo, hi = IN_COLS[name]
        return proj[:, lo:hi]

    u_ref[0] = piece("cv") * _sigmoid(piece("cg"))
    q_ref[0] = piece("q") * (HEAD_K ** -0.5)
    k_ref[0] = piece("k")
    v_ref[0] = piece("v")
    og = piece("og")
    gate_ref[0] = og * _sigmoid(og)

    a_low = piece("a")
    z = _dot(a_low.astype(BF16), wa2_ref[...]) + ba2_ref[...]
    log_sig = jnp.minimum(z, 0.0) - jnp.log(1.0 + jnp.exp(-jnp.abs(z)))
    la_ref[0] = log_sig * (1.0 / GATE_TAU)


def _in_proj(x, mod, g, w, wa2, ba2):
    b_sz, t_len, d = x.shape
    tm = TOKENS_IN
    assert t_len % tm == 0

    def tok(width):
        return pl.BlockSpec((1, tm, width), lambda b, t: (b, t, 0))

    def const(arr):
        return pl.BlockSpec(arr.shape, lambda b, t: (0,) * arr.ndim)

    def out(width):
        return jax.ShapeDtypeStruct((b_sz, t_len, width), F32)

    return pl.pallas_call(
        _in_proj_kernel,
        grid=(b_sz, t_len // tm),
        in_specs=[tok(d), _mod_spec(MOD_SC1), _mod_spec(MOD_SH1), const(g), const(w),
                  const(wa2), const(ba2)],
        out_specs=[tok(D_CONV), tok(D_GLA_K), tok(D_GLA_K), tok(D_GLA_V), tok(D_GLA_V),
                   tok(D_GLA_K)],
        out_shape=[out(D_CONV), out(D_GLA_K), out(D_GLA_K), out(D_GLA_V), out(D_GLA_V),
                   out(D_GLA_K)],
        compiler_params=pltpu.CompilerParams(
            dimension_semantics=("arbitrary", "arbitrary"), vmem_limit_bytes=VMEM_LIMIT_BYTES),
        name="in_proj",
    )(x, mod, mod, g, w, wa2, ba2)


def _mixer_kernel(u_ref, q_ref, k_ref, la_ref, v_ref, gate_ref, cw_ref, cb_ref,
                  lng_ref, lnb_ref, gng_ref, y_ref, win_ref, pre_ref, st_ref, cum_ref):
    nb, tb = u_ref.shape[0], u_ref.shape[1]
    lc = GLA_CHUNK

    @pl.when(pl.program_id(0) == 0)
    def _():
        win_ref[...] = jnp.zeros_like(win_ref)
        st_ref[...] = jnp.zeros_like(st_ref)

    for b in range(nb):
        for c in range(D_CONV // LANES):
            u_lanes = u_ref[b, :, c * LANES:(c + 1) * LANES]
            for phase in range(SUBLANES):
                win_ref[b, phase, c, CONV_HALO - phase:CONV_HALO - phase + tb, :] = u_lanes
    first_tap = CONV_HALO - (CONV_KERNEL - 1)

    for c in range(D_CONV // LANES):
        lanes = slice(c * LANES, (c + 1) * LANES)
        taps = []
        for phase in range(SUBLANES):
            for base in range(0, CONV_HALO + SUBLANES, SUBLANES):
                tap = base + phase - first_tap
                if 0 <= tap < CONV_KERNEL:
                    taps.append((cw_ref[tap, :, lanes], phase, base))
        bias = cb_ref[:, lanes]

        def conv_tile(i, carry, c=c, lanes=lanes, taps=taps, bias=bias):
            r0 = pl.multiple_of(i * CONV_ROWS, CONV_ROWS)
            for b in range(nb):
                parts = [jnp.broadcast_to(bias, (CONV_ROWS, LANES))] + [None] * (CONV_PARTIALS - 1)
                for n, (w8, phase, base) in enumerate(taps):
                    w_rows = jnp.concatenate([w8] * (CONV_ROWS // SUBLANES), axis=0)
                    term = w_rows * win_ref[b, phase, c, pl.ds(r0 + base, CONV_ROWS), :]
                    p = n % CONV_PARTIALS
                    parts[p] = term if parts[p] is None else parts[p] + term
                pre_ref[b, pl.ds(r0, CONV_ROWS), lanes] = functools.reduce(lambda x, y: x + y, parts)
            return carry

        lax.fori_loop(0, tb // CONV_ROWS, conv_tile, 0)

    for b in range(nb):
        for r0 in range(0, tb, NORM_ROWS):
            acc = pre_ref[b, r0:r0 + NORM_ROWS, :]
            mu = jnp.mean(acc, axis=-1, keepdims=True)
            cen = acc - mu
            var = jnp.mean(cen * cen, axis=-1, keepdims=True)
            yn = cen * lax.rsqrt(var + EPS) * lng_ref[...] + lnb_ref[...]
            y_ref[b, r0:r0 + NORM_ROWS, 0:D_CONV] = (yn * _sigmoid(yn)).astype(y_ref.dtype)
    win_ref[:, :, :, 0:CONV_HALO, :] = win_ref[:, :, :, tb:tb + CONV_HALO, :]

    n_chunks = tb // lc
    row = lax.broadcasted_iota(jnp.int32, (lc, lc), 0)
    col = lax.broadcasted_iota(jnp.int32, (lc, lc), 1)
    tril = (col <= row).astype(BF16)
    klane = lax.broadcasted_iota(jnp.int32, (1, D_GLA_K), 1) // HEAD_K
    srow = lax.broadcasted_iota(jnp.int32, (GLA_HEADS * lc, lc), 0)
    scol = lax.broadcasted_iota(jnp.int32, (GLA_HEADS * lc, lc), 1)
    causal = scol <= (srow % lc)
    rows1 = lax.broadcasted_iota(jnp.int32, (lc, 1), 0)

    decay = None
    for b in range(nb):
        for ci in range(n_chunks):
            hi, mid, lo = _split3(la_ref[b, ci * lc:(ci + 1) * lc, :])
            cum = _dot(tril, hi) + _dot(tril, mid) + _dot(tril, lo)
            cum_ref[b, ci * lc:(ci + 1) * lc, :] = cum
            total = -cum[lc - 1:lc, :]
            decay = total if decay is None else jnp.maximum(decay, total)
    max_decay = jnp.max(decay)

    def chunk_factorised(b, ci, q, k, cum, cum_last):
        ref = cum[lc // 2 - 1:lc // 2, :]
        q_dec = q * jnp.exp(cum)
        k_inc = k * jnp.exp(-cum)
        qt = (q_dec * jnp.exp(-ref)).astype(BF16)
        kt = (k_inc * jnp.exp(ref)).astype(BF16)
        qbd = jnp.concatenate(
            [jnp.where(klane == h, qt, jnp.zeros_like(qt)) for h in range(GLA_HEADS)], axis=0)
        s = _dot_nt(qbd, kt)
        s = jnp.where(causal, s, 0.0).astype(BF16)
        heads = []
        for h in range(GLA_HEADS):
            vh = v_ref[b, ci * lc:(ci + 1) * lc, h * HEAD_V:(h + 1) * HEAD_V].astype(BF16)
            heads.append(_dot(s[h * lc:(h + 1) * lc, :], vh))
        o_intra = jnp.concatenate(heads, axis=-1)
        return o_intra, q_dec.astype(BF16), (k_inc * jnp.exp(cum_last)).astype(BF16)

    def chunk_exact(b, ci, q, k, cum, cum_last):
        e_row = lax.broadcasted_iota(jnp.int32, (D_GLA_K, D_GLA_V), 0) // HEAD_K
        e_col = lax.broadcasted_iota(jnp.int32, (D_GLA_K, D_GLA_V), 1) // HEAD_V
        expand = (e_row == e_col).astype(BF16)

        def key_row(j, o):
            kj = k_ref[b, pl.ds(ci * lc + j, 1), :]
            cj = cum_ref[b, pl.ds(ci * lc + j, 1), :]
            vj = v_ref[b, pl.ds(ci * lc + j, 1), :]
            p = q * kj * jnp.exp(jnp.minimum(cum - cj, 0.0))
            p = jnp.where(rows1 >= j, p, 0.0).astype(BF16)
            return o + _dot(p, expand) * vj

        o_intra = lax.fori_loop(0, lc, key_row, jnp.zeros((lc, D_GLA_V), F32))
        return (o_intra, (q * jnp.exp(cum)).astype(BF16),
                (k * jnp.exp(cum_last - cum)).astype(BF16))

    def gla_tile(chunk_terms):
        for ci in range(n_chunks):
            for b in range(nb):
                rows = slice(ci * lc, (ci + 1) * lc)
                q = q_ref[b, rows, :]
                k = k_ref[b, rows, :]
                cum = cum_ref[b, rows, :]
                cum_last = cum[lc - 1:lc, :]
                o_intra, qh, kh = chunk_terms(b, ci, q, k, cum, cum_last)

                st = st_ref[b]
                o_inter = _dot_nt(qh, st.astype(BF16))
                vb = v_ref[b, rows, :].astype(BF16)
                update = jnp.concatenate(
                    [_dot_tn(vb[:, h * HEAD_V:(h + 1) * HEAD_V],
                             jnp.where(klane == h, kh, jnp.zeros_like(kh)))
                     for h in range(GLA_HEADS)], axis=0)
                st_ref[b] = st * jnp.exp(cum_last) + update

                o = o_inter + o_intra
                normed = []
                for h in range(GLA_HEADS):
                    oh = o[:, h * HEAD_V:(h + 1) * HEAD_V]
                    ms = jnp.mean(oh * oh, axis=-1, keepdims=True)
                    normed.append(oh * lax.rsqrt(ms + EPS))
                on = jnp.concatenate(normed, axis=-1) * gng_ref[...]
                y_gla = on * gate_ref[b, rows, :]
                y_ref[b, rows, D_CONV:D_CONV + D_GLA_V] = y_gla.astype(y_ref.dtype)

    @pl.when(max_decay <= GLA_MAX_FACTORISED_DECAY)
    def _():
        gla_tile(chunk_factorised)

    @pl.when(max_decay > GLA_MAX_FACTORISED_DECAY)
    def _():
        gla_tile(chunk_exact)


def _mixer(u, q, k, la, v, gate, cw, cb, lng, lnb, gng):
    b_sz, t_len, _ = u.shape
    tb = TOKENS_MIX
    assert t_len % tb == 0 and tb % GLA_CHUNK == 0 and tb % CONV_ROWS == 0 and tb % NORM_ROWS == 0

    def tok(width):
        return pl.BlockSpec((b_sz, tb, width), lambda t: (0, t, 0))

    def const(arr):
        return pl.BlockSpec(arr.shape, lambda t: (0,) * arr.ndim)

    return pl.pallas_call(
        _mixer_kernel,
        grid=(t_len // tb,),
        in_specs=[tok(D_CONV), tok(D_GLA_K), tok(D_GLA_K), tok(D_GLA_K), tok(D_GLA_V),
                  tok(D_GLA_V), const(cw), const(cb), const(lng), const(lnb), const(gng)],
        out_specs=tok(D_CONV + D_GLA_V),
        out_shape=jax.ShapeDtypeStruct((b_sz, t_len, D_CONV + D_GLA_V), BF16),
        scratch_shapes=[
            pltpu.VMEM((b_sz, SUBLANES, D_CONV // LANES, CONV_HALO + tb, LANES), F32),
            pltpu.VMEM((b_sz, tb, D_CONV), F32),
            pltpu.VMEM((b_sz, D_GLA_V, D_GLA_K), F32),
            pltpu.VMEM((b_sz, tb, D_GLA_K), F32),
        ],
        compiler_params=pltpu.CompilerParams(
            dimension_semantics=("arbitrary",), vmem_limit_bytes=VMEM_LIMIT_BYTES),
        name="mixer",
    )(u, q, k, la, v, gate, cw, cb, lng, lnb, gng)


def _out_ffn_kernel(x_ref, y_ref, gt1_ref, sc2_ref, sh2_ref, gt2_ref, scf_ref, shf_ref,
                    gffn_ref, gfin_ref, wout_hbm, wgu_hbm, wo_hbm, o_ref,
                    wout_ref, wgu_ref, wo_ref, stage_wide_ref, stage_ref, sem_wide, sem):
    @pl.when((pl.program_id(0) == 0) & (pl.program_id(1) == 0))
    def _():
        _stage_bf16(wout_hbm, wout_ref, stage_ref, sem)
        _stage_bf16(wgu_hbm, wgu_ref, stage_wide_ref, sem_wide)
        _stage_bf16(wo_hbm, wo_ref, stage_ref, sem)

    tm = x_ref.shape[1]
    groups = [slice(r, r + FFN_ROWS) for r in range(0, tm, FFN_ROWS)]

    def out_proj(rows):
        return x_ref[0, rows, :] + gt1_ref[0] * _dot(y_ref[0, rows, :], wout_ref[...])

    def ffn_input(x1):
        ms = jnp.mean(x1 * x1, axis=-1, keepdims=True)
        h = x1 * lax.rsqrt(ms + EPS) * gffn_ref[...]
        return (h * (1.0 + sc2_ref[0]) + sh2_ref[0]).astype(BF16)

    def ffn(x1, hb):
        acc = jnp.zeros_like(x1)
        for lo, hi in FFN_SLABS:
            gate = _dot(hb, wgu_ref[:, lo:hi])
            up = _dot(hb, wgu_ref[:, D_FF + lo:D_FF + hi])
            act = (gate * _sigmoid(gate) * up).astype(BF16)
            acc = acc + _dot(act, wo_ref[lo:hi, :])
        return x1 + gt2_ref[0] * acc

    def final_norm(rows, x2):
        ms = jnp.mean(x2 * x2, axis=-1, keepdims=True)
        xn = x2 * lax.rsqrt(ms + EPS) * gfin_ref[...]
        o_ref[0, rows, :] = xn * (1.0 + scf_ref[0]) + shf_ref[0]

    x1s = [out_proj(rows) for rows in groups]
    pending = None
    for rows, x1 in zip(groups, x1s):
        hb = ffn_input(x1)
        if pending is not None:
            final_norm(*pending)
        pending = (rows, ffn(x1, hb))
    final_norm(*pending)


def _out_ffn(x, y, mod, gffn, gfin, wout, wgu, wo):
    b_sz, t_len, d = x.shape
    tm = TOKENS_FFN
    assert t_len % tm == 0 and tm % FFN_ROWS == 0

    def tok(width):
        return pl.BlockSpec((1, tm, width), lambda b, t: (b, t, 0))

    def const(arr):
        return pl.BlockSpec(arr.shape, lambda b, t: (0,) * arr.ndim,
                            pipeline_mode=pl.Buffered(1))

    mods = (MOD_GT1, MOD_SC2, MOD_SH2, MOD_GT2, MOD_SCF, MOD_SHF)
    in_hbm = pl.BlockSpec(memory_space=pl.ANY)
    return pl.pallas_call(
        _out_ffn_kernel,
        grid=(b_sz, t_len // tm),
        in_specs=[tok(d), tok(d)] + [_mod_spec(m) for m in mods]
                 + [const(gffn), const(gfin), in_hbm, in_hbm, in_hbm],
        out_specs=tok(d),
        out_shape=jax.ShapeDtypeStruct((b_sz, t_len, d), F32),
        scratch_shapes=[
            pltpu.VMEM(wout.shape, BF16),
            pltpu.VMEM(wgu.shape, BF16),
            pltpu.VMEM(wo.shape, BF16),
            pltpu.VMEM((2, STAGE_ROWS_WIDE, wgu.shape[1]), F32),
            pltpu.VMEM((2, STAGE_ROWS, d), F32),
            pltpu.SemaphoreType.DMA((2,)),
            pltpu.SemaphoreType.DMA((2,)),
        ],
        compiler_params=pltpu.CompilerParams(
            dimension_semantics=("arbitrary", "arbitrary"), vmem_limit_bytes=VMEM_LIMIT_BYTES),
        name="out_ffn",
    )(x, y, *([mod] * len(mods)), gffn, gfin, wout, wgu, wo)


def kernel(x, c, w_ada, b_ada, g_mix, w_in, conv_w, conv_b, ln_g, ln_b, w_a2, b_a2,
           gla_norm_g, w_out, g_ffn, w_ffn_in, w_ffn_out, w_ada_final, b_ada_final, g_final):
    b_sz, t_len, d = x.shape
    assert w_ada.shape[0] == 1, "single-layer block"
    assert d == D_MODEL

    c_lanes = jnp.broadcast_to(c[:, :, None], (b_sz, d, LANES))
    mod = _modulation(c_lanes, w_ada[0], b_ada[0], w_ada_final, b_ada_final)

    w = jnp.concatenate(
        [w_in[0].astype(BF16), jnp.zeros((d, LANES - GATE_RANK), BF16)], axis=1)
    wa2 = jnp.pad(w_a2[0], ((0, LANES - GATE_RANK), (0, 0))).astype(BF16)

    u, q, k, v, gate, la = _in_proj(
        x, mod, g_mix[0].reshape(1, d), w, wa2, b_a2[0].reshape(1, D_GLA_K))

    cw = jnp.broadcast_to(conv_w[0][:, None, :], (CONV_KERNEL, SUBLANES, D_CONV))
    y = _mixer(u, q, k, la, v, gate, cw, conv_b[0].reshape(1, D_CONV),
               ln_g[0].reshape(1, D_CONV), ln_b[0].reshape(1, D_CONV),
               jnp.tile(gla_norm_g[0], GLA_HEADS).reshape(1, D_GLA_V))

    return _out_ffn(
        x, y, mod, g_ffn[0].reshape(1, d), g_final.reshape(1, d),
        w_out[0], w_ffn_in[0], w_ffn_out[0])
```

```python
import functools

import jax
import jax.numpy as jnp
from jax import lax
from jax.experimental import pallas as pl
from jax.experimental.pallas import tpu as pltpu

F32 = jnp.float32
BF16 = jnp.bfloat16

D_MODEL = 1024
D_CONV = 512
CONV_KERNEL = 31
D_GLA_V = 512
D_GLA_K = 256
GLA_HEADS = 4
HEAD_K = D_GLA_K // GLA_HEADS
HEAD_V = D_GLA_V // GLA_HEADS
GATE_RANK = 16
GATE_TAU = 16.0
D_FF = 2816
EPS = 1e-6

LANES = 128
SUBLANES = 8
MXU_DIM = 256
VMEM_LIMIT_BYTES = 56 * 1024 * 1024
MOD_ROWS = 128
TOKENS_IN = 1024
TOKENS_MIX = 512
TOKENS_FFN = 512
STAGE_ROWS_WIDE = 128
STAGE_ROWS = 512
CONV_HALO = 32
CONV_ROWS = 64
CONV_PARTIALS = 2
NORM_ROWS = 128
GLA_CHUNK = 128
GLA_MAX_FACTORISED_DECAY = 40.0
FFN_ROWS = 256
FFN_SLABS = ((0, 6 * MXU_DIM), (6 * MXU_DIM, D_FF))

IN_COLS = {}
_col = 0
for _name, _width in (("cv", D_CONV), ("cg", D_CONV), ("q", D_GLA_K), ("k", D_GLA_K),
                      ("v", D_GLA_V), ("og", D_GLA_V), ("a", LANES)):
    IN_COLS[_name] = (_col, _col + _width)
    _col += _width

MOD_SH1, MOD_SC1, MOD_GT1, MOD_SH2, MOD_SC2, MOD_GT2, MOD_SHF, MOD_SCF = range(8)


def _sigmoid(x):
    return 0.5 * jnp.tanh(0.5 * x) + 0.5


def _silu(x):
    h = 0.5 * x
    return h * jnp.tanh(h) + h


def _dot(a, b):
    return jnp.dot(a, b, preferred_element_type=F32)


def _dot_nt(a, b):
    return lax.dot_general(a, b, (((1,), (1,)), ((), ())), preferred_element_type=F32)


def _dot_tn(a, b):
    return lax.dot_general(a, b, (((0,), (0,)), ((), ())), preferred_element_type=F32)


def _split3(a):
    hi = a.astype(BF16)
    r1 = a - hi.astype(F32)
    mid = r1.astype(BF16)
    lo = (r1 - mid.astype(F32)).astype(BF16)
    return hi, mid, lo


def _stage_bf16(src_hbm, dst_ref, stage_ref, sem):
    n_rows, n_cols = src_hbm.shape
    chunk = stage_ref.shape[1]
    starts = list(range(0, n_rows, chunk))

    def copy(i):
        r, n = starts[i], min(chunk, n_rows - starts[i])
        return pltpu.make_async_copy(
            src_hbm.at[pl.ds(r, n), :], stage_ref.at[i % 2, pl.ds(0, n), :], sem.at[i % 2])

    copy(0).start()
    for i, r in enumerate(starts):
        if i + 1 < len(starts):
            copy(i + 1).start()
        copy(i).wait()
        n = min(chunk, n_rows - r)
        dst_ref[r:r + n, 0:n_cols] = stage_ref[i % 2, 0:n, :].astype(dst_ref.dtype)


def _modulation_kernel(c_ref, w_ref, b_ref, wf_ref, bf_ref, o_ref, acc_ref):
    j = pl.program_id(0)
    n_batch, k_rows, _ = c_ref.shape
    n = w_ref.shape[1]

    @pl.when(j == 0)
    def _():
        acc_ref[...] = jnp.zeros_like(acc_ref)

    c_act = []
    for b in range(n_batch):
        cb = c_ref[b]
        c_act.append(_silu(cb))
    for ref, offset in ((w_ref, 0), (wf_ref, n)):
        for t in range(ref.shape[1] // LANES):
            w = ref[:, t * LANES:(t + 1) * LANES]
            cols = slice(offset + t * LANES, offset + (t + 1) * LANES)
            for b in range(n_batch):
                prod = (w * c_act[b]).reshape(k_rows // SUBLANES, SUBLANES, LANES)
                acc_ref[b, :, cols] += jnp.sum(prod, axis=0)

    @pl.when(j == pl.num_programs(0) - 1)
    def _():
        bias = jnp.concatenate([b_ref[...], bf_ref[...]], axis=1)
        for b in range(n_batch):
            o_ref[b] = jnp.sum(acc_ref[b], axis=0, keepdims=True) + bias


def _modulation(c_lanes, w, b, wf, bf):
    n_batch, d, _ = c_lanes.shape
    n, nf = w.shape[1], wf.shape[1]
    assert d % MOD_ROWS == 0 and n % LANES == 0 and nf % LANES == 0
    return pl.pallas_call(
        _modulation_kernel,
        grid=(d // MOD_ROWS,),
        in_specs=[
            pl.BlockSpec((n_batch, MOD_ROWS, LANES), lambda j: (0, j, 0)),
            pl.BlockSpec((MOD_ROWS, n), lambda j: (j, 0)),
            pl.BlockSpec((1, n), lambda j: (0, 0)),
            pl.BlockSpec((MOD_ROWS, nf), lambda j: (j, 0)),
            pl.BlockSpec((1, nf), lambda j: (0, 0)),
        ],
        out_specs=pl.BlockSpec((n_batch, 1, n + nf), lambda j: (0, 0, 0)),
        out_shape=jax.ShapeDtypeStruct((n_batch, 1, n + nf), F32),
        scratch_shapes=[pltpu.VMEM((n_batch, SUBLANES, n + nf), F32)],
        compiler_params=pltpu.CompilerParams(
            dimension_semantics=("arbitrary",), vmem_limit_bytes=VMEM_LIMIT_BYTES),
        name="modulation",
    )(c_lanes, w, b.reshape(1, n), wf, bf.reshape(1, nf))


def _mod_spec(which):
    return pl.BlockSpec((1, 1, D_MODEL), lambda b, t: (b, 0, which))


def _in_proj_kernel(x_ref, sc_ref, sh_ref, g_ref, w_ref, wa2_ref, ba2_ref,
                    u_ref, q_ref, k_ref, v_ref, gate_ref, la_ref):
    x = x_ref[0]
    ms = jnp.mean(x * x, axis=-1, keepdims=True)
    h = x * lax.rsqrt(ms + EPS) * g_ref[...]
    h = h * (1.0 + sc_ref[0]) + sh_ref[0]
    hb = h.astype(BF16)

    proj = _dot(hb, w_ref[...])

    def piece(name):
        lo, hi = IN_COLS[name]
        return proj[:, lo:hi]

    u_ref[0] = piece("cv") * _sigmoid(piece("cg"))
    q_ref[0] = piece("q") * (HEAD_K ** -0.5)
    k_ref[0] = piece("k")
    v_ref[0] = piece("v")
    og = piece("og")
    gate_ref[0] = _silu(og)

    a_low = piece("a")
    z = _dot(a_low.astype(BF16), wa2_ref[...]) + ba2_ref[...]
    log_sig = jnp.minimum(z, 0.0) - jnp.log(1.0 + jnp.exp(-jnp.abs(z)))
    la_ref[0] = log_sig * (1.0 / GATE_TAU)


def _in_proj(x, mod, g, w, wa2, ba2):
    b_sz, t_len, d = x.shape
    tm = TOKENS_IN
    assert t_len % tm == 0

    def tok(width):
        return pl.BlockSpec((1, tm, width), lambda b, t: (b, t, 0))

    def const(arr):
        return pl.BlockSpec(arr.shape, lambda b, t: (0,) * arr.ndim)

    def out(width):
        return jax.ShapeDtypeStruct((b_sz, t_len, width), F32)

    return pl.pallas_call(
        _in_proj_kernel,
        grid=(b_sz, t_len // tm),
        in_specs=[tok(d), _mod_spec(MOD_SC1), _mod_spec(MOD_SH1), const(g), const(w),
                  const(wa2), const(ba2)],
        out_specs=[tok(D_CONV), tok(D_GLA_K), tok(D_GLA_K), tok(D_GLA_V), tok(D_GLA_V),
                   tok(D_GLA_K)],
        out_shape=[out(D_CONV), out(D_GLA_K), out(D_GLA_K), out(D_GLA_V), out(D_GLA_V),
                   out(D_GLA_K)],
        compiler_params=pltpu.CompilerParams(
            dimension_semantics=("arbitrary", "arbitrary"), vmem_limit_bytes=VMEM_LIMIT_BYTES),
        name="in_proj",
    )(x, mod, mod, g, w, wa2, ba2)


def _mixer_kernel(u_ref, q_ref, k_ref, la_ref, v_ref, gate_ref, cw_ref, cb_ref,
                  lng_ref, lnb_ref, gng_ref, y_ref, win_ref, pre_ref, st_ref, cum_ref):
    nb, tb = u_ref.shape[0], u_ref.shape[1]
    lc = GLA_CHUNK

    @pl.when(pl.program_id(0) == 0)
    def _():
        win_ref[...] = jnp.zeros_like(win_ref)
        st_ref[...] = jnp.zeros_like(st_ref)

    for b in range(nb):
        for c in range(D_CONV // LANES):
            u_lanes = u_ref[b, :, c * LANES:(c + 1) * LANES]
            for phase in range(SUBLANES):
                win_ref[b, phase, c, CONV_HALO - phase:CONV_HALO - phase + tb, :] = u_lanes
    first_tap = CONV_HALO - (CONV_KERNEL - 1)

    for c in range(D_CONV // LANES):
        lanes = slice(c * LANES, (c + 1) * LANES)
        taps = []
        for phase in range(SUBLANES):
            for base in range(0, CONV_HALO + SUBLANES, SUBLANES):
                tap = base + phase - first_tap
                if 0 <= tap < CONV_KERNEL:
                    taps.append((cw_ref[tap, :, lanes], phase, base))
        bias = cb_ref[:, lanes]

        def conv_tile(i, carry, c=c, lanes=lanes, taps=taps, bias=bias):
            r0 = pl.multiple_of(i * CONV_ROWS, CONV_ROWS)
            for b in range(nb):
                parts = [jnp.broadcast_to(bias, (CONV_ROWS, LANES))] + [None] * (CONV_PARTIALS - 1)
                for n, (w8, phase, base) in enumerate(taps):
                    w_rows = jnp.concatenate([w8] * (CONV_ROWS // SUBLANES), axis=0)
                    term = w_rows * win_ref[b, phase, c, pl.ds(r0 + base, CONV_ROWS), :]
                    p = n % CONV_PARTIALS
                    parts[p] = term if parts[p] is None else parts[p] + term
                pre_ref[b, pl.ds(r0, CONV_ROWS), lanes] = functools.reduce(lambda x, y: x + y, parts)
            return carry

        lax.fori_loop(0, tb // CONV_ROWS, conv_tile, 0)

    for b in range(nb):
        for r0 in range(0, tb, NORM_ROWS):
            acc = pre_ref[b, r0:r0 + NORM_ROWS, :]
            mu = jnp.mean(acc, axis=-1, keepdims=True)
            cen = acc - mu
            var = jnp.mean(cen * cen, axis=-1, keepdims=True)
            yn = cen * lax.rsqrt(var + EPS) * lng_ref[...] + lnb_ref[...]
            y_ref[b, r0:r0 + NORM_ROWS, 0:D_CONV] = _silu(yn).astype(y_ref.dtype)
    win_ref[:, :, :, 0:CONV_HALO, :] = win_ref[:, :, :, tb:tb + CONV_HALO, :]

    n_chunks = tb // lc
    row = lax.broadcasted_iota(jnp.int32, (lc, lc), 0)
    col = lax.broadcasted_iota(jnp.int32, (lc, lc), 1)
    tril = (col <= row).astype(BF16)
    klane = lax.broadcasted_iota(jnp.int32, (1, D_GLA_K), 1) // HEAD_K
    srow = lax.broadcasted_iota(jnp.int32, (GLA_HEADS * lc, lc), 0)
    scol = lax.broadcasted_iota(jnp.int32, (GLA_HEADS * lc, lc), 1)
    causal = scol <= (srow % lc)
    rows1 = lax.broadcasted_iota(jnp.int32, (lc, 1), 0)

    decay = None
    for b in range(nb):
        for ci in range(n_chunks):
            hi, mid, lo = _split3(la_ref[b, ci * lc:(ci + 1) * lc, :])
            cum = _dot(tril, hi) + _dot(tril, mid) + _dot(tril, lo)
            cum_ref[b, ci * lc:(ci + 1) * lc, :] = cum
            total = -cum[lc - 1:lc, :]
            decay = total if decay is None else jnp.maximum(decay, total)
    max_decay = jnp.max(decay)

    def chunk_factorised(b, ci, q, k, cum, cum_last):
        ref = cum[lc // 2 - 1:lc // 2, :]
        q_dec = q * jnp.exp(cum)
        k_inc = k * jnp.exp(-cum)
        qt = (q_dec * jnp.exp(-ref)).astype(BF16)
        kt = (k_inc * jnp.exp(ref)).astype(BF16)
        qbd = jnp.concatenate(
            [jnp.where(klane == h, qt, jnp.zeros_like(qt)) for h in range(GLA_HEADS)], axis=0)
        s = _dot_nt(qbd, kt)
        s = jnp.where(causal, s, 0.0).astype(BF16)
        heads = []
        for h in range(GLA_HEADS):
            vh = v_ref[b, ci * lc:(ci + 1) * lc, h * HEAD_V:(h + 1) * HEAD_V].astype(BF16)
            heads.append(_dot(s[h * lc:(h + 1) * lc, :], vh))
        o_intra = jnp.concatenate(heads, axis=-1)
        return o_intra, q_dec.astype(BF16), (k_inc * jnp.exp(cum_last)).astype(BF16)

    def chunk_exact(b, ci, q, k, cum, cum_last):
        e_row = lax.broadcasted_iota(jnp.int32, (D_GLA_K, D_GLA_V), 0) // HEAD_K
        e_col = lax.broadcasted_iota(jnp.int32, (D_GLA_K, D_GLA_V), 1) // HEAD_V
        expand = (e_row == e_col).astype(BF16)

        def key_row(j, o):
            kj = k_ref[b, pl.ds(ci * lc + j, 1), :]
            cj = cum_ref[b, pl.ds(ci * lc + j, 1), :]
            vj = v_ref[b, pl.ds(ci * lc + j, 1), :]
            p = q * kj * jnp.exp(jnp.minimum(cum - cj, 0.0))
            p = jnp.where(rows1 >= j, p, 0.0).astype(BF16)
            return o + _dot(p, expand) * vj

        o_intra = lax.fori_loop(0, lc, key_row, jnp.zeros((lc, D_GLA_V), F32))
        return (o_intra, (q * jnp.exp(cum)).astype(BF16),
                (k * jnp.exp(cum_last - cum)).astype(BF16))

    def gla_tile(chunk_terms):
        for ci in range(n_chunks):
            for b in range(nb):
                rows = slice(ci * lc, (ci + 1) * lc)
                q = q_ref[b, rows, :]
                k = k_ref[b, rows, :]
                cum = cum_ref[b, rows, :]
                cum_last = cum[lc - 1:lc, :]
                o_intra, qh, kh = chunk_terms(b, ci, q, k, cum, cum_last)

                st = st_ref[b]
                o_inter = _dot_nt(qh, st.astype(BF16))
                vb = v_ref[b, rows, :].astype(BF16)
                update = jnp.concatenate(
                    [_dot_tn(vb[:, h * HEAD_V:(h + 1) * HEAD_V],
                             jnp.where(klane == h, kh, jnp.zeros_like(kh)))
                     for h in range(GLA_HEADS)], axis=0)
                st_ref[b] = st * jnp.exp(cum_last) + update

                o = o_inter + o_intra
                normed = []
                for h in range(GLA_HEADS):
                    oh = o[:, h * HEAD_V:(h + 1) * HEAD_V]
                    ms = jnp.mean(oh * oh, axis=-1, keepdims=True)
                    normed.append(oh * lax.rsqrt(ms + EPS))
                on = jnp.concatenate(normed, axis=-1) * gng_ref[...]
                y_gla = on * gate_ref[b, rows, :]
                y_ref[b, rows, D_CONV:D_CONV + D_GLA_V] = y_gla.astype(y_ref.dtype)

    @pl.when(max_decay <= GLA_MAX_FACTORISED_DECAY)
    def _():
        gla_tile(chunk_factorised)

    @pl.when(max_decay > GLA_MAX_FACTORISED_DECAY)
    def _():
        gla_tile(chunk_exact)


def _mixer(u, q, k, la, v, gate, cw, cb, lng, lnb, gng):
    b_sz, t_len, _ = u.shape
    tb = TOKENS_MIX
    assert t_len % tb == 0 and tb % GLA_CHUNK == 0 and tb % CONV_ROWS == 0 and tb % NORM_ROWS == 0

    def tok(width):
        return pl.BlockSpec((b_sz, tb, width), lambda t: (0, t, 0))

    def const(arr):
        return pl.BlockSpec(arr.shape, lambda t: (0,) * arr.ndim)

    return pl.pallas_call(
        _mixer_kernel,
        grid=(t_len // tb,),
        in_specs=[tok(D_CONV), tok(D_GLA_K), tok(D_GLA_K), tok(D_GLA_K), tok(D_GLA_V),
                  tok(D_GLA_V), const(cw), const(cb), const(lng), const(lnb), const(gng)],
        out_specs=tok(D_CONV + D_GLA_V),
        out_shape=jax.ShapeDtypeStruct((b_sz, t_len, D_CONV + D_GLA_V), BF16),
        scratch_shapes=[
            pltpu.VMEM((b_sz, SUBLANES, D_CONV // LANES, CONV_HALO + tb, LANES), F32),
            pltpu.VMEM((b_sz, tb, D_CONV), F32),
            pltpu.VMEM((b_sz, D_GLA_V, D_GLA_K), F32),
            pltpu.VMEM((b_sz, tb, D_GLA_K), F32),
        ],
        compiler_params=pltpu.CompilerParams(
            dimension_semantics=("arbitrary",), vmem_limit_bytes=VMEM_LIMIT_BYTES),
        name="mixer",
    )(u, q, k, la, v, gate, cw, cb, lng, lnb, gng)


def _out_ffn_kernel(x_ref, y_ref, gt1_ref, sc2_ref, sh2_ref, gt2_ref, scf_ref, shf_ref,
                    gffn_ref, gfin_ref, wout_hbm, wgu_hbm, wo_hbm, o_ref,
                    wout_ref, wgu_ref, wo_ref, stage_wide_ref, stage_ref, sem_wide, sem):
    @pl.when((pl.program_id(0) == 0) & (pl.program_id(1) == 0))
    def _():
        _stage_bf16(wout_hbm, wout_ref, stage_ref, sem)
        _stage_bf16(wgu_hbm, wgu_ref, stage_wide_ref, sem_wide)
        _stage_bf16(wo_hbm, wo_ref, stage_ref, sem)

    tm = x_ref.shape[1]
    groups = [slice(r, r + FFN_ROWS) for r in range(0, tm, FFN_ROWS)]

    def out_proj(rows):
        return x_ref[0, rows, :] + gt1_ref[0] * _dot(y_ref[0, rows, :], wout_ref[...])

    def ffn_input(x1):
        ms = jnp.mean(x1 * x1, axis=-1, keepdims=True)
        h = x1 * lax.rsqrt(ms + EPS) * gffn_ref[...]
        return (h * (1.0 + sc2_ref[0]) + sh2_ref[0]).astype(BF16)

    def ffn(x1, hb):
        acc = jnp.zeros_like(x1)
        for lo, hi in FFN_SLABS:
            gate = _dot(hb, wgu_ref[:, lo:hi])
            up = _dot(hb, wgu_ref[:, D_FF + lo:D_FF + hi])
            act = (_silu(gate) * up).astype(BF16)
            acc = acc + _dot(act, wo_ref[lo:hi, :])
        return x1 + gt2_ref[0] * acc

    def final_norm(rows, x2):
        ms = jnp.mean(x2 * x2, axis=-1, keepdims=True)
        xn = x2 * lax.rsqrt(ms + EPS) * gfin_ref[...]
        o_ref[0, rows, :] = xn * (1.0 + scf_ref[0]) + shf_ref[0]

    x1s = [out_proj(rows) for rows in groups]
    pending = None
    for rows, x1 in zip(groups, x1s):
        hb = ffn_input(x1)
        if pending is not None:
            final_norm(*pending)
        pending = (rows, ffn(x1, hb))
    final_norm(*pending)


def _out_ffn(x, y, mod, gffn, gfin, wout, wgu, wo):
    b_sz, t_len, d = x.shape
    tm = TOKENS_FFN
    assert t_len % tm == 0 and tm % FFN_ROWS == 0

    def tok(width):
        return pl.BlockSpec((1, tm, width), lambda b, t: (b, t, 0))

    def const(arr):
        return pl.BlockSpec(arr.shape, lambda b, t: (0,) * arr.ndim,
                            pipeline_mode=pl.Buffered(1))

    mods = (MOD_GT1, MOD_SC2, MOD_SH2, MOD_GT2, MOD_SCF, MOD_SHF)
    in_hbm = pl.BlockSpec(memory_space=pl.ANY)
    return pl.pallas_call(
        _out_ffn_kernel,
        grid=(b_sz, t_len // tm),
        in_specs=[tok(d), tok(d)] + [_mod_spec(m) for m in mods]
                 + [const(gffn), const(gfin), in_hbm, in_hbm, in_hbm],
        out_specs=tok(d),
        out_shape=jax.ShapeDtypeStruct((b_sz, t_len, d), F32),
        scratch_shapes=[
            pltpu.VMEM(wout.shape, BF16),
            pltpu.VMEM(wgu.shape, BF16),
            pltpu.VMEM(wo.shape, BF16),
            pltpu.VMEM((2, STAGE_ROWS_WIDE, wgu.shape[1]), F32),
            pltpu.VMEM((2, STAGE_ROWS, d), F32),
            pltpu.SemaphoreType.DMA((2,)),
            pltpu.SemaphoreType.DMA((2,)),
        ],
        compiler_params=pltpu.CompilerParams(
            dimension_semantics=("arbitrary", "arbitrary"), vmem_limit_bytes=VMEM_LIMIT_BYTES),
        name="out_ffn",
    )(x, y, *([mod] * len(mods)), gffn, gfin, wout, wgu, wo)


def kernel(x, c, w_ada, b_ada, g_mix, w_in, conv_w, conv_b, ln_g, ln_b, w_a2, b_a2,
           gla_norm_g, w_out, g_ffn, w_ffn_in, w_ffn_out, w_ada_final, b_ada_final, g_final):
    b_sz, t_len, d = x.shape
    assert w_ada.shape[0] == 1, "single-layer block"
    assert d == D_MODEL

    c_lanes = jnp.broadcast_to(c[:, :, None], (b_sz, d, LANES))
    mod = _modulation(c_lanes, w_ada[0], b_ada[0], w_ada_final, b_ada_final)

    w = jnp.concatenate(
        [w_in[0].astype(BF16), jnp.zeros((d, LANES - GATE_RANK), BF16)], axis=1)
    wa2 = jnp.pad(w_a2[0], ((0, LANES - GATE_RANK), (0, 0))).astype(BF16)

    u, q, k, v, gate, la = _in_proj(
        x, mod, g_mix[0].reshape(1, d), w, wa2, b_a2[0].reshape(1, D_GLA_K))

    cw = jnp.broadcast_to(conv_w[0][:, None, :], (CONV_KERNEL, SUBLANES, D_CONV))
    y = _mixer(u, q, k, la, v, gate, cw, conv_b[0].reshape(1, D_CONV),
               ln_g[0].reshape(1, D_CONV), ln_b[0].reshape(1, D_CONV),
               jnp.tile(gla_norm_g[0], GLA_HEADS).reshape(1, D_GLA_V))

    return _out_ffn(
        x, y, mod, g_ffn[0].reshape(1, d), g_final.reshape(1, d),
        w_out[0], w_ffn_in[0], w_ffn_out[0])
```

```python
import functools

import jax
import jax.numpy as jnp
from jax import lax
from jax.experimental import pallas as pl
from jax.experimental.pallas import tpu as pltpu

F32 = jnp.float32
BF16 = jnp.bfloat16

D_MODEL = 1024
D_CONV = 512
CONV_KERNEL = 31
D_GLA_V = 512
D_GLA_K = 256
GLA_HEADS = 4
HEAD_K = D_GLA_K // GLA_HEADS
HEAD_V = D_GLA_V // GLA_HEADS
GATE_RANK = 16
GATE_TAU = 16.0
D_FF = 2816
EPS = 1e-6

LANES = 128
SUBLANES = 8
MXU_DIM = 256
VMEM_LIMIT_BYTES = 56 * 1024 * 1024
MOD_ROWS = 128
TOKENS_IN = 1024
TOKENS_MIX = 512
TOKENS_FFN = 512
STAGE_ROWS_WIDE = 128
STAGE_ROWS = 512
CONV_HALO = 32
CONV_ROWS = 64
CONV_PARTIALS = 2
GLA_CHUNK = 128
GLA_MAX_FACTORISED_DECAY = 40.0
FFN_ROWS = 256
FFN_SLABS = ((0, 6 * MXU_DIM), (6 * MXU_DIM, D_FF))

IN_COLS = {}
_col = 0
for _name, _width in (("cv", D_CONV), ("cg", D_CONV), ("q", D_GLA_K), ("k", D_GLA_K),
                      ("v", D_GLA_V), ("og", D_GLA_V), ("a", LANES)):
    IN_COLS[_name] = (_col, _col + _width)
    _col += _width

MOD_SH1, MOD_SC1, MOD_GT1, MOD_SH2, MOD_SC2, MOD_GT2, MOD_SHF, MOD_SCF = range(8)


def _sigmoid(x):
    return 0.5 * jnp.tanh(0.5 * x) + 0.5


def _silu(x):
    h = 0.5 * x
    return h * jnp.tanh(h) + h


def _dot(a, b):
    return jnp.dot(a, b, preferred_element_type=F32)


def _dot_nt(a, b):
    return lax.dot_general(a, b, (((1,), (1,)), ((), ())), preferred_element_type=F32)


def _dot_tn(a, b):
    return lax.dot_general(a, b, (((0,), (0,)), ((), ())), preferred_element_type=F32)


def _split3(a):
    hi = a.astype(BF16)
    r1 = a - hi.astype(F32)
    mid = r1.astype(BF16)
    lo = (r1 - mid.astype(F32)).astype(BF16)
    return hi, mid, lo


def _stage_bf16(src_hbm, dst_ref, stage_ref, sem):
    n_rows, n_cols = src_hbm.shape
    chunk = stage_ref.shape[1]
    starts = list(range(0, n_rows, chunk))

    def copy(i):
        r, n = starts[i], min(chunk, n_rows - starts[i])
        return pltpu.make_async_copy(
            src_hbm.at[pl.ds(r, n), :], stage_ref.at[i % 2, pl.ds(0, n), :], sem.at[i % 2])

    copy(0).start()
    for i, r in enumerate(starts):
        if i + 1 < len(starts):
            copy(i + 1).start()
        copy(i).wait()
        n = min(chunk, n_rows - r)
        dst_ref[r:r + n, 0:n_cols] = stage_ref[i % 2, 0:n, :].astype(dst_ref.dtype)


def _modulation_kernel(c_ref, w_ref, b_ref, wf_ref, bf_ref, o_ref, acc_ref):
    j = pl.program_id(0)
    n_batch, k_rows, _ = c_ref.shape
    n = w_ref.shape[1]

    @pl.when(j == 0)
    def _():
        acc_ref[...] = jnp.zeros_like(acc_ref)

    c_act = []
    for b in range(n_batch):
        cb = c_ref[b]
        c_act.append(_silu(cb))
    for ref, offset in ((w_ref, 0), (wf_ref, n)):
        for t in range(ref.shape[1] // LANES):
            w = ref[:, t * LANES:(t + 1) * LANES]
            cols = slice(offset + t * LANES, offset + (t + 1) * LANES)
            for b in range(n_batch):
                prod = (w * c_act[b]).reshape(k_rows // SUBLANES, SUBLANES, LANES)
                acc_ref[b, :, cols] += jnp.sum(prod, axis=0)

    @pl.when(j == pl.num_programs(0) - 1)
    def _():
        bias = jnp.concatenate([b_ref[...], bf_ref[...]], axis=1)
        for b in range(n_batch):
            o_ref[b] = jnp.sum(acc_ref[b], axis=0, keepdims=True) + bias


def _modulation(c_lanes, w, b, wf, bf):
    n_batch, d, _ = c_lanes.shape
    n, nf = w.shape[1], wf.shape[1]
    assert d % MOD_ROWS == 0 and n % LANES == 0 and nf % LANES == 0
    return pl.pallas_call(
        _modulation_kernel,
        grid=(d // MOD_ROWS,),
        in_specs=[
            pl.BlockSpec((n_batch, MOD_ROWS, LANES), lambda j: (0, j, 0)),
            pl.BlockSpec((MOD_ROWS, n), lambda j: (j, 0)),
            pl.BlockSpec((1, n), lambda j: (0, 0)),
            pl.BlockSpec((MOD_ROWS, nf), lambda j: (j, 0)),
            pl.BlockSpec((1, nf), lambda j: (0, 0)),
        ],
        out_specs=pl.BlockSpec((n_batch, 1, n + nf), lambda j: (0, 0, 0)),
        out_shape=jax.ShapeDtypeStruct((n_batch, 1, n + nf), F32),
        scratch_shapes=[pltpu.VMEM((n_batch, SUBLANES, n + nf), F32)],
        compiler_params=pltpu.CompilerParams(
            dimension_semantics=("arbitrary",), vmem_limit_bytes=VMEM_LIMIT_BYTES),
        name="modulation",
    )(c_lanes, w, b.reshape(1, n), wf, bf.reshape(1, nf))


def _mod_spec(which):
    return pl.BlockSpec((1, 1, D_MODEL), lambda b, t: (b, 0, which))


def _in_proj_kernel(x_ref, sc_ref, sh_ref, g_ref, w_ref, wa2_ref, ba2_ref,
                    u_ref, q_ref, k_ref, v_ref, gate_ref, la_ref):
    x = x_ref[0]
    ms = jnp.mean(x * x, axis=-1, keepdims=True)
    h = x * lax.rsqrt(ms + EPS) * g_ref[...]
    h = h * (1.0 + sc_ref[0]) + sh_ref[0]
    hb = h.astype(BF16)

    proj = _dot(hb, w_ref[...])

    def piece(name):
        lo, hi = IN_COLS[name]
        return proj[:, lo:hi]

    u_ref[0] = piece("cv") * _sigmoid(piece("cg"))
    q_ref[0] = piece("q") * (HEAD_K ** -0.5)
    k_ref[0] = piece("k")
    v_ref[0] = piece("v")
    og = piece("og")
    gate_ref[0] = _silu(og)

    a_low = piece("a")
    z = _dot(a_low.astype(BF16), wa2_ref[...]) + ba2_ref[...]
    log_sig = jnp.minimum(z, 0.0) - jnp.log(1.0 + jnp.exp(-jnp.abs(z)))
    la_ref[0] = log_sig * (1.0 / GATE_TAU)


def _in_proj(x, mod, g, w, wa2, ba2):
    b_sz, t_len, d = x.shape
    tm = TOKENS_IN
    assert t_len % tm == 0

    def tok(width):
        return pl.BlockSpec((1, tm, width), lambda b, t: (b, t, 0))

    def const(arr):
        return pl.BlockSpec(arr.shape, lambda b, t: (0,) * arr.ndim)

    def out(width):
        return jax.ShapeDtypeStruct((b_sz, t_len, width), F32)

    return pl.pallas_call(
        _in_proj_kernel,
        grid=(b_sz, t_len // tm),
        in_specs=[tok(d), _mod_spec(MOD_SC1), _mod_spec(MOD_SH1), const(g), const(w),
                  const(wa2), const(ba2)],
        out_specs=[tok(D_CONV), tok(D_GLA_K), tok(D_GLA_K), tok(D_GLA_V), tok(D_GLA_V),
                   tok(D_GLA_K)],
        out_shape=[out(D_CONV), out(D_GLA_K), out(D_GLA_K), out(D_GLA_V), out(D_GLA_V),
                   out(D_GLA_K)],
        compiler_params=pltpu.CompilerParams(
            dimension_semantics=("arbitrary", "arbitrary"), vmem_limit_bytes=VMEM_LIMIT_BYTES),
        name="in_proj",
    )(x, mod, mod, g, w, wa2, ba2)


def _mixer_kernel(u_ref, q_ref, k_ref, la_ref, v_ref, gate_ref, cw_ref, cb_ref, gng_ref,
                  yc_ref, yg_ref, win_ref, st_ref, cum_ref):
    nb, tb = u_ref.shape[0], u_ref.shape[1]
    lc = GLA_CHUNK

    @pl.when(pl.program_id(0) == 0)
    def _():
        win_ref[...] = jnp.zeros_like(win_ref)
        st_ref[...] = jnp.zeros_like(st_ref)

    for b in range(nb):
        for c in range(D_CONV // LANES):
            u_lanes = u_ref[b, :, c * LANES:(c + 1) * LANES]
            for phase in range(SUBLANES):
                win_ref[b, phase, c, CONV_HALO - phase:CONV_HALO - phase + tb, :] = u_lanes
    first_tap = CONV_HALO - (CONV_KERNEL - 1)

    for c in range(D_CONV // LANES):
        lanes = slice(c * LANES, (c + 1) * LANES)
        taps = []
        for phase in range(SUBLANES):
            for base in range(0, CONV_HALO + SUBLANES, SUBLANES):
                tap = base + phase - first_tap
                if 0 <= tap < CONV_KERNEL:
                    taps.append((cw_ref[tap, :, lanes], phase, base))
        bias = cb_ref[:, lanes]

        def conv_tile(i, carry, c=c, lanes=lanes, taps=taps, bias=bias):
            r0 = pl.multiple_of(i * CONV_ROWS, CONV_ROWS)
            for b in range(nb):
                parts = [jnp.broadcast_to(bias, (CONV_ROWS, LANES))] + [None] * (CONV_PARTIALS - 1)
                for n, (w8, phase, base) in enumerate(taps):
                    w_rows = jnp.concatenate([w8] * (CONV_ROWS // SUBLANES), axis=0)
                    term = w_rows * win_ref[b, phase, c, pl.ds(r0 + base, CONV_ROWS), :]
                    p = n % CONV_PARTIALS
                    parts[p] = term if parts[p] is None else parts[p] + term
                yc_ref[b, pl.ds(r0, CONV_ROWS), lanes] = functools.reduce(lambda x, y: x + y, parts)
            return carry

        lax.fori_loop(0, tb // CONV_ROWS, conv_tile, 0)

    win_ref[:, :, :, 0:CONV_HALO, :] = win_ref[:, :, :, tb:tb + CONV_HALO, :]

    n_chunks = tb // lc
    row = lax.broadcasted_iota(jnp.int32, (lc, lc), 0)
    col = lax.broadcasted_iota(jnp.int32, (lc, lc), 1)
    tril = (col <= row).astype(BF16)
    klane = lax.broadcasted_iota(jnp.int32, (1, D_GLA_K), 1) // HEAD_K
    srow = lax.broadcasted_iota(jnp.int32, (GLA_HEADS * lc, lc), 0)
    scol = lax.broadcasted_iota(jnp.int32, (GLA_HEADS * lc, lc), 1)
    causal = scol <= (srow % lc)
    rows1 = lax.broadcasted_iota(jnp.int32, (lc, 1), 0)

    decay = None
    for b in range(nb):
        for ci in range(n_chunks):
            hi, mid, lo = _split3(la_ref[b, ci * lc:(ci + 1) * lc, :])
            cum = _dot(tril, hi) + _dot(tril, mid) + _dot(tril, lo)
            cum_ref[b, ci * lc:(ci + 1) * lc, :] = cum
            total = -cum[lc - 1:lc, :]
            decay = total if decay is None else jnp.maximum(decay, total)
    max_decay = jnp.max(decay)

    def chunk_factorised(b, ci, q, k, cum, cum_last):
        ref = cum[lc // 2 - 1:lc // 2, :]
        q_dec = q * jnp.exp(cum)
        k_inc = k * jnp.exp(-cum)
        qt = (q_dec * jnp.exp(-ref)).astype(BF16)
        kt = (k_inc * jnp.exp(ref)).astype(BF16)
        qbd = jnp.concatenate(
            [jnp.where(klane == h, qt, jnp.zeros_like(qt)) for h in range(GLA_HEADS)], axis=0)
        s = _dot_nt(qbd, kt)
        s = jnp.where(causal, s, 0.0).astype(BF16)
        heads = []
        for h in range(GLA_HEADS):
            vh = v_ref[b, ci * lc:(ci + 1) * lc, h * HEAD_V:(h + 1) * HEAD_V].astype(BF16)
            heads.append(_dot(s[h * lc:(h + 1) * lc, :], vh))
        o_intra = jnp.concatenate(heads, axis=-1)
        return o_intra, q_dec.astype(BF16), (k_inc * jnp.exp(cum_last)).astype(BF16)

    def chunk_exact(b, ci, q, k, cum, cum_last):
        e_row = lax.broadcasted_iota(jnp.int32, (D_GLA_K, D_GLA_V), 0) // HEAD_K
        e_col = lax.broadcasted_iota(jnp.int32, (D_GLA_K, D_GLA_V), 1) // HEAD_V
        expand = (e_row == e_col).astype(BF16)

        def key_row(j, o):
            kj = k_ref[b, pl.ds(ci * lc + j, 1), :]
            cj = cum_ref[b, pl.ds(ci * lc + j, 1), :]
            vj = v_ref[b, pl.ds(ci * lc + j, 1), :]
            p = q * kj * jnp.exp(jnp.minimum(cum - cj, 0.0))
            p = jnp.where(rows1 >= j, p, 0.0).astype(BF16)
            return o + _dot(p, expand) * vj

        o_intra = lax.fori_loop(0, lc, key_row, jnp.zeros((lc, D_GLA_V), F32))
        return (o_intra, (q * jnp.exp(cum)).astype(BF16),
                (k * jnp.exp(cum_last - cum)).astype(BF16))

    def gla_tile(chunk_terms):
        for ci in range(n_chunks):
            for b in range(nb):
                rows = slice(ci * lc, (ci + 1) * lc)
                q = q_ref[b, rows, :]
                k = k_ref[b, rows, :]
                cum = cum_ref[b, rows, :]
                cum_last = cum[lc - 1:lc, :]
                o_intra, qh, kh = chunk_terms(b, ci, q, k, cum, cum_last)

                st = st_ref[b]
                o_inter = _dot_nt(qh, st.astype(BF16))
                vb = v_ref[b, rows, :].astype(BF16)
                update = jnp.concatenate(
                    [_dot_tn(vb[:, h * HEAD_V:(h + 1) * HEAD_V],
                             jnp.where(klane == h, kh, jnp.zeros_like(kh)))
                     for h in range(GLA_HEADS)], axis=0)
                st_ref[b] = st * jnp.exp(cum_last) + update

                o = o_inter + o_intra
                normed = []
                for h in range(GLA_HEADS):
                    oh = o[:, h * HEAD_V:(h + 1) * HEAD_V]
                    ms = jnp.mean(oh * oh, axis=-1, keepdims=True)
                    normed.append(oh * lax.rsqrt(ms + EPS))
                on = jnp.concatenate(normed, axis=-1) * gng_ref[...]
                y_gla = on * gate_ref[b, rows, :]
                yg_ref[b, rows, :] = y_gla.astype(yg_ref.dtype)

    @pl.when(max_decay <= GLA_MAX_FACTORISED_DECAY)
    def _():
        gla_tile(chunk_factorised)

    @pl.when(max_decay > GLA_MAX_FACTORISED_DECAY)
    def _():
        gla_tile(chunk_exact)


def _mixer(u, q, k, la, v, gate, cw, cb, gng):
    b_sz, t_len, _ = u.shape
    tb = TOKENS_MIX
    assert t_len % tb == 0 and tb % GLA_CHUNK == 0 and tb % CONV_ROWS == 0

    def tok(width):
        return pl.BlockSpec((b_sz, tb, width), lambda t: (0, t, 0))

    def const(arr):
        return pl.BlockSpec(arr.shape, lambda t: (0,) * arr.ndim)

    return pl.pallas_call(
        _mixer_kernel,
        grid=(t_len // tb,),
        in_specs=[tok(D_CONV), tok(D_GLA_K), tok(D_GLA_K), tok(D_GLA_K), tok(D_GLA_V),
                  tok(D_GLA_V), const(cw), const(cb), const(gng)],
        out_specs=[tok(D_CONV), tok(D_GLA_V)],
        out_shape=[jax.ShapeDtypeStruct((b_sz, t_len, D_CONV), F32),
                   jax.ShapeDtypeStruct((b_sz, t_len, D_GLA_V), BF16)],
        scratch_shapes=[
            pltpu.VMEM((b_sz, SUBLANES, D_CONV // LANES, CONV_HALO + tb, LANES), F32),
            pltpu.VMEM((b_sz, D_GLA_V, D_GLA_K), F32),
            pltpu.VMEM((b_sz, tb, D_GLA_K), F32),
        ],
        compiler_params=pltpu.CompilerParams(
            dimension_semantics=("arbitrary",), vmem_limit_bytes=VMEM_LIMIT_BYTES),
        name="mixer",
    )(u, q, k, la, v, gate, cw, cb, gng)


def _out_ffn_kernel(x_ref, yc_ref, yg_ref, gt1_ref, sc2_ref, sh2_ref, gt2_ref, scf_ref, shf_ref,
                    gffn_ref, gfin_ref, lng_ref, lnb_ref, wout_hbm, wgu_hbm, wo_hbm, o_ref,
                    wout_ref, wgu_ref, wo_ref, stage_wide_ref, stage_ref, sem_wide, sem):
    @pl.when((pl.program_id(0) == 0) & (pl.program_id(1) == 0))
    def _():
        _stage_bf16(wout_hbm, wout_ref, stage_ref, sem)
        _stage_bf16(wgu_hbm, wgu_ref, stage_wide_ref, sem_wide)
        _stage_bf16(wo_hbm, wo_ref, stage_ref, sem)

    tm = x_ref.shape[1]
    groups = [slice(r, r + FFN_ROWS) for r in range(0, tm, FFN_ROWS)]

    def out_proj(rows):
        conv = yc_ref[0, rows, :]
        mu = jnp.mean(conv, axis=-1, keepdims=True)
        cen = conv - mu
        var = jnp.mean(cen * cen, axis=-1, keepdims=True)
        y_conv = _silu(cen * lax.rsqrt(var + EPS) * lng_ref[...] + lnb_ref[...]).astype(BF16)
        mix = (_dot(y_conv, wout_ref[0:D_CONV, :])
               + _dot(yg_ref[0, rows, :], wout_ref[D_CONV:D_CONV + D_GLA_V, :]))
        return x_ref[0, rows, :] + gt1_ref[0] * mix

    def ffn_input(x1):
        ms = jnp.mean(x1 * x1, axis=-1, keepdims=True)
        h = x1 * lax.rsqrt(ms + EPS) * gffn_ref[...]
        return (h * (1.0 + sc2_ref[0]) + sh2_ref[0]).astype(BF16)

    def ffn(x1, hb):
        acc = jnp.zeros_like(x1)
        for lo, hi in FFN_SLABS:
            gate = _dot(hb, wgu_ref[:, lo:hi])
            up = _dot(hb, wgu_ref[:, D_FF + lo:D_FF + hi])
            act = (_silu(gate) * up).astype(BF16)
            acc = acc + _dot(act, wo_ref[lo:hi, :])
        return x1 + gt2_ref[0] * acc

    def final_norm(rows, x2):
        ms = jnp.mean(x2 * x2, axis=-1, keepdims=True)
        xn = x2 * lax.rsqrt(ms + EPS) * gfin_ref[...]
        o_ref[0, rows, :] = xn * (1.0 + scf_ref[0]) + shf_ref[0]

    x1s = [out_proj(rows) for rows in groups]
    pending = None
    for rows, x1 in zip(groups, x1s):
        hb = ffn_input(x1)
        if pending is not None:
            final_norm(*pending)
        pending = (rows, ffn(x1, hb))
    final_norm(*pending)


def _out_ffn(x, yc, yg, mod, gffn, gfin, lng, lnb, wout, wgu, wo):
    b_sz, t_len, d = x.shape
    tm = TOKENS_FFN
    assert t_len % tm == 0 and tm % FFN_ROWS == 0

    def tok(width):
        return pl.BlockSpec((1, tm, width), lambda b, t: (b, t, 0))

    def const(arr):
        return pl.BlockSpec(arr.shape, lambda b, t: (0,) * arr.ndim,
                            pipeline_mode=pl.Buffered(1))

    mods = (MOD_GT1, MOD_SC2, MOD_SH2, MOD_GT2, MOD_SCF, MOD_SHF)
    in_hbm = pl.BlockSpec(memory_space=pl.ANY)
    return pl.pallas_call(
        _out_ffn_kernel,
        grid=(b_sz, t_len // tm),
        in_specs=[tok(d), tok(D_CONV), tok(D_GLA_V)] + [_mod_spec(m) for m in mods]
                 + [const(gffn), const(gfin), const(lng), const(lnb), in_hbm, in_hbm, in_hbm],
        out_specs=tok(d),
        out_shape=jax.ShapeDtypeStruct((b_sz, t_len, d), F32),
        scratch_shapes=[
            pltpu.VMEM(wout.shape, BF16),
            pltpu.VMEM(wgu.shape, BF16),
            pltpu.VMEM(wo.shape, BF16),
            pltpu.VMEM((2, STAGE_ROWS_WIDE, wgu.shape[1]), F32),
            pltpu.VMEM((2, STAGE_ROWS, d), F32),
            pltpu.SemaphoreType.DMA((2,)),
            pltpu.SemaphoreType.DMA((2,)),
        ],
        compiler_params=pltpu.CompilerParams(
            dimension_semantics=("arbitrary", "arbitrary"), vmem_limit_bytes=VMEM_LIMIT_BYTES),
        name="out_ffn",
    )(x, yc, yg, *([mod] * len(mods)), gffn, gfin, lng, lnb, wout, wgu, wo)


def kernel(x, c, w_ada, b_ada, g_mix, w_in, conv_w, conv_b, ln_g, ln_b, w_a2, b_a2,
           gla_norm_g, w_out, g_ffn, w_ffn_in, w_ffn_out, w_ada_final, b_ada_final, g_final):
    b_sz, t_len, d = x.shape
    assert w_ada.shape[0] == 1, "single-layer block"
    assert d == D_MODEL

    c_lanes = jnp.broadcast_to(c[:, :, None], (b_sz, d, LANES))
    mod = _modulation(c_lanes, w_ada[0], b_ada[0], w_ada_final, b_ada_final)

    w = jnp.concatenate(
        [w_in[0].astype(BF16), jnp.zeros((d, LANES - GATE_RANK), BF16)], axis=1)
    wa2 = jnp.pad(w_a2[0], ((0, LANES - GATE_RANK), (0, 0))).astype(BF16)

    u, q, k, v, gate, la = _in_proj(
        x, mod, g_mix[0].reshape(1, d), w, wa2, b_a2[0].reshape(1, D_GLA_K))

    cw = jnp.broadcast_to(conv_w[0][:, None, :], (CONV_KERNEL, SUBLANES, D_CONV))
    y_conv, y_gla = _mixer(u, q, k, la, v, gate, cw, conv_b[0].reshape(1, D_CONV),
                           jnp.tile(gla_norm_g[0], GLA_HEADS).reshape(1, D_GLA_V))

    return _out_ffn(
        x, y_conv, y_gla, mod, g_ffn[0].reshape(1, d), g_final.reshape(1, d),
        ln_g[0].reshape(1, D_CONV), ln_b[0].reshape(1, D_CONV),
        w_out[0], w_ffn_in[0], w_ffn_out[0])
```

```python
import functools

import jax
import jax.numpy as jnp
from jax import lax
from jax.experimental import pallas as pl
from jax.experimental.pallas import tpu as pltpu

F32 = jnp.float32
BF16 = jnp.bfloat16

D_MODEL = 1024
D_CONV = 512
CONV_KERNEL = 31
D_GLA_V = 512
D_GLA_K = 256
GLA_HEADS = 4
HEAD_K = D_GLA_K // GLA_HEADS
HEAD_V = D_GLA_V // GLA_HEADS
GATE_RANK = 16
GATE_TAU = 16.0
D_FF = 2816
EPS = 1e-6

LANES = 128
SUBLANES = 8
MXU_DIM = 256
VMEM_LIMIT_BYTES = 56 * 1024 * 1024
MOD_ROWS = 128
TOKENS_IN = 1024
IN_ROWS = 512
TOKENS_MIX = 512
TOKENS_FFN = 512
STAGE_ROWS_WIDE = 128
STAGE_ROWS = 512
CONV_HALO = 32
CONV_ROWS = 64
CONV_PARTIALS = 2
GLA_CHUNK = 128
GLA_MAX_FACTORISED_DECAY = 40.0
FFN_ROWS = 256
FFN_SLABS = ((0, 6 * MXU_DIM), (6 * MXU_DIM, D_FF))

IN_COLS = {}
_col = 0
for _name, _width in (("cv", D_CONV), ("cg", D_CONV), ("q", D_GLA_K), ("k", D_GLA_K),
                      ("v", D_GLA_V), ("og", D_GLA_V), ("a", LANES)):
    IN_COLS[_name] = (_col, _col + _width)
    _col += _width

MOD_SH1, MOD_SC1, MOD_GT1, MOD_SH2, MOD_SC2, MOD_GT2, MOD_SHF, MOD_SCF = range(8)


def _sigmoid(x):
    return 0.5 * jnp.tanh(0.5 * x) + 0.5


def _silu(x):
    h = 0.5 * x
    return h * jnp.tanh(h) + h


def _dot(a, b):
    return jnp.dot(a, b, preferred_element_type=F32)


def _dot_nt(a, b):
    return lax.dot_general(a, b, (((1,), (1,)), ((), ())), preferred_element_type=F32)


def _dot_tn(a, b):
    return lax.dot_general(a, b, (((0,), (0,)), ((), ())), preferred_element_type=F32)


def _split3(a):
    hi = a.astype(BF16)
    r1 = a - hi.astype(F32)
    mid = r1.astype(BF16)
    lo = (r1 - mid.astype(F32)).astype(BF16)
    return hi, mid, lo


def _stage_bf16(src_hbm, dst_ref, stage_ref, sem):
    n_rows, n_cols = src_hbm.shape
    chunk = stage_ref.shape[1]
    starts = list(range(0, n_rows, chunk))

    def copy(i):
        r, n = starts[i], min(chunk, n_rows - starts[i])
        return pltpu.make_async_copy(
            src_hbm.at[pl.ds(r, n), :], stage_ref.at[i % 2, pl.ds(0, n), :], sem.at[i % 2])

    copy(0).start()
    for i, r in enumerate(starts):
        if i + 1 < len(starts):
            copy(i + 1).start()
        copy(i).wait()
        n = min(chunk, n_rows - r)
        dst_ref[r:r + n, 0:n_cols] = stage_ref[i % 2, 0:n, :].astype(dst_ref.dtype)


def _modulation_kernel(c_ref, w_ref, b_ref, wf_ref, bf_ref, o_ref, acc_ref):
    j = pl.program_id(0)
    n_batch, k_rows, _ = c_ref.shape
    n = w_ref.shape[1]

    @pl.when(j == 0)
    def _():
        acc_ref[...] = jnp.zeros_like(acc_ref)

    c_act = []
    for b in range(n_batch):
        cb = c_ref[b]
        c_act.append(_silu(cb))
    for ref, offset in ((w_ref, 0), (wf_ref, n)):
        for t in range(ref.shape[1] // LANES):
            w = ref[:, t * LANES:(t + 1) * LANES]
            cols = slice(offset + t * LANES, offset + (t + 1) * LANES)
            for b in range(n_batch):
                prod = (w * c_act[b]).reshape(k_rows // SUBLANES, SUBLANES, LANES)
                acc_ref[b, :, cols] += jnp.sum(prod, axis=0)

    @pl.when(j == pl.num_programs(0) - 1)
    def _():
        bias = jnp.concatenate([b_ref[...], bf_ref[...]], axis=1)
        for b in range(n_batch):
            o_ref[b] = jnp.sum(acc_ref[b], axis=0, keepdims=True) + bias


def _modulation(c_lanes, w, b, wf, bf):
    n_batch, d, _ = c_lanes.shape
    n, nf = w.shape[1], wf.shape[1]
    assert d % MOD_ROWS == 0 and n % LANES == 0 and nf % LANES == 0
    return pl.pallas_call(
        _modulation_kernel,
        grid=(d // MOD_ROWS,),
        in_specs=[
            pl.BlockSpec((n_batch, MOD_ROWS, LANES), lambda j: (0, j, 0)),
            pl.BlockSpec((MOD_ROWS, n), lambda j: (j, 0)),
            pl.BlockSpec((1, n), lambda j: (0, 0)),
            pl.BlockSpec((MOD_ROWS, nf), lambda j: (j, 0)),
            pl.BlockSpec((1, nf), lambda j: (0, 0)),
        ],
        out_specs=pl.BlockSpec((n_batch, 1, n + nf), lambda j: (0, 0, 0)),
        out_shape=jax.ShapeDtypeStruct((n_batch, 1, n + nf), F32),
        scratch_shapes=[pltpu.VMEM((n_batch, SUBLANES, n + nf), F32)],
        compiler_params=pltpu.CompilerParams(
            dimension_semantics=("arbitrary",), vmem_limit_bytes=VMEM_LIMIT_BYTES),
        name="modulation",
    )(c_lanes, w, b.reshape(1, n), wf, bf.reshape(1, nf))


def _mod_spec(which):
    return pl.BlockSpec((1, 1, D_MODEL), lambda b, t: (b, 0, which))


def _in_proj_kernel(x_ref, sc_ref, sh_ref, g_ref, w_ref, wa2_ref, ba2_ref,
                    u_ref, q_ref, k_ref, v_ref, gate_ref, la_ref):
    tm = x_ref.shape[1]
    groups = [slice(r, r + IN_ROWS) for r in range(0, tm, IN_ROWS)]

    def normed(rows):
        x = x_ref[0, rows, :]
        ms = jnp.mean(x * x, axis=-1, keepdims=True)
        h = x * lax.rsqrt(ms + EPS) * g_ref[...]
        return (h * (1.0 + sc_ref[0]) + sh_ref[0]).astype(BF16)

    def epilogue(rows, proj):
        def piece(name):
            lo, hi = IN_COLS[name]
            return proj[:, lo:hi]

        u_ref[0, rows, :] = piece("cv") * _sigmoid(piece("cg"))
        q_ref[0, rows, :] = piece("q") * (HEAD_K ** -0.5)
        k_ref[0, rows, :] = piece("k")
        v_ref[0, rows, :] = piece("v")
        gate_ref[0, rows, :] = _silu(piece("og"))

        a_low = piece("a")
        z = _dot(a_low.astype(BF16), wa2_ref[...]) + ba2_ref[...]
        log_sig = jnp.minimum(z, 0.0) - jnp.log(1.0 + jnp.exp(-jnp.abs(z)))
        la_ref[0, rows, :] = log_sig * (1.0 / GATE_TAU)

    pending = None
    for rows in groups:
        proj = _dot(normed(rows), w_ref[...])
        if pending is not None:
            epilogue(*pending)
        pending = (rows, proj)
    epilogue(*pending)


def _in_proj(x, mod, g, w, wa2, ba2):
    b_sz, t_len, d = x.shape
    tm = TOKENS_IN
    assert t_len % tm == 0

    def tok(width):
        return pl.BlockSpec((1, tm, width), lambda b, t: (b, t, 0))

    def const(arr):
        return pl.BlockSpec(arr.shape, lambda b, t: (0,) * arr.ndim)

    def out(width):
        return jax.ShapeDtypeStruct((b_sz, t_len, width), F32)

    return pl.pallas_call(
        _in_proj_kernel,
        grid=(b_sz, t_len // tm),
        in_specs=[tok(d), _mod_spec(MOD_SC1), _mod_spec(MOD_SH1), const(g), const(w),
                  const(wa2), const(ba2)],
        out_specs=[tok(D_CONV), tok(D_GLA_K), tok(D_GLA_K), tok(D_GLA_V), tok(D_GLA_V),
                   tok(D_GLA_K)],
        out_shape=[out(D_CONV), out(D_GLA_K), out(D_GLA_K), out(D_GLA_V), out(D_GLA_V),
                   out(D_GLA_K)],
        compiler_params=pltpu.CompilerParams(
            dimension_semantics=("arbitrary", "arbitrary"), vmem_limit_bytes=VMEM_LIMIT_BYTES),
        name="in_proj",
    )(x, mod, mod, g, w, wa2, ba2)


def _mixer_kernel(u_ref, q_ref, k_ref, la_ref, v_ref, gate_ref, cw_ref, cb_ref, gng_ref,
                  yc_ref, yg_ref, win_ref, st_ref, cum_ref):
    nb, tb = u_ref.shape[0], u_ref.shape[1]
    lc = GLA_CHUNK

    @pl.when(pl.program_id(0) == 0)
    def _():
        win_ref[...] = jnp.zeros_like(win_ref)
        st_ref[...] = jnp.zeros_like(st_ref)

    for b in range(nb):
        for c in range(D_CONV // LANES):
            u_lanes = u_ref[b, :, c * LANES:(c + 1) * LANES]
            for phase in range(SUBLANES):
                win_ref[b, phase, c, CONV_HALO - phase:CONV_HALO - phase + tb, :] = u_lanes
    first_tap = CONV_HALO - (CONV_KERNEL - 1)

    for c in range(D_CONV // LANES):
        lanes = slice(c * LANES, (c + 1) * LANES)
        taps = []
        for phase in range(SUBLANES):
            for base in range(0, CONV_HALO + SUBLANES, SUBLANES):
                tap = base + phase - first_tap
                if 0 <= tap < CONV_KERNEL:
                    taps.append((cw_ref[tap, :, lanes], phase, base))
        bias = cb_ref[:, lanes]

        def conv_tile(i, carry, c=c, lanes=lanes, taps=taps, bias=bias):
            r0 = pl.multiple_of(i * CONV_ROWS, CONV_ROWS)
            for b in range(nb):
                parts = [jnp.broadcast_to(bias, (CONV_ROWS, LANES))] + [None] * (CONV_PARTIALS - 1)
                for n, (w8, phase, base) in enumerate(taps):
                    w_rows = jnp.concatenate([w8] * (CONV_ROWS // SUBLANES), axis=0)
                    term = w_rows * win_ref[b, phase, c, pl.ds(r0 + base, CONV_ROWS), :]
                    p = n % CONV_PARTIALS
                    parts[p] = term if parts[p] is None else parts[p] + term
                yc_ref[b, pl.ds(r0, CONV_ROWS), lanes] = functools.reduce(lambda x, y: x + y, parts)
            return carry

        lax.fori_loop(0, tb // CONV_ROWS, conv_tile, 0)

    win_ref[:, :, :, 0:CONV_HALO, :] = win_ref[:, :, :, tb:tb + CONV_HALO, :]

    n_chunks = tb // lc
    row = lax.broadcasted_iota(jnp.int32, (lc, lc), 0)
    col = lax.broadcasted_iota(jnp.int32, (lc, lc), 1)
    tril = (col <= row).astype(BF16)
    klane = lax.broadcasted_iota(jnp.int32, (1, D_GLA_K), 1) // HEAD_K
    srow = lax.broadcasted_iota(jnp.int32, (GLA_HEADS * lc, lc), 0)
    scol = lax.broadcasted_iota(jnp.int32, (GLA_HEADS * lc, lc), 1)
    causal = scol <= (srow % lc)
    rows1 = lax.broadcasted_iota(jnp.int32, (lc, 1), 0)

    decay = None
    for b in range(nb):
        for ci in range(n_chunks):
            hi, mid, lo = _split3(la_ref[b, ci * lc:(ci + 1) * lc, :])
            cum = _dot(tril, hi) + _dot(tril, mid) + _dot(tril, lo)
            cum_ref[b, ci * lc:(ci + 1) * lc, :] = cum
            total = -cum[lc - 1:lc, :]
            decay = total if decay is None else jnp.maximum(decay, total)
    max_decay = jnp.max(decay)

    def chunk_factorised(b, ci, q, k, cum, cum_last):
        ref = cum[lc // 2 - 1:lc // 2, :]
        q_dec = q * jnp.exp(cum)
        k_inc = k * jnp.exp(-cum)
        qt = (q_dec * jnp.exp(-ref)).astype(BF16)
        kt = (k_inc * jnp.exp(ref)).astype(BF16)
        qbd = jnp.concatenate(
            [jnp.where(klane == h, qt, jnp.zeros_like(qt)) for h in range(GLA_HEADS)], axis=0)
        s = _dot_nt(qbd, kt)
        s = jnp.where(causal, s, 0.0).astype(BF16)
        heads = []
        for h in range(GLA_HEADS):
            vh = v_ref[b, ci * lc:(ci + 1) * lc, h * HEAD_V:(h + 1) * HEAD_V].astype(BF16)
            heads.append(_dot(s[h * lc:(h + 1) * lc, :], vh))
        o_intra = jnp.concatenate(heads, axis=-1)
        return o_intra, q_dec.astype(BF16), (k_inc * jnp.exp(cum_last)).astype(BF16)

    def chunk_exact(b, ci, q, k, cum, cum_last):
        e_row = lax.broadcasted_iota(jnp.int32, (D_GLA_K, D_GLA_V), 0) // HEAD_K
        e_col = lax.broadcasted_iota(jnp.int32, (D_GLA_K, D_GLA_V), 1) // HEAD_V
        expand = (e_row == e_col).astype(BF16)

        def key_row(j, o):
            kj = k_ref[b, pl.ds(ci * lc + j, 1), :]
            cj = cum_ref[b, pl.ds(ci * lc + j, 1), :]
            vj = v_ref[b, pl.ds(ci * lc + j, 1), :]
            p = q * kj * jnp.exp(jnp.minimum(cum - cj, 0.0))
            p = jnp.where(rows1 >= j, p, 0.0).astype(BF16)
            return o + _dot(p, expand) * vj

        o_intra = lax.fori_loop(0, lc, key_row, jnp.zeros((lc, D_GLA_V), F32))
        return (o_intra, (q * jnp.exp(cum)).astype(BF16),
                (k * jnp.exp(cum_last - cum)).astype(BF16))

    def gla_tile(chunk_terms):
        for ci in range(n_chunks):
            for b in range(nb):
                rows = slice(ci * lc, (ci + 1) * lc)
                q = q_ref[b, rows, :]
                k = k_ref[b, rows, :]
                cum = cum_ref[b, rows, :]
                cum_last = cum[lc - 1:lc, :]
                o_intra, qh, kh = chunk_terms(b, ci, q, k, cum, cum_last)

                st = st_ref[b]
                o_inter = _dot_nt(qh, st.astype(BF16))
                vb = v_ref[b, rows, :].astype(BF16)
                update = jnp.concatenate(
                    [_dot_tn(vb[:, h * HEAD_V:(h + 1) * HEAD_V],
                             jnp.where(klane == h, kh, jnp.zeros_like(kh)))
                     for h in range(GLA_HEADS)], axis=0)
                st_ref[b] = st * jnp.exp(cum_last) + update

                o = o_inter + o_intra
                normed = []
                for h in range(GLA_HEADS):
                    oh = o[:, h * HEAD_V:(h + 1) * HEAD_V]
                    ms = jnp.mean(oh * oh, axis=-1, keepdims=True)
                    normed.append(oh * lax.rsqrt(ms + EPS))
                on = jnp.concatenate(normed, axis=-1) * gng_ref[...]
                y_gla = on * gate_ref[b, rows, :]
                yg_ref[b, rows, :] = y_gla.astype(yg_ref.dtype)

    @pl.when(max_decay <= GLA_MAX_FACTORISED_DECAY)
    def _():
        gla_tile(chunk_factorised)

    @pl.when(max_decay > GLA_MAX_FACTORISED_DECAY)
    def _():
        gla_tile(chunk_exact)


def _mixer(u, q, k, la, v, gate, cw, cb, gng):
    b_sz, t_len, _ = u.shape
    tb = TOKENS_MIX
    assert t_len % tb == 0 and tb % GLA_CHUNK == 0 and tb % CONV_ROWS == 0

    def tok(width):
        return pl.BlockSpec((b_sz, tb, width), lambda t: (0, t, 0))

    def const(arr):
        return pl.BlockSpec(arr.shape, lambda t: (0,) * arr.ndim)

    return pl.pallas_call(
        _mixer_kernel,
        grid=(t_len // tb,),
        in_specs=[tok(D_CONV), tok(D_GLA_K), tok(D_GLA_K), tok(D_GLA_K), tok(D_GLA_V),
                  tok(D_GLA_V), const(cw), const(cb), const(gng)],
        out_specs=[tok(D_CONV), tok(D_GLA_V)],
        out_shape=[jax.ShapeDtypeStruct((b_sz, t_len, D_CONV), F32),
                   jax.ShapeDtypeStruct((b_sz, t_len, D_GLA_V), BF16)],
        scratch_shapes=[
            pltpu.VMEM((b_sz, SUBLANES, D_CONV // LANES, CONV_HALO + tb, LANES), F32),
            pltpu.VMEM((b_sz, D_GLA_V, D_GLA_K), F32),
            pltpu.VMEM((b_sz, tb, D_GLA_K), F32),
        ],
        compiler_params=pltpu.CompilerParams(
            dimension_semantics=("arbitrary",), vmem_limit_bytes=VMEM_LIMIT_BYTES),
        name="mixer",
    )(u, q, k, la, v, gate, cw, cb, gng)


def _out_ffn_kernel(x_ref, yc_ref, yg_ref, gt1_ref, sc2_ref, sh2_ref, gt2_ref, scf_ref, shf_ref,
                    gffn_ref, gfin_ref, lng_ref, lnb_ref, wout_hbm, wgu_hbm, wo_hbm, o_ref,
                    wout_ref, wgu_ref, wo_ref, stage_wide_ref, stage_ref, sem_wide, sem):
    @pl.when((pl.program_id(0) == 0) & (pl.program_id(1) == 0))
    def _():
        _stage_bf16(wout_hbm, wout_ref, stage_ref, sem)
        _stage_bf16(wgu_hbm, wgu_ref, stage_wide_ref, sem_wide)
        _stage_bf16(wo_hbm, wo_ref, stage_ref, sem)

    tm = x_ref.shape[1]
    groups = [slice(r, r + FFN_ROWS) for r in range(0, tm, FFN_ROWS)]

    def out_proj(rows):
        conv = yc_ref[0, rows, :]
        mu = jnp.mean(conv, axis=-1, keepdims=True)
        cen = conv - mu
        var = jnp.mean(cen * cen, axis=-1, keepdims=True)
        y_conv = _silu(cen * lax.rsqrt(var + EPS) * lng_ref[...] + lnb_ref[...]).astype(BF16)
        mix = (_dot(y_conv, wout_ref[0:D_CONV, :])
               + _dot(yg_ref[0, rows, :], wout_ref[D_CONV:D_CONV + D_GLA_V, :]))
        return x_ref[0, rows, :] + gt1_ref[0] * mix

    def ffn_input(x1):
        ms = jnp.mean(x1 * x1, axis=-1, keepdims=True)
        h = x1 * lax.rsqrt(ms + EPS) * gffn_ref[...]
        return (h * (1.0 + sc2_ref[0]) + sh2_ref[0]).astype(BF16)

    def ffn(x1, hb):
        acc = jnp.zeros_like(x1)
        for lo, hi in FFN_SLABS:
            gate = _dot(hb, wgu_ref[:, lo:hi])
            up = _dot(hb, wgu_ref[:, D_FF + lo:D_FF + hi])
            act = (_silu(gate) * up).astype(BF16)
            acc = acc + _dot(act, wo_ref[lo:hi, :])
        return x1 + gt2_ref[0] * acc

    def final_norm(rows, x2):
        ms = jnp.mean(x2 * x2, axis=-1, keepdims=True)
        xn = x2 * lax.rsqrt(ms + EPS) * gfin_ref[...]
        o_ref[0, rows, :] = xn * (1.0 + scf_ref[0]) + shf_ref[0]

    x1s = [out_proj(rows) for rows in groups]
    pending = None
    for rows, x1 in zip(groups, x1s):
        hb = ffn_input(x1)
        if pending is not None:
            final_norm(*pending)
        pending = (rows, ffn(x1, hb))
    final_norm(*pending)


def _out_ffn(x, yc, yg, mod, gffn, gfin, lng, lnb, wout, wgu, wo):
    b_sz, t_len, d = x.shape
    tm = TOKENS_FFN
    assert t_len % tm == 0 and tm % FFN_ROWS == 0

    def tok(width):
        return pl.BlockSpec((1, tm, width), lambda b, t: (b, t, 0))

    def const(arr):
        return pl.BlockSpec(arr.shape, lambda b, t: (0,) * arr.ndim,
                            pipeline_mode=pl.Buffered(1))

    mods = (MOD_GT1, MOD_SC2, MOD_SH2, MOD_GT2, MOD_SCF, MOD_SHF)
    in_hbm = pl.BlockSpec(memory_space=pl.ANY)
    return pl.pallas_call(
        _out_ffn_kernel,
        grid=(b_sz, t_len // tm),
        in_specs=[tok(d), tok(D_CONV), tok(D_GLA_V)] + [_mod_spec(m) for m in mods]
                 + [const(gffn), const(gfin), const(lng), const(lnb), in_hbm, in_hbm, in_hbm],
        out_specs=tok(d),
        out_shape=jax.ShapeDtypeStruct((b_sz, t_len, d), F32),
        scratch_shapes=[
            pltpu.VMEM(wout.shape, BF16),
            pltpu.VMEM(wgu.shape, BF16),
            pltpu.VMEM(wo.shape, BF16),
            pltpu.VMEM((2, STAGE_ROWS_WIDE, wgu.shape[1]), F32),
            pltpu.VMEM((2, STAGE_ROWS, d), F32),
            pltpu.SemaphoreType.DMA((2,)),
            pltpu.SemaphoreType.DMA((2,)),
        ],
        compiler_params=pltpu.CompilerParams(
            dimension_semantics=("arbitrary", "arbitrary"), vmem_limit_bytes=VMEM_LIMIT_BYTES),
        name="out_ffn",
    )(x, yc, yg, *([mod] * len(mods)), gffn, gfin, lng, lnb, wout, wgu, wo)


def kernel(x, c, w_ada, b_ada, g_mix, w_in, conv_w, conv_b, ln_g, ln_b, w_a2, b_a2,
           gla_norm_g, w_out, g_ffn, w_ffn_in, w_ffn_out, w_ada_final, b_ada_final, g_final):
    b_sz, t_len, d = x.shape
    assert w_ada.shape[0] == 1, "single-layer block"
    assert d == D_MODEL

    c_lanes = jnp.broadcast_to(c[:, :, None], (b_sz, d, LANES))
    mod = _modulation(c_lanes, w_ada[0], b_ada[0], w_ada_final, b_ada_final)

    w = jnp.concatenate(
        [w_in[0].astype(BF16), jnp.zeros((d, LANES - GATE_RANK), BF16)], axis=1)
    wa2 = jnp.pad(w_a2[0], ((0, LANES - GATE_RANK), (0, 0))).astype(BF16)

    u, q, k, v, gate, la = _in_proj(
        x, mod, g_mix[0].reshape(1, d), w, wa2, b_a2[0].reshape(1, D_GLA_K))

    cw = jnp.broadcast_to(conv_w[0][:, None, :], (CONV_KERNEL, SUBLANES, D_CONV))
    y_conv, y_gla = _mixer(u, q, k, la, v, gate, cw, conv_b[0].reshape(1, D_CONV),
                           jnp.tile(gla_norm_g[0], GLA_HEADS).reshape(1, D_GLA_V))

    return _out_ffn(
        x, y_conv, y_gla, mod, g_ffn[0].reshape(1, d), g_final.reshape(1, d),
        ln_g[0].reshape(1, D_CONV), ln_b[0].reshape(1, D_CONV),
        w_out[0], w_ffn_in[0], w_ffn_out[0])
```

```python
import functools

import jax
import jax.numpy as jnp
from jax import lax
from jax.experimental import pallas as pl
from jax.experimental.pallas import tpu as pltpu

F32 = jnp.float32
BF16 = jnp.bfloat16

D_MODEL = 1024
D_CONV = 512
CONV_KERNEL = 31
D_GLA_V = 512
D_GLA_K = 256
GLA_HEADS = 4
HEAD_K = D_GLA_K // GLA_HEADS
HEAD_V = D_GLA_V // GLA_HEADS
GATE_RANK = 16
GATE_TAU = 16.0
D_FF = 2816
EPS = 1e-6

LANES = 128
SUBLANES = 8
MXU_DIM = 256
VMEM_LIMIT_BYTES = 56 * 1024 * 1024
MOD_ROWS = 256
TOKENS_IN = 1024
IN_ROWS = 512
TOKENS_MIX = 512
TOKENS_FFN = 512
STAGE_ROWS_WIDE = 128
STAGE_ROWS = 512
CONV_HALO = 32
CONV_ROWS = 64
CONV_PARTIALS = 2
GLA_CHUNK = 128
GLA_MAX_FACTORISED_DECAY = 40.0
FFN_ROWS = 256
FFN_SLABS = ((0, 6 * MXU_DIM), (6 * MXU_DIM, D_FF))

IN_COLS = {}
_col = 0
for _name, _width in (("cv", D_CONV), ("cg", D_CONV), ("q", D_GLA_K), ("k", D_GLA_K),
                      ("v", D_GLA_V), ("og", D_GLA_V), ("a", LANES)):
    IN_COLS[_name] = (_col, _col + _width)
    _col += _width

MOD_SH1, MOD_SC1, MOD_GT1, MOD_SH2, MOD_SC2, MOD_GT2, MOD_SHF, MOD_SCF = range(8)


def _sigmoid(x):
    return 0.5 * jnp.tanh(0.5 * x) + 0.5


def _silu(x):
    h = 0.5 * x
    return h * jnp.tanh(h) + h


def _dot(a, b):
    return jnp.dot(a, b, preferred_element_type=F32)


def _dot_nt(a, b):
    return lax.dot_general(a, b, (((1,), (1,)), ((), ())), preferred_element_type=F32)


def _dot_tn(a, b):
    return lax.dot_general(a, b, (((0,), (0,)), ((), ())), preferred_element_type=F32)


def _split3(a):
    hi = a.astype(BF16)
    r1 = a - hi.astype(F32)
    mid = r1.astype(BF16)
    lo = (r1 - mid.astype(F32)).astype(BF16)
    return hi, mid, lo


def _stage_bf16(src_hbm, dst_ref, stage_ref, sem):
    n_rows, n_cols = src_hbm.shape
    chunk = stage_ref.shape[1]
    starts = list(range(0, n_rows, chunk))

    def copy(i):
        r, n = starts[i], min(chunk, n_rows - starts[i])
        return pltpu.make_async_copy(
            src_hbm.at[pl.ds(r, n), :], stage_ref.at[i % 2, pl.ds(0, n), :], sem.at[i % 2])

    copy(0).start()
    for i, r in enumerate(starts):
        if i + 1 < len(starts):
            copy(i + 1).start()
        copy(i).wait()
        n = min(chunk, n_rows - r)
        dst_ref[r:r + n, 0:n_cols] = stage_ref[i % 2, 0:n, :].astype(dst_ref.dtype)


def _modulation_kernel(c_ref, w_ref, b_ref, wf_ref, bf_ref, o_ref, acc_ref):
    j = pl.program_id(0)
    n_batch, k_rows, _ = c_ref.shape
    n = w_ref.shape[1]

    @pl.when(j == 0)
    def _():
        acc_ref[...] = jnp.zeros_like(acc_ref)

    c_act = []
    for b in range(n_batch):
        cb = c_ref[b]
        c_act.append(_silu(cb))
    for ref, offset in ((w_ref, 0), (wf_ref, n)):
        for t in range(ref.shape[1] // LANES):
            w = ref[:, t * LANES:(t + 1) * LANES]
            cols = slice(offset + t * LANES, offset + (t + 1) * LANES)
            for b in range(n_batch):
                prod = (w * c_act[b]).reshape(k_rows // SUBLANES, SUBLANES, LANES)
                acc_ref[b, :, cols] += jnp.sum(prod, axis=0)

    @pl.when(j == pl.num_programs(0) - 1)
    def _():
        bias = jnp.concatenate([b_ref[...], bf_ref[...]], axis=1)
        for b in range(n_batch):
            o_ref[b] = jnp.sum(acc_ref[b], axis=0, keepdims=True) + bias


def _modulation(c_lanes, w, b, wf, bf):
    n_batch, d, _ = c_lanes.shape
    n, nf = w.shape[1], wf.shape[1]
    assert d % MOD_ROWS == 0 and n % LANES == 0 and nf % LANES == 0
    return pl.pallas_call(
        _modulation_kernel,
        grid=(d // MOD_ROWS,),
        in_specs=[
            pl.BlockSpec((n_batch, MOD_ROWS, LANES), lambda j: (0, j, 0)),
            pl.BlockSpec((MOD_ROWS, n), lambda j: (j, 0)),
            pl.BlockSpec((1, n), lambda j: (0, 0)),
            pl.BlockSpec((MOD_ROWS, nf), lambda j: (j, 0)),
            pl.BlockSpec((1, nf), lambda j: (0, 0)),
        ],
        out_specs=pl.BlockSpec((n_batch, 1, n + nf), lambda j: (0, 0, 0)),
        out_shape=jax.ShapeDtypeStruct((n_batch, 1, n + nf), F32),
        scratch_shapes=[pltpu.VMEM((n_batch, SUBLANES, n + nf), F32)],
        compiler_params=pltpu.CompilerParams(
            dimension_semantics=("arbitrary",), vmem_limit_bytes=VMEM_LIMIT_BYTES),
        name="modulation",
    )(c_lanes, w, b.reshape(1, n), wf, bf.reshape(1, nf))


def _mod_spec(which):
    return pl.BlockSpec((1, 1, D_MODEL), lambda b, t: (b, 0, which))


def _in_proj_kernel(x_ref, sc_ref, sh_ref, g_ref, w_ref, wa2_ref, ba2_ref,
                    u_ref, q_ref, k_ref, v_ref, gate_ref, la_ref):
    tm = x_ref.shape[1]
    groups = [slice(r, r + IN_ROWS) for r in range(0, tm, IN_ROWS)]

    def normed(rows):
        x = x_ref[0, rows, :]
        ms = jnp.mean(x * x, axis=-1, keepdims=True)
        h = x * lax.rsqrt(ms + EPS) * g_ref[...]
        return (h * (1.0 + sc_ref[0]) + sh_ref[0]).astype(BF16)

    def epilogue(rows, proj):
        def piece(name):
            lo, hi = IN_COLS[name]
            return proj[:, lo:hi]

        u_ref[0, rows, :] = piece("cv") * _sigmoid(piece("cg"))
        q_ref[0, rows, :] = piece("q") * (HEAD_K ** -0.5)
        k_ref[0, rows, :] = piece("k")
        v_ref[0, rows, :] = piece("v")
        gate_ref[0, rows, :] = _silu(piece("og"))

        a_low = piece("a")
        z = _dot(a_low.astype(BF16), wa2_ref[...]) + ba2_ref[...]
        log_sig = jnp.minimum(z, 0.0) - jnp.log(1.0 + jnp.exp(-jnp.abs(z)))
        la_ref[0, rows, :] = log_sig * (1.0 / GATE_TAU)

    pending = None
    for rows in groups:
        proj = _dot(normed(rows), w_ref[...])
        if pending is not None:
            epilogue(*pending)
        pending = (rows, proj)
    epilogue(*pending)


def _in_proj(x, mod, g, w, wa2, ba2):
    b_sz, t_len, d = x.shape
    tm = TOKENS_IN
    assert t_len % tm == 0

    def tok(width):
        return pl.BlockSpec((1, tm, width), lambda b, t: (b, t, 0))

    def const(arr):
        return pl.BlockSpec(arr.shape, lambda b, t: (0,) * arr.ndim)

    def out(width):
        return jax.ShapeDtypeStruct((b_sz, t_len, width), F32)

    return pl.pallas_call(
        _in_proj_kernel,
        grid=(b_sz, t_len // tm),
        in_specs=[tok(d), _mod_spec(MOD_SC1), _mod_spec(MOD_SH1), const(g), const(w),
                  const(wa2), const(ba2)],
        out_specs=[tok(D_CONV), tok(D_GLA_K), tok(D_GLA_K), tok(D_GLA_V), tok(D_GLA_V),
                   tok(D_GLA_K)],
        out_shape=[out(D_CONV), out(D_GLA_K), out(D_GLA_K), out(D_GLA_V), out(D_GLA_V),
                   out(D_GLA_K)],
        compiler_params=pltpu.CompilerParams(
            dimension_semantics=("arbitrary", "arbitrary"), vmem_limit_bytes=VMEM_LIMIT_BYTES),
        name="in_proj",
    )(x, mod, mod, g, w, wa2, ba2)


def _mixer_kernel(u_ref, q_ref, k_ref, la_ref, v_ref, gate_ref, cw_ref, cb_ref, gng_ref,
                  yc_ref, yg_ref, win_ref, st_ref, cum_ref):
    nb, tb = u_ref.shape[0], u_ref.shape[1]
    lc = GLA_CHUNK

    @pl.when(pl.program_id(0) == 0)
    def _():
        win_ref[...] = jnp.zeros_like(win_ref)
        st_ref[...] = jnp.zeros_like(st_ref)

    for b in range(nb):
        for c in range(D_CONV // LANES):
            u_lanes = u_ref[b, :, c * LANES:(c + 1) * LANES]
            for phase in range(SUBLANES):
                win_ref[b, phase, c, CONV_HALO - phase:CONV_HALO - phase + tb, :] = u_lanes
    first_tap = CONV_HALO - (CONV_KERNEL - 1)

    for c in range(D_CONV // LANES):
        lanes = slice(c * LANES, (c + 1) * LANES)
        taps = []
        for phase in range(SUBLANES):
            for base in range(0, CONV_HALO + SUBLANES, SUBLANES):
                tap = base + phase - first_tap
                if 0 <= tap < CONV_KERNEL:
                    taps.append((cw_ref[tap, :, lanes], phase, base))
        bias = cb_ref[:, lanes]

        def conv_tile(i, carry, c=c, lanes=lanes, taps=taps, bias=bias):
            r0 = pl.multiple_of(i * CONV_ROWS, CONV_ROWS)
            for b in range(nb):
                parts = [jnp.broadcast_to(bias, (CONV_ROWS, LANES))] + [None] * (CONV_PARTIALS - 1)
                for n, (w8, phase, base) in enumerate(taps):
                    w_rows = jnp.concatenate([w8] * (CONV_ROWS // SUBLANES), axis=0)
                    term = w_rows * win_ref[b, phase, c, pl.ds(r0 + base, CONV_ROWS), :]
                    p = n % CONV_PARTIALS
                    parts[p] = term if parts[p] is None else parts[p] + term
                yc_ref[b, pl.ds(r0, CONV_ROWS), lanes] = functools.reduce(lambda x, y: x + y, parts)
            return carry

        lax.fori_loop(0, tb // CONV_ROWS, conv_tile, 0)

    win_ref[:, :, :, 0:CONV_HALO, :] = win_ref[:, :, :, tb:tb + CONV_HALO, :]

    n_chunks = tb // lc
    row = lax.broadcasted_iota(jnp.int32, (lc, lc), 0)
    col = lax.broadcasted_iota(jnp.int32, (lc, lc), 1)
    tril = (col <= row).astype(BF16)
    klane = lax.broadcasted_iota(jnp.int32, (1, D_GLA_K), 1) // HEAD_K
    srow = lax.broadcasted_iota(jnp.int32, (GLA_HEADS * lc, lc), 0)
    scol = lax.broadcasted_iota(jnp.int32, (GLA_HEADS * lc, lc), 1)
    causal = scol <= (srow % lc)
    rows1 = lax.broadcasted_iota(jnp.int32, (lc, 1), 0)

    decay = None
    for b in range(nb):
        for ci in range(n_chunks):
            hi, mid, lo = _split3(la_ref[b, ci * lc:(ci + 1) * lc, :])
            cum = _dot(tril, hi) + _dot(tril, mid) + _dot(tril, lo)
            cum_ref[b, ci * lc:(ci + 1) * lc, :] = cum
            total = -cum[lc - 1:lc, :]
            decay = total if decay is None else jnp.maximum(decay, total)
    max_decay = jnp.max(decay)

    def chunk_factorised(b, ci, q, k, cum, cum_last):
        ref = cum[lc // 2 - 1:lc // 2, :]
        q_dec = q * jnp.exp(cum)
        k_inc = k * jnp.exp(-cum)
        qt = (q_dec * jnp.exp(-ref)).astype(BF16)
        kt = (k_inc * jnp.exp(ref)).astype(BF16)
        qbd = jnp.concatenate(
            [jnp.where(klane == h, qt, jnp.zeros_like(qt)) for h in range(GLA_HEADS)], axis=0)
        s = _dot_nt(qbd, kt)
        s = jnp.where(causal, s, 0.0).astype(BF16)
        heads = []
        for h in range(GLA_HEADS):
            vh = v_ref[b, ci * lc:(ci + 1) * lc, h * HEAD_V:(h + 1) * HEAD_V].astype(BF16)
            heads.append(_dot(s[h * lc:(h + 1) * lc, :], vh))
        o_intra = jnp.concatenate(heads, axis=-1)
        return o_intra, q_dec.astype(BF16), (k_inc * jnp.exp(cum_last)).astype(BF16)

    def chunk_exact(b, ci, q, k, cum, cum_last):
        e_row = lax.broadcasted_iota(jnp.int32, (D_GLA_K, D_GLA_V), 0) // HEAD_K
        e_col = lax.broadcasted_iota(jnp.int32, (D_GLA_K, D_GLA_V), 1) // HEAD_V
        expand = (e_row == e_col).astype(BF16)

        def key_row(j, o):
            kj = k_ref[b, pl.ds(ci * lc + j, 1), :]
            cj = cum_ref[b, pl.ds(ci * lc + j, 1), :]
            vj = v_ref[b, pl.ds(ci * lc + j, 1), :]
            p = q * kj * jnp.exp(jnp.minimum(cum - cj, 0.0))
            p = jnp.where(rows1 >= j, p, 0.0).astype(BF16)
            return o + _dot(p, expand) * vj

        o_intra = lax.fori_loop(0, lc, key_row, jnp.zeros((lc, D_GLA_V), F32))
        return (o_intra, (q * jnp.exp(cum)).astype(BF16),
                (k * jnp.exp(cum_last - cum)).astype(BF16))

    def gla_tile(chunk_terms):
        for ci in range(n_chunks):
            for b in range(nb):
                rows = slice(ci * lc, (ci + 1) * lc)
                q = q_ref[b, rows, :]
                k = k_ref[b, rows, :]
                cum = cum_ref[b, rows, :]
                cum_last = cum[lc - 1:lc, :]
                o_intra, qh, kh = chunk_terms(b, ci, q, k, cum, cum_last)

                st = st_ref[b]
                o_inter = _dot_nt(qh, st.astype(BF16))
                vb = v_ref[b, rows, :].astype(BF16)
                update = jnp.concatenate(
                    [_dot_tn(vb[:, h * HEAD_V:(h + 1) * HEAD_V],
                             jnp.where(klane == h, kh, jnp.zeros_like(kh)))
                     for h in range(GLA_HEADS)], axis=0)
                st_ref[b] = st * jnp.exp(cum_last) + update

                o = o_inter + o_intra
                normed = []
                for h in range(GLA_HEADS):
                    oh = o[:, h * HEAD_V:(h + 1) * HEAD_V]
                    ms = jnp.mean(oh * oh, axis=-1, keepdims=True)
                    normed.append(oh * lax.rsqrt(ms + EPS) * gng_ref[...])
                y_gla = jnp.concatenate(normed, axis=-1) * gate_ref[b, rows, :]
                yg_ref[b, rows, :] = y_gla.astype(yg_ref.dtype)

    @pl.when(max_decay <= GLA_MAX_FACTORISED_DECAY)
    def _():
        gla_tile(chunk_factorised)

    @pl.when(max_decay > GLA_MAX_FACTORISED_DECAY)
    def _():
        gla_tile(chunk_exact)


def _mixer(u, q, k, la, v, gate, cw, cb, gng):
    b_sz, t_len, _ = u.shape
    tb = TOKENS_MIX
    assert t_len % tb == 0 and tb % GLA_CHUNK == 0 and tb % CONV_ROWS == 0

    def tok(width):
        return pl.BlockSpec((b_sz, tb, width), lambda t: (0, t, 0))

    def const(arr):
        return pl.BlockSpec(arr.shape, lambda t: (0,) * arr.ndim)

    return pl.pallas_call(
        _mixer_kernel,
        grid=(t_len // tb,),
        in_specs=[tok(D_CONV), tok(D_GLA_K), tok(D_GLA_K), tok(D_GLA_K), tok(D_GLA_V),
                  tok(D_GLA_V), const(cw), const(cb), const(gng)],
        out_specs=[tok(D_CONV), tok(D_GLA_V)],
        out_shape=[jax.ShapeDtypeStruct((b_sz, t_len, D_CONV), F32),
                   jax.ShapeDtypeStruct((b_sz, t_len, D_GLA_V), BF16)],
        scratch_shapes=[
            pltpu.VMEM((b_sz, SUBLANES, D_CONV // LANES, CONV_HALO + tb, LANES), F32),
            pltpu.VMEM((b_sz, D_GLA_V, D_GLA_K), F32),
            pltpu.VMEM((b_sz, tb, D_GLA_K), F32),
        ],
        compiler_params=pltpu.CompilerParams(
            dimension_semantics=("arbitrary",), vmem_limit_bytes=VMEM_LIMIT_BYTES),
        name="mixer",
    )(u, q, k, la, v, gate, cw, cb, gng)


def _out_ffn_kernel(x_ref, yc_ref, yg_ref, gt1_ref, sc2_ref, sh2_ref, gt2_ref, scf_ref, shf_ref,
                    gffn_ref, gfin_ref, lng_ref, lnb_ref, wout_hbm, wgu_hbm, wo_hbm, o_ref,
                    wout_ref, wgu_ref, wo_ref, stage_wide_ref, stage_ref, sem_wide, sem):
    @pl.when((pl.program_id(0) == 0) & (pl.program_id(1) == 0))
    def _():
        _stage_bf16(wout_hbm, wout_ref, stage_ref, sem)
        _stage_bf16(wgu_hbm, wgu_ref, stage_wide_ref, sem_wide)
        _stage_bf16(wo_hbm, wo_ref, stage_ref, sem)

    tm = x_ref.shape[1]
    groups = [slice(r, r + FFN_ROWS) for r in range(0, tm, FFN_ROWS)]

    def out_proj(rows):
        conv = yc_ref[0, rows, :]
        mu = jnp.mean(conv, axis=-1, keepdims=True)
        cen = conv - mu
        var = jnp.mean(cen * cen, axis=-1, keepdims=True)
        y_conv = _silu(cen * lax.rsqrt(var + EPS) * lng_ref[...] + lnb_ref[...]).astype(BF16)
        mix = (_dot(y_conv, wout_ref[0:D_CONV, :])
               + _dot(yg_ref[0, rows, :], wout_ref[D_CONV:D_CONV + D_GLA_V, :]))
        return x_ref[0, rows, :] + gt1_ref[0] * mix

    def ffn_input(x1):
        ms = jnp.mean(x1 * x1, axis=-1, keepdims=True)
        h = x1 * lax.rsqrt(ms + EPS) * gffn_ref[...]
        return (h * (1.0 + sc2_ref[0]) + sh2_ref[0]).astype(BF16)

    def ffn(x1, hb):
        acc = jnp.zeros_like(x1)
        for lo, hi in FFN_SLABS:
            gate = _dot(hb, wgu_ref[:, lo:hi])
            up = _dot(hb, wgu_ref[:, D_FF + lo:D_FF + hi])
            act = (_silu(gate) * up).astype(BF16)
            acc = acc + _dot(act, wo_ref[lo:hi, :])
        return x1 + gt2_ref[0] * acc

    def final_norm(rows, x2):
        ms = jnp.mean(x2 * x2, axis=-1, keepdims=True)
        xn = x2 * lax.rsqrt(ms + EPS) * gfin_ref[...]
        o_ref[0, rows, :] = xn * (1.0 + scf_ref[0]) + shf_ref[0]

    x1s = [out_proj(rows) for rows in groups]
    pending = None
    for rows, x1 in zip(groups, x1s):
        hb = ffn_input(x1)
        if pending is not None:
            final_norm(*pending)
        pending = (rows, ffn(x1, hb))
    final_norm(*pending)


def _out_ffn(x, yc, yg, mod, gffn, gfin, lng, lnb, wout, wgu, wo):
    b_sz, t_len, d = x.shape
    tm = TOKENS_FFN
    assert t_len % tm == 0 and tm % FFN_ROWS == 0

    def tok(width):
        return pl.BlockSpec((1, tm, width), lambda b, t: (b, t, 0))

    def const(arr):
        return pl.BlockSpec(arr.shape, lambda b, t: (0,) * arr.ndim,
                            pipeline_mode=pl.Buffered(1))

    mods = (MOD_GT1, MOD_SC2, MOD_SH2, MOD_GT2, MOD_SCF, MOD_SHF)
    in_hbm = pl.BlockSpec(memory_space=pl.ANY)
    return pl.pallas_call(
        _out_ffn_kernel,
        grid=(b_sz, t_len // tm),
        in_specs=[tok(d), tok(D_CONV), tok(D_GLA_V)] + [_mod_spec(m) for m in mods]
                 + [const(gffn), const(gfin), const(lng), const(lnb), in_hbm, in_hbm, in_hbm],
        out_specs=tok(d),
        out_shape=jax.ShapeDtypeStruct((b_sz, t_len, d), F32),
        scratch_shapes=[
            pltpu.VMEM(wout.shape, BF16),
            pltpu.VMEM(wgu.shape, BF16),
            pltpu.VMEM(wo.shape, BF16),
            pltpu.VMEM((2, STAGE_ROWS_WIDE, wgu.shape[1]), F32),
            pltpu.VMEM((2, STAGE_ROWS, d), F32),
            pltpu.SemaphoreType.DMA((2,)),
            pltpu.SemaphoreType.DMA((2,)),
        ],
        compiler_params=pltpu.CompilerParams(
            dimension_semantics=("arbitrary", "arbitrary"), vmem_limit_bytes=VMEM_LIMIT_BYTES),
        name="out_ffn",
    )(x, yc, yg, *([mod] * len(mods)), gffn, gfin, lng, lnb, wout, wgu, wo)


def kernel(x, c, w_ada, b_ada, g_mix, w_in, conv_w, conv_b, ln_g, ln_b, w_a2, b_a2,
           gla_norm_g, w_out, g_ffn, w_ffn_in, w_ffn_out, w_ada_final, b_ada_final, g_final):
    b_sz, t_len, d = x.shape
    assert w_ada.shape[0] == 1, "single-layer block"
    assert d == D_MODEL

    c_lanes = jnp.broadcast_to(c[:, :, None], (b_sz, d, LANES))
    mod = _modulation(c_lanes, w_ada[0], b_ada[0], w_ada_final, b_ada_final)

    w = jnp.concatenate(
        [w_in[0].astype(BF16), jnp.zeros((d, LANES - GATE_RANK), BF16)], axis=1)
    wa2 = jnp.pad(w_a2[0], ((0, LANES - GATE_RANK), (0, 0))).astype(BF16)

    u, q, k, v, gate, la = _in_proj(
        x, mod, g_mix[0].reshape(1, d), w, wa2, b_a2[0].reshape(1, D_GLA_K))

    cw = jnp.broadcast_to(conv_w[0][:, None, :], (CONV_KERNEL, SUBLANES, D_CONV))
    y_conv, y_gla = _mixer(u, q, k, la, v, gate, cw, conv_b[0].reshape(1, D_CONV),
                           gla_norm_g[0].reshape(1, HEAD_V))

    return _out_ffn(
        x, y_conv, y_gla, mod, g_ffn[0].reshape(1, d), g_final.reshape(1, d),
        ln_g[0].reshape(1, D_CONV), ln_b[0].reshape(1, D_CONV),
        w_out[0], w_ffn_in[0], w_ffn_out[0])
```

```python
import functools

import jax
import jax.numpy as jnp
from jax import lax
from jax.experimental import pallas as pl
from jax.experimental.pallas import tpu as pltpu

F32 = jnp.float32
BF16 = jnp.bfloat16

D_MODEL = 1024
D_CONV = 512
CONV_KERNEL = 31
D_GLA_V = 512
D_GLA_K = 256
GLA_HEADS = 4
HEAD_K = D_GLA_K // GLA_HEADS
HEAD_V = D_GLA_V // GLA_HEADS
GATE_RANK = 16
GATE_TAU = 16.0
D_FF = 2816
EPS = 1e-6

LANES = 128
SUBLANES = 8
MXU_DIM = 256
VMEM_LIMIT_BYTES = 56 * 1024 * 1024
MOD_ROWS = 256
TOKENS_IN = 1024
IN_ROWS = 512
TOKENS_MIX = 512
TOKENS_FFN = 512
CONV_HALO = 32
CONV_ROWS = 64
CONV_PARTIALS = 2
GLA_CHUNK = 128
GLA_MAX_FACTORISED_DECAY = 40.0
FFN_ROWS = 256
FFN_SLABS = ((0, 6 * MXU_DIM), (6 * MXU_DIM, D_FF))

IN_COLS = {}
_col = 0
for _name, _width in (("cv", D_CONV), ("cg", D_CONV), ("q", D_GLA_K), ("k", D_GLA_K),
                      ("v", D_GLA_V), ("og", D_GLA_V), ("a", LANES)):
    IN_COLS[_name] = (_col, _col + _width)
    _col += _width

MOD_SH1, MOD_SC1, MOD_GT1, MOD_SH2, MOD_SC2, MOD_GT2, MOD_SHF, MOD_SCF = range(8)


def _sigmoid(x):
    return 0.5 * jnp.tanh(0.5 * x) + 0.5


def _silu(x):
    h = 0.5 * x
    return h * jnp.tanh(h) + h


def _dot(a, b):
    return jnp.dot(a, b, preferred_element_type=F32)


def _dot_nt(a, b):
    return lax.dot_general(a, b, (((1,), (1,)), ((), ())), preferred_element_type=F32)


def _dot_tn(a, b):
    return lax.dot_general(a, b, (((0,), (0,)), ((), ())), preferred_element_type=F32)


def _split3(a):
    hi = a.astype(BF16)
    r1 = a - hi.astype(F32)
    mid = r1.astype(BF16)
    lo = (r1 - mid.astype(F32)).astype(BF16)
    return hi, mid, lo


def _modulation_kernel(c_ref, w_ref, b_ref, wf_ref, bf_ref, o_ref, acc_ref):
    j = pl.program_id(0)
    n_batch, k_rows, _ = c_ref.shape
    n = w_ref.shape[1]

    @pl.when(j == 0)
    def _():
        acc_ref[...] = jnp.zeros_like(acc_ref)

    c_act = []
    for b in range(n_batch):
        cb = c_ref[b]
        c_act.append(_silu(cb))
    for ref, offset in ((w_ref, 0), (wf_ref, n)):
        for t in range(ref.shape[1] // LANES):
            w = ref[:, t * LANES:(t + 1) * LANES]
            cols = slice(offset + t * LANES, offset + (t + 1) * LANES)
            for b in range(n_batch):
                prod = (w * c_act[b]).reshape(k_rows // SUBLANES, SUBLANES, LANES)
                acc_ref[b, :, cols] += jnp.sum(prod, axis=0)

    @pl.when(j == pl.num_programs(0) - 1)
    def _():
        bias = jnp.concatenate([b_ref[...], bf_ref[...]], axis=1)
        for b in range(n_batch):
            o_ref[b] = jnp.sum(acc_ref[b], axis=0, keepdims=True) + bias


def _modulation(c_lanes, w, b, wf, bf):
    n_batch, d, _ = c_lanes.shape
    n, nf = w.shape[1], wf.shape[1]
    assert d % MOD_ROWS == 0 and n % LANES == 0 and nf % LANES == 0
    return pl.pallas_call(
        _modulation_kernel,
        grid=(d // MOD_ROWS,),
        in_specs=[
            pl.BlockSpec((n_batch, MOD_ROWS, LANES), lambda j: (0, j, 0)),
            pl.BlockSpec((MOD_ROWS, n), lambda j: (j, 0)),
            pl.BlockSpec((1, n), lambda j: (0, 0)),
            pl.BlockSpec((MOD_ROWS, nf), lambda j: (j, 0)),
            pl.BlockSpec((1, nf), lambda j: (0, 0)),
        ],
        out_specs=pl.BlockSpec((n_batch, 1, n + nf), lambda j: (0, 0, 0)),
        out_shape=jax.ShapeDtypeStruct((n_batch, 1, n + nf), F32),
        scratch_shapes=[pltpu.VMEM((n_batch, SUBLANES, n + nf), F32)],
        compiler_params=pltpu.CompilerParams(
            dimension_semantics=("arbitrary",), vmem_limit_bytes=VMEM_LIMIT_BYTES),
        name="modulation",
    )(c_lanes, w, b.reshape(1, n), wf, bf.reshape(1, nf))


def _mod_spec(which):
    return pl.BlockSpec((1, 1, D_MODEL), lambda b, t: (b, 0, which))


def _in_proj_kernel(x_ref, sc_ref, sh_ref, g_ref, w_ref, wa2_ref, ba2_ref,
                    wout_f32, wgu_f32, wo_f32,
                    u_ref, q_ref, k_ref, v_ref, gate_ref, la_ref,
                    wout_bf16, wgu_bf16, wo_bf16):
    def round_weight_blocks():
        for src, dst in ((wout_f32, wout_bf16), (wgu_f32, wgu_bf16), (wo_f32, wo_bf16)):
            dst[...] = src[...].astype(dst.dtype)

    tm = x_ref.shape[1]
    groups = [slice(r, r + IN_ROWS) for r in range(0, tm, IN_ROWS)]

    def normed(rows):
        x = x_ref[0, rows, :]
        ms = jnp.mean(x * x, axis=-1, keepdims=True)
        h = x * lax.rsqrt(ms + EPS) * g_ref[...]
        return (h * (1.0 + sc_ref[0]) + sh_ref[0]).astype(BF16)

    def epilogue(rows, proj):
        def piece(name):
            lo, hi = IN_COLS[name]
            return proj[:, lo:hi]

        u_ref[0, rows, :] = piece("cv") * _sigmoid(piece("cg"))
        q_ref[0, rows, :] = piece("q") * (HEAD_K ** -0.5)
        k_ref[0, rows, :] = piece("k")
        v_ref[0, rows, :] = piece("v")
        gate_ref[0, rows, :] = _silu(piece("og"))

        a_low = piece("a")
        z = _dot(a_low.astype(BF16), wa2_ref[...]) + ba2_ref[...]
        log_sig = jnp.minimum(z, 0.0) - jnp.log(1.0 + jnp.exp(-jnp.abs(z)))
        la_ref[0, rows, :] = log_sig * (1.0 / GATE_TAU)

    pending = None
    for rows in groups:
        proj = _dot(normed(rows), w_ref[...])
        if pending is None:
            round_weight_blocks()
        else:
            epilogue(*pending)
        pending = (rows, proj)
    epilogue(*pending)


def _in_proj(x, mod, g, w, wa2, ba2, later_weights):
    b_sz, t_len, d = x.shape
    tm = TOKENS_IN
    assert t_len % tm == 0
    n_t = t_len // tm
    n_steps = b_sz * n_t
    bf16_rows = 2 * SUBLANES
    assert all(a.shape[0] % (n_steps * bf16_rows) == 0 for a in later_weights)

    def row_block(arr):
        return pl.BlockSpec((arr.shape[0] // n_steps, arr.shape[1]),
                            lambda b, t: (b * n_t + t, 0))

    def tok(width):
        return pl.BlockSpec((1, tm, width), lambda b, t: (b, t, 0))

    def const(arr):
        return pl.BlockSpec(arr.shape, lambda b, t: (0,) * arr.ndim)

    def out(width):
        return jax.ShapeDtypeStruct((b_sz, t_len, width), F32)

    return pl.pallas_call(
        _in_proj_kernel,
        grid=(b_sz, t_len // tm),
        in_specs=[tok(d), _mod_spec(MOD_SC1), _mod_spec(MOD_SH1), const(g), const(w),
                  const(wa2), const(ba2)] + [row_block(a) for a in later_weights],
        out_specs=[tok(D_CONV), tok(D_GLA_K), tok(D_GLA_K), tok(D_GLA_V), tok(D_GLA_V),
                   tok(D_GLA_K)] + [row_block(a) for a in later_weights],
        out_shape=[out(D_CONV), out(D_GLA_K), out(D_GLA_K), out(D_GLA_V), out(D_GLA_V),
                   out(D_GLA_K)]
                  + [jax.ShapeDtypeStruct(a.shape, BF16) for a in later_weights],
        compiler_params=pltpu.CompilerParams(
            dimension_semantics=("arbitrary", "arbitrary"), vmem_limit_bytes=VMEM_LIMIT_BYTES),
        name="in_proj",
    )(x, mod, mod, g, w, wa2, ba2, *later_weights)


def _mixer_kernel(u_ref, q_ref, k_ref, la_ref, v_ref, gate_ref, cw_ref, cb_ref, gng_ref,
                  yc_ref, yg_ref, win_ref, st_ref, cum_ref):
    nb, tb = u_ref.shape[0], u_ref.shape[1]
    lc = GLA_CHUNK

    @pl.when(pl.program_id(0) == 0)
    def _():
        win_ref[...] = jnp.zeros_like(win_ref)
        st_ref[...] = jnp.zeros_like(st_ref)

    for b in range(nb):
        for c in range(D_CONV // LANES):
            u_lanes = u_ref[b, :, c * LANES:(c + 1) * LANES]
            for phase in range(SUBLANES):
                win_ref[b, phase, c, CONV_HALO - phase:CONV_HALO - phase + tb, :] = u_lanes
    first_tap = CONV_HALO - (CONV_KERNEL - 1)

    for c in range(D_CONV // LANES):
        lanes = slice(c * LANES, (c + 1) * LANES)
        taps = []
        for phase in range(SUBLANES):
            for base in range(0, CONV_HALO + SUBLANES, SUBLANES):
                tap = base + phase - first_tap
                if 0 <= tap < CONV_KERNEL:
                    taps.append((cw_ref[tap, :, lanes], phase, base))
        bias = cb_ref[:, lanes]

        def conv_tile(i, carry, c=c, lanes=lanes, taps=taps, bias=bias):
            r0 = pl.multiple_of(i * CONV_ROWS, CONV_ROWS)
            for b in range(nb):
                parts = [jnp.broadcast_to(bias, (CONV_ROWS, LANES))] + [None] * (CONV_PARTIALS - 1)
                for n, (w8, phase, base) in enumerate(taps):
                    w_rows = jnp.concatenate([w8] * (CONV_ROWS // SUBLANES), axis=0)
                    term = w_rows * win_ref[b, phase, c, pl.ds(r0 + base, CONV_ROWS), :]
                    p = n % CONV_PARTIALS
                    parts[p] = term if parts[p] is None else parts[p] + term
                yc_ref[b, pl.ds(r0, CONV_ROWS), lanes] = functools.reduce(lambda x, y: x + y, parts)
            return carry

        lax.fori_loop(0, tb // CONV_ROWS, conv_tile, 0)

    win_ref[:, :, :, 0:CONV_HALO, :] = win_ref[:, :, :, tb:tb + CONV_HALO, :]

    n_chunks = tb // lc
    row = lax.broadcasted_iota(jnp.int32, (lc, lc), 0)
    col = lax.broadcasted_iota(jnp.int32, (lc, lc), 1)
    tril = (col <= row).astype(BF16)
    klane = lax.broadcasted_iota(jnp.int32, (1, D_GLA_K), 1) // HEAD_K
    srow = lax.broadcasted_iota(jnp.int32, (GLA_HEADS * lc, lc), 0)
    scol = lax.broadcasted_iota(jnp.int32, (GLA_HEADS * lc, lc), 1)
    causal = scol <= (srow % lc)
    rows1 = lax.broadcasted_iota(jnp.int32, (lc, 1), 0)

    decay = None
    for b in range(nb):
        for ci in range(n_chunks):
            hi, mid, lo = _split3(la_ref[b, ci * lc:(ci + 1) * lc, :])
            cum = _dot(tril, hi) + _dot(tril, mid) + _dot(tril, lo)
            cum_ref[b, ci * lc:(ci + 1) * lc, :] = cum
            total = -cum[lc - 1:lc, :]
            decay = total if decay is None else jnp.maximum(decay, total)
    max_decay = jnp.max(decay)

    def chunk_factorised(b, ci, q, k, cum, cum_last):
        ref = cum[lc // 2 - 1:lc // 2, :]
        q_dec = q * jnp.exp(cum)
        k_inc = k * jnp.exp(-cum)
        qt = (q_dec * jnp.exp(-ref)).astype(BF16)
        kt = (k_inc * jnp.exp(ref)).astype(BF16)
        qbd = jnp.concatenate(
            [jnp.where(klane == h, qt, jnp.zeros_like(qt)) for h in range(GLA_HEADS)], axis=0)
        s = _dot_nt(qbd, kt)
        s = jnp.where(causal, s, 0.0).astype(BF16)
        heads = []
        for h in range(GLA_HEADS):
            vh = v_ref[b, ci * lc:(ci + 1) * lc, h * HEAD_V:(h + 1) * HEAD_V].astype(BF16)
            heads.append(_dot(s[h * lc:(h + 1) * lc, :], vh))
        o_intra = jnp.concatenate(heads, axis=-1)
        return o_intra, q_dec.astype(BF16), (k_inc * jnp.exp(cum_last)).astype(BF16)

    def chunk_exact(b, ci, q, k, cum, cum_last):
        e_row = lax.broadcasted_iota(jnp.int32, (D_GLA_K, D_GLA_V), 0) // HEAD_K
        e_col = lax.broadcasted_iota(jnp.int32, (D_GLA_K, D_GLA_V), 1) // HEAD_V
        expand = (e_row == e_col).astype(BF16)

        def key_row(j, o):
            kj = k_ref[b, pl.ds(ci * lc + j, 1), :]
            cj = cum_ref[b, pl.ds(ci * lc + j, 1), :]
            vj = v_ref[b, pl.ds(ci * lc + j, 1), :]
            p = q * kj * jnp.exp(jnp.minimum(cum - cj, 0.0))
            p = jnp.where(rows1 >= j, p, 0.0).astype(BF16)
            return o + _dot(p, expand) * vj

        o_intra = lax.fori_loop(0, lc, key_row, jnp.zeros((lc, D_GLA_V), F32))
        return (o_intra, (q * jnp.exp(cum)).astype(BF16),
                (k * jnp.exp(cum_last - cum)).astype(BF16))

    def gla_tile(chunk_terms):
        for ci in range(n_chunks):
            for b in range(nb):
                rows = slice(ci * lc, (ci + 1) * lc)
                q = q_ref[b, rows, :]
                k = k_ref[b, rows, :]
                cum = cum_ref[b, rows, :]
                cum_last = cum[lc - 1:lc, :]
                o_intra, qh, kh = chunk_terms(b, ci, q, k, cum, cum_last)

                st = st_ref[b]
                o_inter = _dot_nt(qh, st.astype(BF16))
                vb = v_ref[b, rows, :].astype(BF16)
                update = jnp.concatenate(
                    [_dot_tn(vb[:, h * HEAD_V:(h + 1) * HEAD_V],
                             jnp.where(klane == h, kh, jnp.zeros_like(kh)))
                     for h in range(GLA_HEADS)], axis=0)
                st_ref[b] = st * jnp.exp(cum_last) + update

                o = o_inter + o_intra
                normed = []
                for h in range(GLA_HEADS):
                    oh = o[:, h * HEAD_V:(h + 1) * HEAD_V]
                    ms = jnp.mean(oh * oh, axis=-1, keepdims=True)
                    normed.append(oh * lax.rsqrt(ms + EPS) * gng_ref[...])
                y_gla = jnp.concatenate(normed, axis=-1) * gate_ref[b, rows, :]
                yg_ref[b, rows, :] = y_gla.astype(yg_ref.dtype)

    @pl.when(max_decay <= GLA_MAX_FACTORISED_DECAY)
    def _():
        gla_tile(chunk_factorised)

    @pl.when(max_decay > GLA_MAX_FACTORISED_DECAY)
    def _():
        gla_tile(chunk_exact)


def _mixer(u, q, k, la, v, gate, cw, cb, gng):
    b_sz, t_len, _ = u.shape
    tb = TOKENS_MIX
    assert t_len % tb == 0 and tb % GLA_CHUNK == 0 and tb % CONV_ROWS == 0

    def tok(width):
        return pl.BlockSpec((b_sz, tb, width), lambda t: (0, t, 0))

    def const(arr):
        return pl.BlockSpec(arr.shape, lambda t: (0,) * arr.ndim)

    return pl.pallas_call(
        _mixer_kernel,
        grid=(t_len // tb,),
        in_specs=[tok(D_CONV), tok(D_GLA_K), tok(D_GLA_K), tok(D_GLA_K), tok(D_GLA_V),
                  tok(D_GLA_V), const(cw), const(cb), const(gng)],
        out_specs=[tok(D_CONV), tok(D_GLA_V)],
        out_shape=[jax.ShapeDtypeStruct((b_sz, t_len, D_CONV), F32),
                   jax.ShapeDtypeStruct((b_sz, t_len, D_GLA_V), BF16)],
        scratch_shapes=[
            pltpu.VMEM((b_sz, SUBLANES, D_CONV // LANES, CONV_HALO + tb, LANES), F32),
            pltpu.VMEM((b_sz, D_GLA_V, D_GLA_K), F32),
            pltpu.VMEM((b_sz, tb, D_GLA_K), F32),
        ],
        compiler_params=pltpu.CompilerParams(
            dimension_semantics=("arbitrary",), vmem_limit_bytes=VMEM_LIMIT_BYTES),
        name="mixer",
    )(u, q, k, la, v, gate, cw, cb, gng)


def _out_ffn_kernel(x_ref, yc_ref, yg_ref, gt1_ref, sc2_ref, sh2_ref, gt2_ref, scf_ref, shf_ref,
                    gffn_ref, gfin_ref, lng_ref, lnb_ref, wout_hbm, wgu_hbm, wo_hbm, o_ref,
                    wout_ref, wgu_ref, wo_ref, sem):
    is_first = (pl.program_id(0) == 0) & (pl.program_id(1) == 0)

    @pl.when(is_first)
    def _():
        copies = [pltpu.make_async_copy(src, dst, sem.at[i]) for i, (src, dst) in enumerate(
            ((wout_hbm, wout_ref), (wgu_hbm, wgu_ref), (wo_hbm, wo_ref)))]
        for copy in copies:
            copy.start()
        _out_ffn_body(x_ref, yc_ref, yg_ref, gt1_ref, sc2_ref, sh2_ref, gt2_ref, scf_ref,
                      shf_ref, gffn_ref, gfin_ref, lng_ref, lnb_ref, o_ref,
                      wout_ref, wgu_ref, wo_ref, copies[0].wait,
                      lambda: (copies[1].wait(), copies[2].wait()))

    @pl.when(jnp.logical_not(is_first))
    def _():
        _out_ffn_body(x_ref, yc_ref, yg_ref, gt1_ref, sc2_ref, sh2_ref, gt2_ref, scf_ref,
                      shf_ref, gffn_ref, gfin_ref, lng_ref, lnb_ref, o_ref,
                      wout_ref, wgu_ref, wo_ref, lambda: None, lambda: None)


def _out_ffn_body(x_ref, yc_ref, yg_ref, gt1_ref, sc2_ref, sh2_ref, gt2_ref, scf_ref, shf_ref,
                  gffn_ref, gfin_ref, lng_ref, lnb_ref, o_ref, wout_ref, wgu_ref, wo_ref,
                  before_out_proj, before_ffn):
    tm = x_ref.shape[1]
    groups = [slice(r, r + FFN_ROWS) for r in range(0, tm, FFN_ROWS)]

    def out_proj(rows):
        conv = yc_ref[0, rows, :]
        mu = jnp.mean(conv, axis=-1, keepdims=True)
        cen = conv - mu
        var = jnp.mean(cen * cen, axis=-1, keepdims=True)
        y_conv = _silu(cen * lax.rsqrt(var + EPS) * lng_ref[...] + lnb_ref[...]).astype(BF16)
        mix = (_dot(y_conv, wout_ref[0:D_CONV, :])
               + _dot(yg_ref[0, rows, :], wout_ref[D_CONV:D_CONV + D_GLA_V, :]))
        return x_ref[0, rows, :] + gt1_ref[0] * mix

    def ffn_input(x1):
        ms = jnp.mean(x1 * x1, axis=-1, keepdims=True)
        h = x1 * lax.rsqrt(ms + EPS) * gffn_ref[...]
        return (h * (1.0 + sc2_ref[0]) + sh2_ref[0]).astype(BF16)

    def ffn(x1, hb):
        acc = jnp.zeros_like(x1)
        for lo, hi in FFN_SLABS:
            gate = _dot(hb, wgu_ref[:, lo:hi])
            up = _dot(hb, wgu_ref[:, D_FF + lo:D_FF + hi])
            act = (_silu(gate) * up).astype(BF16)
            acc = acc + _dot(act, wo_ref[lo:hi, :])
        return x1 + gt2_ref[0] * acc

    def final_norm(rows, x2):
        ms = jnp.mean(x2 * x2, axis=-1, keepdims=True)
        xn = x2 * lax.rsqrt(ms + EPS) * gfin_ref[...]
        o_ref[0, rows, :] = xn * (1.0 + scf_ref[0]) + shf_ref[0]

    before_out_proj()
    x1s = [out_proj(rows) for rows in groups]
    before_ffn()
    pending = None
    for rows, x1 in zip(groups, x1s):
        hb = ffn_input(x1)
        if pending is not None:
            final_norm(*pending)
        pending = (rows, ffn(x1, hb))
    final_norm(*pending)


def _out_ffn(x, yc, yg, mod, gffn, gfin, lng, lnb, wout, wgu, wo):
    b_sz, t_len, d = x.shape
    tm = TOKENS_FFN
    assert t_len % tm == 0 and tm % FFN_ROWS == 0

    def tok(width):
        return pl.BlockSpec((1, tm, width), lambda b, t: (b, t, 0))

    def const(arr):
        return pl.BlockSpec(arr.shape, lambda b, t: (0,) * arr.ndim,
                            pipeline_mode=pl.Buffered(1))

    mods = (MOD_GT1, MOD_SC2, MOD_SH2, MOD_GT2, MOD_SCF, MOD_SHF)
    in_hbm = pl.BlockSpec(memory_space=pl.ANY)
    return pl.pallas_call(
        _out_ffn_kernel,
        grid=(b_sz, t_len // tm),
        in_specs=[tok(d), tok(D_CONV), tok(D_GLA_V)] + [_mod_spec(m) for m in mods]
                 + [const(gffn), const(gfin), const(lng), const(lnb), in_hbm, in_hbm, in_hbm],
        out_specs=tok(d),
        out_shape=jax.ShapeDtypeStruct((b_sz, t_len, d), F32),
        scratch_shapes=[
            pltpu.VMEM(wout.shape, BF16),
            pltpu.VMEM(wgu.shape, BF16),
            pltpu.VMEM(wo.shape, BF16),
            pltpu.SemaphoreType.DMA((3,)),
        ],
        compiler_params=pltpu.CompilerParams(
            dimension_semantics=("arbitrary", "arbitrary"), vmem_limit_bytes=VMEM_LIMIT_BYTES),
        name="out_ffn",
    )(x, yc, yg, *([mod] * len(mods)), gffn, gfin, lng, lnb, wout, wgu, wo)


def kernel(x, c, w_ada, b_ada, g_mix, w_in, conv_w, conv_b, ln_g, ln_b, w_a2, b_a2,
           gla_norm_g, w_out, g_ffn, w_ffn_in, w_ffn_out, w_ada_final, b_ada_final, g_final):
    b_sz, t_len, d = x.shape
    assert w_ada.shape[0] == 1, "single-layer block"
    assert d == D_MODEL

    c_lanes = jnp.broadcast_to(c[:, :, None], (b_sz, d, LANES))
    mod = _modulation(c_lanes, w_ada[0], b_ada[0], w_ada_final, b_ada_final)

    w = jnp.concatenate(
        [w_in[0].astype(BF16), jnp.zeros((d, LANES - GATE_RANK), BF16)], axis=1)
    wa2 = jnp.pad(w_a2[0], ((0, LANES - GATE_RANK), (0, 0))).astype(BF16)

    u, q, k, v, gate, la, wout, wgu, wo = _in_proj(
        x, mod, g_mix[0].reshape(1, d), w, wa2, b_a2[0].reshape(1, D_GLA_K),
        (w_out[0], w_ffn_in[0], w_ffn_out[0]))

    cw = jnp.broadcast_to(conv_w[0][:, None, :], (CONV_KERNEL, SUBLANES, D_CONV))
    y_conv, y_gla = _mixer(u, q, k, la, v, gate, cw, conv_b[0].reshape(1, D_CONV),
                           gla_norm_g[0].reshape(1, HEAD_V))

    return _out_ffn(
        x, y_conv, y_gla, mod, g_ffn[0].reshape(1, d), g_final.reshape(1, d),
        ln_g[0].reshape(1, D_CONV), ln_b[0].reshape(1, D_CONV), wout, wgu, wo)
```

```python
import functools

import jax
import jax.numpy as jnp
from jax import lax
from jax.experimental import pallas as pl
from jax.experimental.pallas import tpu as pltpu

F32 = jnp.float32
BF16 = jnp.bfloat16

D_MODEL = 1024
D_CONV = 512
CONV_KERNEL = 31
D_GLA_V = 512
D_GLA_K = 256
GLA_HEADS = 4
HEAD_K = D_GLA_K // GLA_HEADS
HEAD_V = D_GLA_V // GLA_HEADS
GATE_RANK = 16
GATE_TAU = 16.0
D_FF = 2816
EPS = 1e-6

LANES = 128
SUBLANES = 8
MXU_DIM = 256
VMEM_LIMIT_BYTES = 56 * 1024 * 1024
MOD_ROWS = 256
TOKENS_IN = 1024
IN_ROWS = 512
TOKENS_MIX = 512
TOKENS_FFN = 512
CONV_HALO = 32
CONV_ROWS = 64
CONV_PARTIALS = 2
GLA_CHUNK = 128
GLA_MAX_FACTORISED_DECAY = 40.0
FFN_ROWS = 256
FFN_SLABS = ((0, 6 * MXU_DIM), (6 * MXU_DIM, D_FF))

IN_COLS = {}
_col = 0
for _name, _width in (("cv", D_CONV), ("cg", D_CONV), ("q", D_GLA_K), ("k", D_GLA_K),
                      ("v", D_GLA_V), ("og", D_GLA_V), ("a", LANES)):
    IN_COLS[_name] = (_col, _col + _width)
    _col += _width

MOD_SH1, MOD_SC1, MOD_GT1, MOD_SH2, MOD_SC2, MOD_GT2, MOD_SHF, MOD_SCF = range(8)


def _sigmoid(x):
    return 0.5 * jnp.tanh(0.5 * x) + 0.5


def _silu(x):
    h = 0.5 * x
    return h * jnp.tanh(h) + h


def _dot(a, b):
    return jnp.dot(a, b, preferred_element_type=F32)


def _dot_nt(a, b):
    return lax.dot_general(a, b, (((1,), (1,)), ((), ())), preferred_element_type=F32)


def _dot_tn(a, b):
    return lax.dot_general(a, b, (((0,), (0,)), ((), ())), preferred_element_type=F32)


def _split3(a):
    hi = a.astype(BF16)
    r1 = a - hi.astype(F32)
    mid = r1.astype(BF16)
    lo = (r1 - mid.astype(F32)).astype(BF16)
    return hi, mid, lo


def _modulation_kernel(c_ref, w_ref, b_ref, wf_ref, bf_ref, o_ref, acc_ref):
    j = pl.program_id(0)
    n_batch, k_rows, _ = c_ref.shape
    n = w_ref.shape[1]

    @pl.when(j == 0)
    def _():
        acc_ref[...] = jnp.zeros_like(acc_ref)

    c_act = []
    for b in range(n_batch):
        cb = c_ref[b]
        c_act.append(_silu(cb))
    for ref, offset in ((w_ref, 0), (wf_ref, n)):
        for t in range(ref.shape[1] // LANES):
            w = ref[:, t * LANES:(t + 1) * LANES]
            cols = slice(offset + t * LANES, offset + (t + 1) * LANES)
            for b in range(n_batch):
                prod = (w * c_act[b]).reshape(k_rows // SUBLANES, SUBLANES, LANES)
                acc_ref[b, :, cols] += jnp.sum(prod, axis=0)

    @pl.when(j == pl.num_programs(0) - 1)
    def _():
        bias = jnp.concatenate([b_ref[...], bf_ref[...]], axis=1)
        for b in range(n_batch):
            o_ref[b] = jnp.sum(acc_ref[b], axis=0, keepdims=True) + bias


def _modulation(c_lanes, w, b, wf, bf):
    n_batch, d, _ = c_lanes.shape
    n, nf = w.shape[1], wf.shape[1]
    assert d % MOD_ROWS == 0 and n % LANES == 0 and nf % LANES == 0
    return pl.pallas_call(
        _modulation_kernel,
        grid=(d // MOD_ROWS,),
        in_specs=[
            pl.BlockSpec((n_batch, MOD_ROWS, LANES), lambda j: (0, j, 0)),
            pl.BlockSpec((MOD_ROWS, n), lambda j: (j, 0)),
            pl.BlockSpec((1, n), lambda j: (0, 0)),
            pl.BlockSpec((MOD_ROWS, nf), lambda j: (j, 0)),
            pl.BlockSpec((1, nf), lambda j: (0, 0)),
        ],
        out_specs=pl.BlockSpec((n_batch, 1, n + nf), lambda j: (0, 0, 0)),
        out_shape=jax.ShapeDtypeStruct((n_batch, 1, n + nf), F32),
        scratch_shapes=[pltpu.VMEM((n_batch, SUBLANES, n + nf), F32)],
        compiler_params=pltpu.CompilerParams(
            dimension_semantics=("arbitrary",), vmem_limit_bytes=VMEM_LIMIT_BYTES),
        name="modulation",
    )(c_lanes, w, b.reshape(1, n), wf, bf.reshape(1, nf))


def _mod_spec(which):
    return pl.BlockSpec((1, 1, D_MODEL), lambda b, t: (b, 0, which))


def _in_proj_kernel(x_ref, sc_ref, sh_ref, g_ref, w_ref, wa2_ref, ba2_ref,
                    wout_f32, wgu_f32, wo_f32,
                    u_ref, q_ref, k_ref, v_ref, gate_ref, la_ref,
                    wout_bf16, wgu_bf16, wo_bf16):
    def round_weight_blocks():
        for src, dst in ((wout_f32, wout_bf16), (wgu_f32, wgu_bf16), (wo_f32, wo_bf16)):
            dst[...] = src[...].astype(dst.dtype)

    tm = x_ref.shape[1]
    groups = [slice(r, r + IN_ROWS) for r in range(0, tm, IN_ROWS)]

    def normed(rows):
        x = x_ref[0, rows, :]
        ms = jnp.mean(x * x, axis=-1, keepdims=True)
        h = x * lax.rsqrt(ms + EPS) * g_ref[...]
        return (h * (1.0 + sc_ref[0]) + sh_ref[0]).astype(BF16)

    def epilogue(rows, proj):
        def piece(name):
            lo, hi = IN_COLS[name]
            return proj[:, lo:hi]

        u_ref[0, rows, :] = piece("cv") * _sigmoid(piece("cg"))
        q_ref[0, rows, :] = piece("q") * (HEAD_K ** -0.5)
        k_ref[0, rows, :] = piece("k")
        v_ref[0, rows, :] = piece("v")
        gate_ref[0, rows, :] = _silu(piece("og"))

        a_low = piece("a")
        z = _dot(a_low.astype(BF16), wa2_ref[...]) + ba2_ref[...]
        log_sig = jnp.minimum(z, 0.0) - jnp.log(1.0 + jnp.exp(-jnp.abs(z)))
        la_ref[0, rows, :] = log_sig * (1.0 / GATE_TAU)

    pending = None
    for rows in groups:
        proj = _dot(normed(rows), w_ref[...])
        if pending is not None:
            epilogue(*pending)
        pending = (rows, proj)
    round_weight_blocks()
    epilogue(*pending)


def _in_proj(x, mod, g, w, wa2, ba2, later_weights):
    b_sz, t_len, d = x.shape
    tm = TOKENS_IN
    assert t_len % tm == 0
    n_t = t_len // tm
    n_steps = b_sz * n_t
    bf16_rows = 2 * SUBLANES
    assert all(a.shape[0] % (n_steps * bf16_rows) == 0 for a in later_weights)

    def row_block(arr):
        return pl.BlockSpec((arr.shape[0] // n_steps, arr.shape[1]),
                            lambda b, t: (b * n_t + t, 0))

    def tok(width):
        return pl.BlockSpec((1, tm, width), lambda b, t: (b, t, 0))

    def const(arr):
        return pl.BlockSpec(arr.shape, lambda b, t: (0,) * arr.ndim)

    def out(width):
        return jax.ShapeDtypeStruct((b_sz, t_len, width), F32)

    return pl.pallas_call(
        _in_proj_kernel,
        grid=(b_sz, t_len // tm),
        in_specs=[tok(d), _mod_spec(MOD_SC1), _mod_spec(MOD_SH1), const(g), const(w),
                  const(wa2), const(ba2)] + [row_block(a) for a in later_weights],
        out_specs=[tok(D_CONV), tok(D_GLA_K), tok(D_GLA_K), tok(D_GLA_V), tok(D_GLA_V),
                   tok(D_GLA_K)] + [row_block(a) for a in later_weights],
        out_shape=[out(D_CONV), out(D_GLA_K), out(D_GLA_K), out(D_GLA_V), out(D_GLA_V),
                   out(D_GLA_K)]
                  + [jax.ShapeDtypeStruct(a.shape, BF16) for a in later_weights],
        compiler_params=pltpu.CompilerParams(
            dimension_semantics=("arbitrary", "arbitrary"), vmem_limit_bytes=VMEM_LIMIT_BYTES),
        name="in_proj",
    )(x, mod, mod, g, w, wa2, ba2, *later_weights)


def _mixer_kernel(u_ref, q_ref, k_ref, la_ref, v_ref, gate_ref, cw_ref, cb_ref, gng_ref,
                  yc_ref, yg_ref, win_ref, st_ref, cum_ref):
    nb, tb = u_ref.shape[0], u_ref.shape[1]
    lc = GLA_CHUNK

    @pl.when(pl.program_id(0) == 0)
    def _():
        win_ref[...] = jnp.zeros_like(win_ref)
        st_ref[...] = jnp.zeros_like(st_ref)

    for b in range(nb):
        for c in range(D_CONV // LANES):
            u_lanes = u_ref[b, :, c * LANES:(c + 1) * LANES]
            for phase in range(SUBLANES):
                win_ref[b, phase, c, CONV_HALO - phase:CONV_HALO - phase + tb, :] = u_lanes
    first_tap = CONV_HALO - (CONV_KERNEL - 1)

    for c in range(D_CONV // LANES):
        lanes = slice(c * LANES, (c + 1) * LANES)
        taps = []
        for phase in range(SUBLANES):
            for base in range(0, CONV_HALO + SUBLANES, SUBLANES):
                tap = base + phase - first_tap
                if 0 <= tap < CONV_KERNEL:
                    taps.append((cw_ref[tap, :, lanes], phase, base))
        bias = cb_ref[:, lanes]

        def conv_tile(i, carry, c=c, lanes=lanes, taps=taps, bias=bias):
            r0 = pl.multiple_of(i * CONV_ROWS, CONV_ROWS)
            for b in range(nb):
                parts = [jnp.broadcast_to(bias, (CONV_ROWS, LANES))] + [None] * (CONV_PARTIALS - 1)
                for n, (w8, phase, base) in enumerate(taps):
                    w_rows = jnp.concatenate([w8] * (CONV_ROWS // SUBLANES), axis=0)
                    term = w_rows * win_ref[b, phase, c, pl.ds(r0 + base, CONV_ROWS), :]
                    p = n % CONV_PARTIALS
                    parts[p] = term if parts[p] is None else parts[p] + term
                yc_ref[b, pl.ds(r0, CONV_ROWS), lanes] = functools.reduce(lambda x, y: x + y, parts)
            return carry

        lax.fori_loop(0, tb // CONV_ROWS, conv_tile, 0)

    win_ref[:, :, :, 0:CONV_HALO, :] = win_ref[:, :, :, tb:tb + CONV_HALO, :]

    n_chunks = tb // lc
    row = lax.broadcasted_iota(jnp.int32, (lc, lc), 0)
    col = lax.broadcasted_iota(jnp.int32, (lc, lc), 1)
    tril = (col <= row).astype(BF16)
    klane = lax.broadcasted_iota(jnp.int32, (1, D_GLA_K), 1) // HEAD_K
    srow = lax.broadcasted_iota(jnp.int32, (GLA_HEADS * lc, lc), 0)
    scol = lax.broadcasted_iota(jnp.int32, (GLA_HEADS * lc, lc), 1)
    causal = scol <= (srow % lc)
    rows1 = lax.broadcasted_iota(jnp.int32, (lc, 1), 0)

    decay = None
    for b in range(nb):
        for ci in range(n_chunks):
            hi, mid, lo = _split3(la_ref[b, ci * lc:(ci + 1) * lc, :])
            cum = _dot(tril, hi) + _dot(tril, mid) + _dot(tril, lo)
            cum_ref[b, ci * lc:(ci + 1) * lc, :] = cum
            total = -cum[lc - 1:lc, :]
            decay = total if decay is None else jnp.maximum(decay, total)
    max_decay = jnp.max(decay)

    def chunk_factorised(b, ci, q, k, cum, cum_last):
        ref = cum[lc // 2 - 1:lc // 2, :]
        q_dec = q * jnp.exp(cum)
        k_inc = k * jnp.exp(-cum)
        qt = (q_dec * jnp.exp(-ref)).astype(BF16)
        kt = (k_inc * jnp.exp(ref)).astype(BF16)
        qbd = jnp.concatenate(
            [jnp.where(klane == h, qt, jnp.zeros_like(qt)) for h in range(GLA_HEADS)], axis=0)
        s = _dot_nt(qbd, kt)
        s = jnp.where(causal, s, 0.0).astype(BF16)
        heads = []
        for h in range(GLA_HEADS):
            vh = v_ref[b, ci * lc:(ci + 1) * lc, h * HEAD_V:(h + 1) * HEAD_V].astype(BF16)
            heads.append(_dot(s[h * lc:(h + 1) * lc, :], vh))
        o_intra = jnp.concatenate(heads, axis=-1)
        return o_intra, q_dec.astype(BF16), (k_inc * jnp.exp(cum_last)).astype(BF16)

    def chunk_exact(b, ci, q, k, cum, cum_last):
        e_row = lax.broadcasted_iota(jnp.int32, (D_GLA_K, D_GLA_V), 0) // HEAD_K
        e_col = lax.broadcasted_iota(jnp.int32, (D_GLA_K, D_GLA_V), 1) // HEAD_V
        expand = (e_row == e_col).astype(BF16)

        def key_row(j, o):
            kj = k_ref[b, pl.ds(ci * lc + j, 1), :]
            cj = cum_ref[b, pl.ds(ci * lc + j, 1), :]
            vj = v_ref[b, pl.ds(ci * lc + j, 1), :]
            p = q * kj * jnp.exp(jnp.minimum(cum - cj, 0.0))
            p = jnp.where(rows1 >= j, p, 0.0).astype(BF16)
            return o + _dot(p, expand) * vj

        o_intra = lax.fori_loop(0, lc, key_row, jnp.zeros((lc, D_GLA_V), F32))
        return (o_intra, (q * jnp.exp(cum)).astype(BF16),
                (k * jnp.exp(cum_last - cum)).astype(BF16))

    def gla_tile(chunk_terms):
        for ci in range(n_chunks):
            for b in range(nb):
                rows = slice(ci * lc, (ci + 1) * lc)
                q = q_ref[b, rows, :]
                k = k_ref[b, rows, :]
                cum = cum_ref[b, rows, :]
                cum_last = cum[lc - 1:lc, :]
                o_intra, qh, kh = chunk_terms(b, ci, q, k, cum, cum_last)

                st = st_ref[b]
                o_inter = _dot_nt(qh, st.astype(BF16))
                vb = v_ref[b, rows, :].astype(BF16)
                update = jnp.concatenate(
                    [_dot_tn(vb[:, h * HEAD_V:(h + 1) * HEAD_V],
                             jnp.where(klane == h, kh, jnp.zeros_like(kh)))
                     for h in range(GLA_HEADS)], axis=0)
                st_ref[b] = st * jnp.exp(cum_last) + update

                o = o_inter + o_intra
                normed = []
                for h in range(GLA_HEADS):
                    oh = o[:, h * HEAD_V:(h + 1) * HEAD_V]
                    ms = jnp.mean(oh * oh, axis=-1, keepdims=True)
                    normed.append(oh * lax.rsqrt(ms + EPS) * gng_ref[...])
                y_gla = jnp.concatenate(normed, axis=-1) * gate_ref[b, rows, :]
                yg_ref[b, rows, :] = y_gla.astype(yg_ref.dtype)

    @pl.when(max_decay <= GLA_MAX_FACTORISED_DECAY)
    def _():
        gla_tile(chunk_factorised)

    @pl.when(max_decay > GLA_MAX_FACTORISED_DECAY)
    def _():
        gla_tile(chunk_exact)


def _mixer(u, q, k, la, v, gate, cw, cb, gng):
    b_sz, t_len, _ = u.shape
    tb = TOKENS_MIX
    assert t_len % tb == 0 and tb % GLA_CHUNK == 0 and tb % CONV_ROWS == 0

    def tok(width):
        return pl.BlockSpec((b_sz, tb, width), lambda t: (0, t, 0))

    def const(arr):
        return pl.BlockSpec(arr.shape, lambda t: (0,) * arr.ndim)

    return pl.pallas_call(
        _mixer_kernel,
        grid=(t_len // tb,),
        in_specs=[tok(D_CONV), tok(D_GLA_K), tok(D_GLA_K), tok(D_GLA_K), tok(D_GLA_V),
                  tok(D_GLA_V), const(cw), const(cb), const(gng)],
        out_specs=[tok(D_CONV), tok(D_GLA_V)],
        out_shape=[jax.ShapeDtypeStruct((b_sz, t_len, D_CONV), F32),
                   jax.ShapeDtypeStruct((b_sz, t_len, D_GLA_V), BF16)],
        scratch_shapes=[
            pltpu.VMEM((b_sz, SUBLANES, D_CONV // LANES, CONV_HALO + tb, LANES), F32),
            pltpu.VMEM((b_sz, D_GLA_V, D_GLA_K), F32),
            pltpu.VMEM((b_sz, tb, D_GLA_K), F32),
        ],
        compiler_params=pltpu.CompilerParams(
            dimension_semantics=("arbitrary",), vmem_limit_bytes=VMEM_LIMIT_BYTES),
        name="mixer",
    )(u, q, k, la, v, gate, cw, cb, gng)


def _out_ffn_kernel(x_ref, yc_ref, yg_ref, gt1_ref, sc2_ref, sh2_ref, gt2_ref, scf_ref, shf_ref,
                    gffn_ref, gfin_ref, lng_ref, lnb_ref, wout_hbm, wgu_hbm, wo_hbm, o_ref,
                    wout_ref, wgu_ref, wo_ref, sem):
    is_first = (pl.program_id(0) == 0) & (pl.program_id(1) == 0)

    @pl.when(is_first)
    def _():
        pieces = [("out", wout_hbm, wout_ref)]
        for n, (lo, hi) in enumerate(FFN_SLABS):
            up_lo, up_hi = D_FF + lo, D_FF + hi
            pieces += [(("gate", n), wgu_hbm.at[:, lo:hi], wgu_ref.at[:, lo:hi]),
                       (("up", n), wgu_hbm.at[:, up_lo:up_hi], wgu_ref.at[:, up_lo:up_hi]),
                       (("down", n), wo_hbm.at[lo:hi, :], wo_ref.at[lo:hi, :])]
        copies = {name: pltpu.make_async_copy(src, dst, sem.at[i])
                  for i, (name, src, dst) in enumerate(pieces)}
        for copy in copies.values():
            copy.start()
        _out_ffn_body(x_ref, yc_ref, yg_ref, gt1_ref, sc2_ref, sh2_ref, gt2_ref, scf_ref,
                      shf_ref, gffn_ref, gfin_ref, lng_ref, lnb_ref, o_ref,
                      wout_ref, wgu_ref, wo_ref, copies)
        assert not copies

    @pl.when(jnp.logical_not(is_first))
    def _():
        _out_ffn_body(x_ref, yc_ref, yg_ref, gt1_ref, sc2_ref, sh2_ref, gt2_ref, scf_ref,
                      shf_ref, gffn_ref, gfin_ref, lng_ref, lnb_ref, o_ref,
                      wout_ref, wgu_ref, wo_ref, {})


def _out_ffn_body(x_ref, yc_ref, yg_ref, gt1_ref, sc2_ref, sh2_ref, gt2_ref, scf_ref, shf_ref,
                  gffn_ref, gfin_ref, lng_ref, lnb_ref, o_ref, wout_ref, wgu_ref, wo_ref,
                  in_flight):
    tm = x_ref.shape[1]
    groups = [slice(r, r + FFN_ROWS) for r in range(0, tm, FFN_ROWS)]

    def arrived(piece):
        copy = in_flight.pop(piece, None)
        if copy is not None:
            copy.wait()

    def out_proj(rows):
        conv = yc_ref[0, rows, :]
        mu = jnp.mean(conv, axis=-1, keepdims=True)
        cen = conv - mu
        var = jnp.mean(cen * cen, axis=-1, keepdims=True)
        y_conv = _silu(cen * lax.rsqrt(var + EPS) * lng_ref[...] + lnb_ref[...]).astype(BF16)
        mix = (_dot(y_conv, wout_ref[0:D_CONV, :])
               + _dot(yg_ref[0, rows, :], wout_ref[D_CONV:D_CONV + D_GLA_V, :]))
        return x_ref[0, rows, :] + gt1_ref[0] * mix

    def ffn_input(x1):
        ms = jnp.mean(x1 * x1, axis=-1, keepdims=True)
        h = x1 * lax.rsqrt(ms + EPS) * gffn_ref[...]
        return (h * (1.0 + sc2_ref[0]) + sh2_ref[0]).astype(BF16)

    def ffn(x1, hb):
        acc = jnp.zeros_like(x1)
        for n, (lo, hi) in enumerate(FFN_SLABS):
            arrived(("gate", n))
            gate = _dot(hb, wgu_ref[:, lo:hi])
            arrived(("up", n))
            up = _dot(hb, wgu_ref[:, D_FF + lo:D_FF + hi])
            act = (_silu(gate) * up).astype(BF16)
            arrived(("down", n))
            acc = acc + _dot(act, wo_ref[lo:hi, :])
        return x1 + gt2_ref[0] * acc

    def final_norm(rows, x2):
        ms = jnp.mean(x2 * x2, axis=-1, keepdims=True)
        xn = x2 * lax.rsqrt(ms + EPS) * gfin_ref[...]
        o_ref[0, rows, :] = xn * (1.0 + scf_ref[0]) + shf_ref[0]

    arrived("out")
    x1s = [out_proj(rows) for rows in groups]
    pending = None
    for rows, x1 in zip(groups, x1s):
        hb = ffn_input(x1)
        if pending is not None:
            final_norm(*pending)
        pending = (rows, ffn(x1, hb))
    final_norm(*pending)


def _out_ffn(x, yc, yg, mod, gffn, gfin, lng, lnb, wout, wgu, wo):
    b_sz, t_len, d = x.shape
    tm = TOKENS_FFN
    assert t_len % tm == 0 and tm % FFN_ROWS == 0

    def tok(width):
        return pl.BlockSpec((1, tm, width), lambda b, t: (b, t, 0))

    def const(arr):
        return pl.BlockSpec(arr.shape, lambda b, t: (0,) * arr.ndim,
                            pipeline_mode=pl.Buffered(1))

    mods = (MOD_GT1, MOD_SC2, MOD_SH2, MOD_GT2, MOD_SCF, MOD_SHF)
    in_hbm = pl.BlockSpec(memory_space=pl.ANY)
    return pl.pallas_call(
        _out_ffn_kernel,
        grid=(b_sz, t_len // tm),
        in_specs=[tok(d), tok(D_CONV), tok(D_GLA_V)] + [_mod_spec(m) for m in mods]
                 + [const(gffn), const(gfin), const(lng), const(lnb), in_hbm, in_hbm, in_hbm],
        out_specs=tok(d),
        out_shape=jax.ShapeDtypeStruct((b_sz, t_len, d), F32),
        scratch_shapes=[
            pltpu.VMEM(wout.shape, BF16),
            pltpu.VMEM(wgu.shape, BF16),
            pltpu.VMEM(wo.shape, BF16),
            pltpu.SemaphoreType.DMA((1 + 3 * len(FFN_SLABS),)),
        ],
        compiler_params=pltpu.CompilerParams(
            dimension_semantics=("arbitrary", "arbitrary"), vmem_limit_bytes=VMEM_LIMIT_BYTES),
        name="out_ffn",
    )(x, yc, yg, *([mod] * len(mods)), gffn, gfin, lng, lnb, wout, wgu, wo)


def kernel(x, c, w_ada, b_ada, g_mix, w_in, conv_w, conv_b, ln_g, ln_b, w_a2, b_a2,
           gla_norm_g, w_out, g_ffn, w_ffn_in, w_ffn_out, w_ada_final, b_ada_final, g_final):
    b_sz, t_len, d = x.shape
    assert w_ada.shape[0] == 1, "single-layer block"
    assert d == D_MODEL

    c_lanes = jnp.broadcast_to(c[:, :, None], (b_sz, d, LANES))
    mod = _modulation(c_lanes, w_ada[0], b_ada[0], w_ada_final, b_ada_final)

    w = jnp.concatenate(
        [w_in[0].astype(BF16), jnp.zeros((d, LANES - GATE_RANK), BF16)], axis=1)
    wa2 = jnp.pad(w_a2[0], ((0, LANES - GATE_RANK), (0, 0))).astype(BF16)

    u, q, k, v, gate, la, wout, wgu, wo = _in_proj(
        x, mod, g_mix[0].reshape(1, d), w, wa2, b_a2[0].reshape(1, D_GLA_K),
        (w_out[0], w_ffn_in[0], w_ffn_out[0]))

    cw = jnp.broadcast_to(conv_w[0][:, None, :], (CONV_KERNEL, SUBLANES, D_CONV))
    y_conv, y_gla = _mixer(u, q, k, la, v, gate, cw, conv_b[0].reshape(1, D_CONV),
                           gla_norm_g[0].reshape(1, HEAD_V))

    return _out_ffn(
        x, y_conv, y_gla, mod, g_ffn[0].reshape(1, d), g_final.reshape(1, d),
        ln_g[0].reshape(1, D_CONV), ln_b[0].reshape(1, D_CONV), wout, wgu, wo)
```

```python
import functools

import jax
import jax.numpy as jnp
from jax import lax
from jax.experimental import pallas as pl
from jax.experimental.pallas import tpu as pltpu

F32 = jnp.float32
BF16 = jnp.bfloat16

D_MODEL = 1024
D_CONV = 512
CONV_KERNEL = 31
D_GLA_V = 512
D_GLA_K = 256
GLA_HEADS = 4
HEAD_K = D_GLA_K // GLA_HEADS
HEAD_V = D_GLA_V // GLA_HEADS
GATE_RANK = 16
GATE_TAU = 16.0
D_FF = 2816
EPS = 1e-6

LANES = 128
SUBLANES = 8
MXU_DIM = 256
VMEM_LIMIT_BYTES = 56 * 1024 * 1024
MOD_ROWS = 256
TOKENS_IN = 1024
IN_ROWS = 512
TOKENS_MIX = 512
TOKENS_FFN = 512
CONV_HALO = 32
CONV_ROWS = 64
CONV_PARTIALS = 2
GLA_CHUNK = 128
GLA_MAX_FACTORISED_DECAY = 40.0
FFN_ROWS = 256
FFN_SLABS = ((0, 6 * MXU_DIM), (6 * MXU_DIM, D_FF))

IN_COLS = {}
_col = 0
for _name, _width in (("cv", D_CONV), ("cg", D_CONV), ("q", D_GLA_K), ("k", D_GLA_K),
                      ("v", D_GLA_V), ("og", D_GLA_V), ("a", LANES)):
    IN_COLS[_name] = (_col, _col + _width)
    _col += _width

MOD_SH1, MOD_SC1, MOD_GT1, MOD_SH2, MOD_SC2, MOD_GT2, MOD_SHF, MOD_SCF = range(8)


def _sigmoid(x):
    return 0.5 * jnp.tanh(0.5 * x) + 0.5


def _silu(x):
    h = 0.5 * x
    return h * jnp.tanh(h) + h


def _dot(a, b):
    return jnp.dot(a, b, preferred_element_type=F32)


def _dot_nt(a, b):
    return lax.dot_general(a, b, (((1,), (1,)), ((), ())), preferred_element_type=F32)


def _dot_tn(a, b):
    return lax.dot_general(a, b, (((0,), (0,)), ((), ())), preferred_element_type=F32)


def _split3(a):
    hi = a.astype(BF16)
    r1 = a - hi.astype(F32)
    mid = r1.astype(BF16)
    lo = (r1 - mid.astype(F32)).astype(BF16)
    return hi, mid, lo


def _modulation_kernel(c_ref, w_ref, b_ref, wf_ref, bf_ref, o_ref, acc_ref):
    j = pl.program_id(0)
    n_batch, k_rows, _ = c_ref.shape
    n = w_ref.shape[1]

    @pl.when(j == 0)
    def _():
        acc_ref[...] = jnp.zeros_like(acc_ref)

    c_act = []
    for b in range(n_batch):
        cb = c_ref[b]
        c_act.append(_silu(cb))
    for ref, offset in ((w_ref, 0), (wf_ref, n)):
        for t in range(ref.shape[1] // LANES):
            w = ref[:, t * LANES:(t + 1) * LANES]
            cols = slice(offset + t * LANES, offset + (t + 1) * LANES)
            for b in range(n_batch):
                prod = (w * c_act[b]).reshape(k_rows // SUBLANES, SUBLANES, LANES)
                acc_ref[b, :, cols] += jnp.sum(prod, axis=0)

    @pl.when(j == pl.num_programs(0) - 1)
    def _():
        bias = jnp.concatenate([b_ref[...], bf_ref[...]], axis=1)
        for b in range(n_batch):
            o_ref[b] = jnp.sum(acc_ref[b], axis=0, keepdims=True) + bias


def _modulation(c_lanes, w, b, wf, bf):
    n_batch, d, _ = c_lanes.shape
    n, nf = w.shape[1], wf.shape[1]
    assert d % MOD_ROWS == 0 and n % LANES == 0 and nf % LANES == 0
    return pl.pallas_call(
        _modulation_kernel,
        grid=(d // MOD_ROWS,),
        in_specs=[
            pl.BlockSpec((n_batch, MOD_ROWS, LANES), lambda j: (0, j, 0)),
            pl.BlockSpec((MOD_ROWS, n), lambda j: (j, 0)),
            pl.BlockSpec((1, n), lambda j: (0, 0)),
            pl.BlockSpec((MOD_ROWS, nf), lambda j: (j, 0)),
            pl.BlockSpec((1, nf), lambda j: (0, 0)),
        ],
        out_specs=pl.BlockSpec((n_batch, 1, n + nf), lambda j: (0, 0, 0)),
        out_shape=jax.ShapeDtypeStruct((n_batch, 1, n + nf), F32),
        scratch_shapes=[pltpu.VMEM((n_batch, SUBLANES, n + nf), F32)],
        compiler_params=pltpu.CompilerParams(
            dimension_semantics=("arbitrary",), vmem_limit_bytes=VMEM_LIMIT_BYTES),
        name="modulation",
    )(c_lanes, w, b.reshape(1, n), wf, bf.reshape(1, nf))


def _mod_spec(which):
    return pl.BlockSpec((1, 1, D_MODEL), lambda b, t: (b, 0, which))


def _in_proj_kernel(x_ref, sc_ref, sh_ref, g_ref, w_ref, wa2_ref, ba2_ref,
                    wout_f32, wgu_f32, wo_f32,
                    u_ref, q_ref, k_ref, v_ref, gate_ref, la_ref,
                    wout_bf16, wgu_bf16, wo_bf16):
    def round_weight_blocks():
        for src, dst in ((wout_f32, wout_bf16), (wgu_f32, wgu_bf16), (wo_f32, wo_bf16)):
            dst[...] = src[...].astype(dst.dtype)

    tm = x_ref.shape[1]
    groups = [slice(r, r + IN_ROWS) for r in range(0, tm, IN_ROWS)]

    def normed(rows):
        x = x_ref[0, rows, :]
        ms = jnp.mean(x * x, axis=-1, keepdims=True)
        h = x * lax.rsqrt(ms + EPS) * g_ref[...]
        return (h * (1.0 + sc_ref[0]) + sh_ref[0]).astype(BF16)

    def epilogue(rows, proj):
        def piece(name):
            lo, hi = IN_COLS[name]
            return proj[:, lo:hi]

        u_ref[0, rows, :] = piece("cv") * _sigmoid(piece("cg"))
        q_ref[0, rows, :] = piece("q") * (HEAD_K ** -0.5)
        k_ref[0, rows, :] = piece("k")
        v_ref[0, rows, :] = piece("v")
        gate_ref[0, rows, :] = _silu(piece("og"))

        a_low = piece("a")
        z = _dot(a_low.astype(BF16), wa2_ref[...]) + ba2_ref[...]
        log_sig = jnp.minimum(z, 0.0) - jnp.log(1.0 + jnp.exp(-jnp.abs(z)))
        la_ref[0, rows, :] = log_sig * (1.0 / GATE_TAU)

    pending = None
    for rows in groups:
        proj = _dot(normed(rows), w_ref[...])
        if pending is not None:
            epilogue(*pending)
        pending = (rows, proj)
    round_weight_blocks()
    epilogue(*pending)


def _in_proj(x, mod, g, w, wa2, ba2, later_weights):
    b_sz, t_len, d = x.shape
    tm = TOKENS_IN
    assert t_len % tm == 0
    n_t = t_len // tm
    n_steps = b_sz * n_t
    bf16_rows = 2 * SUBLANES
    assert all(a.shape[0] % (n_steps * bf16_rows) == 0 for a in later_weights)

    def row_block(arr):
        return pl.BlockSpec((arr.shape[0] // n_steps, arr.shape[1]),
                            lambda b, t: (b * n_t + t, 0))

    def tok(width):
        return pl.BlockSpec((1, tm, width), lambda b, t: (b, t, 0))

    def const(arr):
        return pl.BlockSpec(arr.shape, lambda b, t: (0,) * arr.ndim)

    def out(width):
        return jax.ShapeDtypeStruct((b_sz, t_len, width), F32)

    return pl.pallas_call(
        _in_proj_kernel,
        grid=(b_sz, t_len // tm),
        in_specs=[tok(d), _mod_spec(MOD_SC1), _mod_spec(MOD_SH1), const(g), const(w),
                  const(wa2), const(ba2)] + [row_block(a) for a in later_weights],
        out_specs=[tok(D_CONV), tok(D_GLA_K), tok(D_GLA_K), tok(D_GLA_V), tok(D_GLA_V),
                   tok(D_GLA_K)] + [row_block(a) for a in later_weights],
        out_shape=[out(D_CONV), out(D_GLA_K), out(D_GLA_K), out(D_GLA_V), out(D_GLA_V),
                   out(D_GLA_K)]
                  + [jax.ShapeDtypeStruct(a.shape, BF16) for a in later_weights],
        compiler_params=pltpu.CompilerParams(
            dimension_semantics=("arbitrary", "arbitrary"), vmem_limit_bytes=VMEM_LIMIT_BYTES),
        name="in_proj",
    )(x, mod, mod, g, w, wa2, ba2, *later_weights)


def _mixer_kernel(u_ref, q_ref, k_ref, la_ref, v_ref, gate_ref, cw_ref, cb_ref, gng_ref,
                  yc_ref, yg_ref, win_ref, st_ref, cum_ref, cw8_ref):
    nb, tb = u_ref.shape[0], u_ref.shape[1]
    lc = GLA_CHUNK

    @pl.when(pl.program_id(0) == 0)
    def _():
        win_ref[...] = jnp.zeros_like(win_ref)
        st_ref[...] = jnp.zeros_like(st_ref)
        for tap in range(CONV_KERNEL):
            cw8_ref[tap] = jnp.broadcast_to(cw_ref[0, tap:tap + 1, :], (SUBLANES, D_CONV))

    for b in range(nb):
        for c in range(D_CONV // LANES):
            u_lanes = u_ref[b, :, c * LANES:(c + 1) * LANES]
            for phase in range(SUBLANES):
                win_ref[b, phase, c, CONV_HALO - phase:CONV_HALO - phase + tb, :] = u_lanes
    first_tap = CONV_HALO - (CONV_KERNEL - 1)

    for c in range(D_CONV // LANES):
        lanes = slice(c * LANES, (c + 1) * LANES)
        taps = []
        for phase in range(SUBLANES):
            for base in range(0, CONV_HALO + SUBLANES, SUBLANES):
                tap = base + phase - first_tap
                if 0 <= tap < CONV_KERNEL:
                    taps.append((cw8_ref[tap, :, lanes], phase, base))
        bias = cb_ref[:, lanes]

        def conv_tile(i, carry, c=c, lanes=lanes, taps=taps, bias=bias):
            r0 = pl.multiple_of(i * CONV_ROWS, CONV_ROWS)
            for b in range(nb):
                parts = [jnp.broadcast_to(bias, (CONV_ROWS, LANES))] + [None] * (CONV_PARTIALS - 1)
                for n, (w8, phase, base) in enumerate(taps):
                    w_rows = jnp.concatenate([w8] * (CONV_ROWS // SUBLANES), axis=0)
                    term = w_rows * win_ref[b, phase, c, pl.ds(r0 + base, CONV_ROWS), :]
                    p = n % CONV_PARTIALS
                    parts[p] = term if parts[p] is None else parts[p] + term
                yc_ref[b, pl.ds(r0, CONV_ROWS), lanes] = functools.reduce(lambda x, y: x + y, parts)
            return carry

        lax.fori_loop(0, tb // CONV_ROWS, conv_tile, 0)

    win_ref[:, :, :, 0:CONV_HALO, :] = win_ref[:, :, :, tb:tb + CONV_HALO, :]

    n_chunks = tb // lc
    row = lax.broadcasted_iota(jnp.int32, (lc, lc), 0)
    col = lax.broadcasted_iota(jnp.int32, (lc, lc), 1)
    tril = (col <= row).astype(BF16)
    klane = lax.broadcasted_iota(jnp.int32, (1, D_GLA_K), 1) // HEAD_K
    srow = lax.broadcasted_iota(jnp.int32, (GLA_HEADS * lc, lc), 0)
    scol = lax.broadcasted_iota(jnp.int32, (GLA_HEADS * lc, lc), 1)
    causal = scol <= (srow % lc)
    rows1 = lax.broadcasted_iota(jnp.int32, (lc, 1), 0)

    decay = None
    for b in range(nb):
        for ci in range(n_chunks):
            hi, mid, lo = _split3(la_ref[b, ci * lc:(ci + 1) * lc, :])
            cum = _dot(tril, hi) + _dot(tril, mid) + _dot(tril, lo)
            cum_ref[b, ci * lc:(ci + 1) * lc, :] = cum
            total = -cum[lc - 1:lc, :]
            decay = total if decay is None else jnp.maximum(decay, total)
    max_decay = jnp.max(decay)

    def chunk_factorised(b, ci, q, k, cum, cum_last):
        ref = cum[lc // 2 - 1:lc // 2, :]
        q_dec = q * jnp.exp(cum)
        k_inc = k * jnp.exp(-cum)
        qt = (q_dec * jnp.exp(-ref)).astype(BF16)
        kt = (k_inc * jnp.exp(ref)).astype(BF16)
        qbd = jnp.concatenate(
            [jnp.where(klane == h, qt, jnp.zeros_like(qt)) for h in range(GLA_HEADS)], axis=0)
        s = _dot_nt(qbd, kt)
        s = jnp.where(causal, s, 0.0).astype(BF16)
        heads = []
        for h in range(GLA_HEADS):
            vh = v_ref[b, ci * lc:(ci + 1) * lc, h * HEAD_V:(h + 1) * HEAD_V].astype(BF16)
            heads.append(_dot(s[h * lc:(h + 1) * lc, :], vh))
        o_intra = jnp.concatenate(heads, axis=-1)
        return o_intra, q_dec.astype(BF16), (k_inc * jnp.exp(cum_last)).astype(BF16)

    def chunk_exact(b, ci, q, k, cum, cum_last):
        e_row = lax.broadcasted_iota(jnp.int32, (D_GLA_K, D_GLA_V), 0) // HEAD_K
        e_col = lax.broadcasted_iota(jnp.int32, (D_GLA_K, D_GLA_V), 1) // HEAD_V
        expand = (e_row == e_col).astype(BF16)

        def key_row(j, o):
            kj = k_ref[b, pl.ds(ci * lc + j, 1), :]
            cj = cum_ref[b, pl.ds(ci * lc + j, 1), :]
            vj = v_ref[b, pl.ds(ci * lc + j, 1), :]
            p = q * kj * jnp.exp(jnp.minimum(cum - cj, 0.0))
            p = jnp.where(rows1 >= j, p, 0.0).astype(BF16)
            return o + _dot(p, expand) * vj

        o_intra = lax.fori_loop(0, lc, key_row, jnp.zeros((lc, D_GLA_V), F32))
        return (o_intra, (q * jnp.exp(cum)).astype(BF16),
                (k * jnp.exp(cum_last - cum)).astype(BF16))

    def gla_tile(chunk_terms):
        for ci in range(n_chunks):
            for b in range(nb):
                rows = slice(ci * lc, (ci + 1) * lc)
                q = q_ref[b, rows, :]
                k = k_ref[b, rows, :]
                cum = cum_ref[b, rows, :]
                cum_last = cum[lc - 1:lc, :]
                o_intra, qh, kh = chunk_terms(b, ci, q, k, cum, cum_last)

                st = st_ref[b]
                o_inter = _dot_nt(qh, st.astype(BF16))
                vb = v_ref[b, rows, :].astype(BF16)
                update = jnp.concatenate(
                    [_dot_tn(vb[:, h * HEAD_V:(h + 1) * HEAD_V],
                             jnp.where(klane == h, kh, jnp.zeros_like(kh)))
                     for h in range(GLA_HEADS)], axis=0)
                st_ref[b] = st * jnp.exp(cum_last) + update

                o = o_inter + o_intra
                normed = []
                for h in range(GLA_HEADS):
                    oh = o[:, h * HEAD_V:(h + 1) * HEAD_V]
                    ms = jnp.mean(oh * oh, axis=-1, keepdims=True)
                    normed.append(oh * lax.rsqrt(ms + EPS) * gng_ref[...])
                y_gla = jnp.concatenate(normed, axis=-1) * gate_ref[b, rows, :]
                yg_ref[b, rows, :] = y_gla.astype(yg_ref.dtype)

    @pl.when(max_decay <= GLA_MAX_FACTORISED_DECAY)
    def _():
        gla_tile(chunk_factorised)

    @pl.when(max_decay > GLA_MAX_FACTORISED_DECAY)
    def _():
        gla_tile(chunk_exact)


def _mixer(u, q, k, la, v, gate, cw, cb, gng):
    b_sz, t_len, _ = u.shape
    tb = TOKENS_MIX
    assert t_len % tb == 0 and tb % GLA_CHUNK == 0 and tb % CONV_ROWS == 0

    def tok(width):
        return pl.BlockSpec((b_sz, tb, width), lambda t: (0, t, 0))

    def const(arr):
        return pl.BlockSpec(arr.shape, lambda t: (0,) * arr.ndim)

    return pl.pallas_call(
        _mixer_kernel,
        grid=(t_len // tb,),
        in_specs=[tok(D_CONV), tok(D_GLA_K), tok(D_GLA_K), tok(D_GLA_K), tok(D_GLA_V),
                  tok(D_GLA_V), const(cw), const(cb), const(gng)],
        out_specs=[tok(D_CONV), tok(D_GLA_V)],
        out_shape=[jax.ShapeDtypeStruct((b_sz, t_len, D_CONV), F32),
                   jax.ShapeDtypeStruct((b_sz, t_len, D_GLA_V), BF16)],
        scratch_shapes=[
            pltpu.VMEM((b_sz, SUBLANES, D_CONV // LANES, CONV_HALO + tb, LANES), F32),
            pltpu.VMEM((b_sz, D_GLA_V, D_GLA_K), F32),
            pltpu.VMEM((b_sz, tb, D_GLA_K), F32),
            pltpu.VMEM((CONV_KERNEL, SUBLANES, D_CONV), F32),
        ],
        compiler_params=pltpu.CompilerParams(
            dimension_semantics=("arbitrary",), vmem_limit_bytes=VMEM_LIMIT_BYTES),
        name="mixer",
    )(u, q, k, la, v, gate, cw, cb, gng)


def _out_ffn_kernel(x_ref, yc_ref, yg_ref, gt1_ref, sc2_ref, sh2_ref, gt2_ref, scf_ref, shf_ref,
                    gffn_ref, gfin_ref, lng_ref, lnb_ref, wout_hbm, wgu_hbm, wo_hbm, o_ref,
                    wout_ref, wgu_ref, wo_ref, sem):
    is_first = (pl.program_id(0) == 0) & (pl.program_id(1) == 0)

    @pl.when(is_first)
    def _():
        pieces = [("out", wout_hbm, wout_ref)]
        for n, (lo, hi) in enumerate(FFN_SLABS):
            up_lo, up_hi = D_FF + lo, D_FF + hi
            pieces += [(("gate", n), wgu_hbm.at[:, lo:hi], wgu_ref.at[:, lo:hi]),
                       (("up", n), wgu_hbm.at[:, up_lo:up_hi], wgu_ref.at[:, up_lo:up_hi]),
                       (("down", n), wo_hbm.at[lo:hi, :], wo_ref.at[lo:hi, :])]
        copies = {name: pltpu.make_async_copy(src, dst, sem.at[i])
                  for i, (name, src, dst) in enumerate(pieces)}
        for copy in copies.values():
            copy.start()
        _out_ffn_body(x_ref, yc_ref, yg_ref, gt1_ref, sc2_ref, sh2_ref, gt2_ref, scf_ref,
                      shf_ref, gffn_ref, gfin_ref, lng_ref, lnb_ref, o_ref,
                      wout_ref, wgu_ref, wo_ref, copies)
        assert not copies

    @pl.when(jnp.logical_not(is_first))
    def _():
        _out_ffn_body(x_ref, yc_ref, yg_ref, gt1_ref, sc2_ref, sh2_ref, gt2_ref, scf_ref,
                      shf_ref, gffn_ref, gfin_ref, lng_ref, lnb_ref, o_ref,
                      wout_ref, wgu_ref, wo_ref, {})


def _out_ffn_body(x_ref, yc_ref, yg_ref, gt1_ref, sc2_ref, sh2_ref, gt2_ref, scf_ref, shf_ref,
                  gffn_ref, gfin_ref, lng_ref, lnb_ref, o_ref, wout_ref, wgu_ref, wo_ref,
                  in_flight):
    tm = x_ref.shape[1]
    groups = [slice(r, r + FFN_ROWS) for r in range(0, tm, FFN_ROWS)]

    def arrived(piece):
        copy = in_flight.pop(piece, None)
        if copy is not None:
            copy.wait()

    def out_proj(rows):
        conv = yc_ref[0, rows, :]
        mu = jnp.mean(conv, axis=-1, keepdims=True)
        cen = conv - mu
        var = jnp.mean(cen * cen, axis=-1, keepdims=True)
        y_conv = _silu(cen * lax.rsqrt(var + EPS) * lng_ref[...] + lnb_ref[...]).astype(BF16)
        mix = (_dot(y_conv, wout_ref[0:D_CONV, :])
               + _dot(yg_ref[0, rows, :], wout_ref[D_CONV:D_CONV + D_GLA_V, :]))
        return x_ref[0, rows, :] + gt1_ref[0] * mix

    def ffn_input(x1):
        ms = jnp.mean(x1 * x1, axis=-1, keepdims=True)
        h = x1 * lax.rsqrt(ms + EPS) * gffn_ref[...]
        return (h * (1.0 + sc2_ref[0]) + sh2_ref[0]).astype(BF16)

    def ffn(x1, hb):
        acc = jnp.zeros_like(x1)
        for n, (lo, hi) in enumerate(FFN_SLABS):
            arrived(("gate", n))
            gate = _dot(hb, wgu_ref[:, lo:hi])
            arrived(("up", n))
            up = _dot(hb, wgu_ref[:, D_FF + lo:D_FF + hi])
            act = (_silu(gate) * up).astype(BF16)
            arrived(("down", n))
            acc = acc + _dot(act, wo_ref[lo:hi, :])
        return x1 + gt2_ref[0] * acc

    def final_norm(rows, x2):
        ms = jnp.mean(x2 * x2, axis=-1, keepdims=True)
        xn = x2 * lax.rsqrt(ms + EPS) * gfin_ref[...]
        o_ref[0, rows, :] = xn * (1.0 + scf_ref[0]) + shf_ref[0]

    arrived("out")
    x1s = [out_proj(rows) for rows in groups]
    pending = None
    for rows, x1 in zip(groups, x1s):
        hb = ffn_input(x1)
        if pending is not None:
            final_norm(*pending)
        pending = (rows, ffn(x1, hb))
    final_norm(*pending)


def _out_ffn(x, yc, yg, mod, gffn, gfin, lng, lnb, wout, wgu, wo):
    b_sz, t_len, d = x.shape
    tm = TOKENS_FFN
    assert t_len % tm == 0 and tm % FFN_ROWS == 0

    def tok(width):
        return pl.BlockSpec((1, tm, width), lambda b, t: (b, t, 0))

    def const(arr):
        return pl.BlockSpec(arr.shape, lambda b, t: (0,) * arr.ndim,
                            pipeline_mode=pl.Buffered(1))

    mods = (MOD_GT1, MOD_SC2, MOD_SH2, MOD_GT2, MOD_SCF, MOD_SHF)
    in_hbm = pl.BlockSpec(memory_space=pl.ANY)
    return pl.pallas_call(
        _out_ffn_kernel,
        grid=(b_sz, t_len // tm),
        in_specs=[tok(d), tok(D_CONV), tok(D_GLA_V)] + [_mod_spec(m) for m in mods]
                 + [const(gffn), const(gfin), const(lng), const(lnb), in_hbm, in_hbm, in_hbm],
        out_specs=tok(d),
        out_shape=jax.ShapeDtypeStruct((b_sz, t_len, d), F32),
        scratch_shapes=[
            pltpu.VMEM(wout.shape, BF16),
            pltpu.VMEM(wgu.shape, BF16),
            pltpu.VMEM(wo.shape, BF16),
            pltpu.SemaphoreType.DMA((1 + 3 * len(FFN_SLABS),)),
        ],
        compiler_params=pltpu.CompilerParams(
            dimension_semantics=("arbitrary", "arbitrary"), vmem_limit_bytes=VMEM_LIMIT_BYTES),
        name="out_ffn",
    )(x, yc, yg, *([mod] * len(mods)), gffn, gfin, lng, lnb, wout, wgu, wo)


def kernel(x, c, w_ada, b_ada, g_mix, w_in, conv_w, conv_b, ln_g, ln_b, w_a2, b_a2,
           gla_norm_g, w_out, g_ffn, w_ffn_in, w_ffn_out, w_ada_final, b_ada_final, g_final):
    b_sz, t_len, d = x.shape
    assert w_ada.shape[0] == 1, "single-layer block"
    assert d == D_MODEL

    c_lanes = jnp.broadcast_to(c[:, :, None], (b_sz, d, LANES))
    mod = _modulation(c_lanes, w_ada[0], b_ada[0], w_ada_final, b_ada_final)

    w = jnp.concatenate(
        [w_in[0].astype(BF16), jnp.zeros((d, LANES - GATE_RANK), BF16)], axis=1)
    wa2 = jnp.pad(w_a2[0], ((0, LANES - GATE_RANK), (0, 0))).astype(BF16)

    u, q, k, v, gate, la, wout, wgu, wo = _in_proj(
        x, mod, g_mix[0].reshape(1, d), w, wa2, b_a2[0].reshape(1, D_GLA_K),
        (w_out[0], w_ffn_in[0], w_ffn_out[0]))

    y_conv, y_gla = _mixer(u, q, k, la, v, gate, conv_w, conv_b[0].reshape(1, D_CONV),
                           gla_norm_g[0].reshape(1, HEAD_V))

    return _out_ffn(
        x, y_conv, y_gla, mod, g_ffn[0].reshape(1, d), g_final.reshape(1, d),
        ln_g[0].reshape(1, D_CONV), ln_b[0].reshape(1, D_CONV), wout, wgu, wo)
```

```python
import functools

import jax
import jax.numpy as jnp
from jax import lax
from jax.experimental import pallas as pl
from jax.experimental.pallas import tpu as pltpu

F32 = jnp.float32
BF16 = jnp.bfloat16

D_MODEL = 1024
D_CONV = 512
CONV_KERNEL = 31
D_GLA_V = 512
D_GLA_K = 256
GLA_HEADS = 4
HEAD_K = D_GLA_K // GLA_HEADS
HEAD_V = D_GLA_V // GLA_HEADS
GATE_RANK = 16
GATE_TAU = 16.0
D_FF = 2816
EPS = 1e-6

LANES = 128
SUBLANES = 8
MXU_DIM = 256
VMEM_LIMIT_BYTES = 56 * 1024 * 1024
MOD_ROWS = 256
TOKENS_IN = 1024
IN_ROWS = 512
TOKENS_MIX = 512
TOKENS_FFN = 512
CONV_HALO = 32
CONV_ROWS = 64
CONV_PARTIALS = 2
GLA_CHUNK = 128
GLA_MAX_FACTORISED_DECAY = 40.0
FFN_ROWS = 256
FFN_SLABS = ((0, 6 * MXU_DIM), (6 * MXU_DIM, D_FF))

IN_COLS = {}
_col = 0
for _name, _width in (("cv", D_CONV), ("cg", D_CONV), ("q", D_GLA_K), ("k", D_GLA_K),
                      ("v", D_GLA_V), ("og", D_GLA_V), ("a", LANES)):
    IN_COLS[_name] = (_col, _col + _width)
    _col += _width

MOD_SH1, MOD_SC1, MOD_GT1, MOD_SH2, MOD_SC2, MOD_GT2, MOD_SHF, MOD_SCF = range(8)


def _sigmoid(x):
    return 0.5 * jnp.tanh(0.5 * x) + 0.5


def _silu(x):
    h = 0.5 * x
    return h * jnp.tanh(h) + h


def _dot(a, b):
    return jnp.dot(a, b, preferred_element_type=F32)


def _dot_nt(a, b):
    return lax.dot_general(a, b, (((1,), (1,)), ((), ())), preferred_element_type=F32)


def _dot_tn(a, b):
    return lax.dot_general(a, b, (((0,), (0,)), ((), ())), preferred_element_type=F32)


def _split3(a):
    hi = a.astype(BF16)
    r1 = a - hi.astype(F32)
    mid = r1.astype(BF16)
    lo = (r1 - mid.astype(F32)).astype(BF16)
    return hi, mid, lo


def _modulation_kernel(c_ref, w_ref, b_ref, wf_ref, bf_ref, o_ref, acc_ref):
    j = pl.program_id(0)
    n_batch, k_rows, _ = c_ref.shape
    n = w_ref.shape[1]

    @pl.when(j == 0)
    def _():
        acc_ref[...] = jnp.zeros_like(acc_ref)

    c_act = []
    for b in range(n_batch):
        cb = c_ref[b]
        c_act.append(_silu(cb))
    for ref, offset in ((w_ref, 0), (wf_ref, n)):
        for t in range(ref.shape[1] // LANES):
            w = ref[:, t * LANES:(t + 1) * LANES]
            cols = slice(offset + t * LANES, offset + (t + 1) * LANES)
            for b in range(n_batch):
                prod = (w * c_act[b]).reshape(k_rows // SUBLANES, SUBLANES, LANES)
                acc_ref[b, :, cols] += jnp.sum(prod, axis=0)

    @pl.when(j == pl.num_programs(0) - 1)
    def _():
        bias = jnp.concatenate([b_ref[...], bf_ref[...]], axis=1)
        for b in range(n_batch):
            o_ref[b] = jnp.sum(acc_ref[b], axis=0, keepdims=True) + bias


def _modulation(c_lanes, w, b, wf, bf):
    n_batch, d, _ = c_lanes.shape
    n, nf = w.shape[1], wf.shape[1]
    assert d % MOD_ROWS == 0 and n % LANES == 0 and nf % LANES == 0
    return pl.pallas_call(
        _modulation_kernel,
        grid=(d // MOD_ROWS,),
        in_specs=[
            pl.BlockSpec((n_batch, MOD_ROWS, LANES), lambda j: (0, j, 0)),
            pl.BlockSpec((MOD_ROWS, n), lambda j: (j, 0)),
            pl.BlockSpec((1, n), lambda j: (0, 0)),
            pl.BlockSpec((MOD_ROWS, nf), lambda j: (j, 0)),
            pl.BlockSpec((1, nf), lambda j: (0, 0)),
        ],
        out_specs=pl.BlockSpec((n_batch, 1, n + nf), lambda j: (0, 0, 0)),
        out_shape=jax.ShapeDtypeStruct((n_batch, 1, n + nf), F32),
        scratch_shapes=[pltpu.VMEM((n_batch, SUBLANES, n + nf), F32)],
        compiler_params=pltpu.CompilerParams(
            dimension_semantics=("arbitrary",), vmem_limit_bytes=VMEM_LIMIT_BYTES),
        name="modulation",
    )(c_lanes, w, b.reshape(1, n), wf, bf.reshape(1, nf))


def _mod_spec(which):
    return pl.BlockSpec((1, 1, D_MODEL), lambda b, t: (b, 0, which))


def _in_proj_kernel(x_ref, sc_ref, sh_ref, g_ref, w_ref, wa2_ref, ba2_ref,
                    wout_f32, wgu_f32, wo_f32,
                    u_ref, q_ref, k_ref, v_ref, gate_ref, la_ref,
                    wout_bf16, wgu_bf16, wo_bf16):
    def round_weight_blocks():
        for src, dst in ((wout_f32, wout_bf16), (wgu_f32, wgu_bf16), (wo_f32, wo_bf16)):
            dst[...] = src[...].astype(dst.dtype)

    tm = x_ref.shape[1]
    groups = [slice(r, r + IN_ROWS) for r in range(0, tm, IN_ROWS)]

    def normed(rows):
        x = x_ref[0, rows, :]
        ms = jnp.mean(x * x, axis=-1, keepdims=True)
        h = x * lax.rsqrt(ms + EPS) * g_ref[...]
        return (h * (1.0 + sc_ref[0]) + sh_ref[0]).astype(BF16)

    def epilogue(rows, proj):
        def piece(name):
            lo, hi = IN_COLS[name]
            return proj[:, lo:hi]

        u_ref[0, rows, :] = piece("cv") * _sigmoid(piece("cg"))
        q_ref[0, rows, :] = piece("q") * (HEAD_K ** -0.5)
        k_ref[0, rows, :] = piece("k")
        v_ref[0, rows, :] = piece("v")
        gate_ref[0, rows, :] = _silu(piece("og"))

        a_low = piece("a")
        wa2 = jnp.concatenate(
            [wa2_ref[0].astype(BF16), jnp.zeros((LANES - GATE_RANK, D_GLA_K), BF16)], axis=0)
        z = _dot(a_low.astype(BF16), wa2) + ba2_ref[...]
        log_sig = jnp.minimum(z, 0.0) - jnp.log(1.0 + jnp.exp(-jnp.abs(z)))
        la_ref[0, rows, :] = log_sig * (1.0 / GATE_TAU)

    pending = None
    for rows in groups:
        proj = _dot(normed(rows), w_ref[...])
        if pending is not None:
            epilogue(*pending)
        pending = (rows, proj)
    round_weight_blocks()
    epilogue(*pending)


def _in_proj(x, mod, g, w, wa2, ba2, later_weights):
    b_sz, t_len, d = x.shape
    tm = TOKENS_IN
    assert t_len % tm == 0
    n_t = t_len // tm
    n_steps = b_sz * n_t
    bf16_rows = 2 * SUBLANES
    assert all(a.shape[0] % (n_steps * bf16_rows) == 0 for a in later_weights)

    def row_block(arr):
        return pl.BlockSpec((arr.shape[0] // n_steps, arr.shape[1]),
                            lambda b, t: (b * n_t + t, 0))

    def tok(width):
        return pl.BlockSpec((1, tm, width), lambda b, t: (b, t, 0))

    def const(arr):
        return pl.BlockSpec(arr.shape, lambda b, t: (0,) * arr.ndim)

    def out(width):
        return jax.ShapeDtypeStruct((b_sz, t_len, width), F32)

    return pl.pallas_call(
        _in_proj_kernel,
        grid=(b_sz, t_len // tm),
        in_specs=[tok(d), _mod_spec(MOD_SC1), _mod_spec(MOD_SH1), const(g), const(w),
                  const(wa2), const(ba2)] + [row_block(a) for a in later_weights],
        out_specs=[tok(D_CONV), tok(D_GLA_K), tok(D_GLA_K), tok(D_GLA_V), tok(D_GLA_V),
                   tok(D_GLA_K)] + [row_block(a) for a in later_weights],
        out_shape=[out(D_CONV), out(D_GLA_K), out(D_GLA_K), out(D_GLA_V), out(D_GLA_V),
                   out(D_GLA_K)]
                  + [jax.ShapeDtypeStruct(a.shape, BF16) for a in later_weights],
        compiler_params=pltpu.CompilerParams(
            dimension_semantics=("arbitrary", "arbitrary"), vmem_limit_bytes=VMEM_LIMIT_BYTES),
        name="in_proj",
    )(x, mod, mod, g, w, wa2, ba2, *later_weights)


def _mixer_kernel(u_ref, q_ref, k_ref, la_ref, v_ref, gate_ref, cw_ref, cb_ref, gng_ref,
                  yc_ref, yg_ref, win_ref, st_ref, cum_ref, cw8_ref):
    nb, tb = u_ref.shape[0], u_ref.shape[1]
    lc = GLA_CHUNK

    @pl.when(pl.program_id(0) == 0)
    def _():
        win_ref[...] = jnp.zeros_like(win_ref)
        st_ref[...] = jnp.zeros_like(st_ref)
        for tap in range(CONV_KERNEL):
            cw8_ref[tap] = jnp.broadcast_to(cw_ref[0, tap:tap + 1, :], (SUBLANES, D_CONV))

    for b in range(nb):
        for c in range(D_CONV // LANES):
            u_lanes = u_ref[b, :, c * LANES:(c + 1) * LANES]
            for phase in range(SUBLANES):
                win_ref[b, phase, c, CONV_HALO - phase:CONV_HALO - phase + tb, :] = u_lanes
    first_tap = CONV_HALO - (CONV_KERNEL - 1)

    for c in range(D_CONV // LANES):
        lanes = slice(c * LANES, (c + 1) * LANES)
        taps = []
        for phase in range(SUBLANES):
            for base in range(0, CONV_HALO + SUBLANES, SUBLANES):
                tap = base + phase - first_tap
                if 0 <= tap < CONV_KERNEL:
                    taps.append((cw8_ref[tap, :, lanes], phase, base))
        bias = cb_ref[:, lanes]

        def conv_tile(i, carry, c=c, lanes=lanes, taps=taps, bias=bias):
            r0 = pl.multiple_of(i * CONV_ROWS, CONV_ROWS)
            for b in range(nb):
                parts = [jnp.broadcast_to(bias, (CONV_ROWS, LANES))] + [None] * (CONV_PARTIALS - 1)
                for n, (w8, phase, base) in enumerate(taps):
                    w_rows = jnp.concatenate([w8] * (CONV_ROWS // SUBLANES), axis=0)
                    term = w_rows * win_ref[b, phase, c, pl.ds(r0 + base, CONV_ROWS), :]
                    p = n % CONV_PARTIALS
                    parts[p] = term if parts[p] is None else parts[p] + term
                yc_ref[b, pl.ds(r0, CONV_ROWS), lanes] = functools.reduce(lambda x, y: x + y, parts)
            return carry

        lax.fori_loop(0, tb // CONV_ROWS, conv_tile, 0)

    win_ref[:, :, :, 0:CONV_HALO, :] = win_ref[:, :, :, tb:tb + CONV_HALO, :]

    n_chunks = tb // lc
    row = lax.broadcasted_iota(jnp.int32, (lc, lc), 0)
    col = lax.broadcasted_iota(jnp.int32, (lc, lc), 1)
    tril = (col <= row).astype(BF16)
    klane = lax.broadcasted_iota(jnp.int32, (1, D_GLA_K), 1) // HEAD_K
    srow = lax.broadcasted_iota(jnp.int32, (GLA_HEADS * lc, lc), 0)
    scol = lax.broadcasted_iota(jnp.int32, (GLA_HEADS * lc, lc), 1)
    causal = scol <= (srow % lc)
    rows1 = lax.broadcasted_iota(jnp.int32, (lc, 1), 0)

    decay = None
    for b in range(nb):
        for ci in range(n_chunks):
            hi, mid, lo = _split3(la_ref[b, ci * lc:(ci + 1) * lc, :])
            cum = _dot(tril, hi) + _dot(tril, mid) + _dot(tril, lo)
            cum_ref[b, ci * lc:(ci + 1) * lc, :] = cum
            total = -cum[lc - 1:lc, :]
            decay = total if decay is None else jnp.maximum(decay, total)
    max_decay = jnp.max(decay)

    def chunk_factorised(b, ci, q, k, cum, cum_last):
        ref = cum[lc // 2 - 1:lc // 2, :]
        q_dec = q * jnp.exp(cum)
        k_inc = k * jnp.exp(-cum)
        qt = (q_dec * jnp.exp(-ref)).astype(BF16)
        kt = (k_inc * jnp.exp(ref)).astype(BF16)
        qbd = jnp.concatenate(
            [jnp.where(klane == h, qt, jnp.zeros_like(qt)) for h in range(GLA_HEADS)], axis=0)
        s = _dot_nt(qbd, kt)
        s = jnp.where(causal, s, 0.0).astype(BF16)
        heads = []
        for h in range(GLA_HEADS):
            vh = v_ref[b, ci * lc:(ci + 1) * lc, h * HEAD_V:(h + 1) * HEAD_V].astype(BF16)
            heads.append(_dot(s[h * lc:(h + 1) * lc, :], vh))
        o_intra = jnp.concatenate(heads, axis=-1)
        return o_intra, q_dec.astype(BF16), (k_inc * jnp.exp(cum_last)).astype(BF16)

    def chunk_exact(b, ci, q, k, cum, cum_last):
        e_row = lax.broadcasted_iota(jnp.int32, (D_GLA_K, D_GLA_V), 0) // HEAD_K
        e_col = lax.broadcasted_iota(jnp.int32, (D_GLA_K, D_GLA_V), 1) // HEAD_V
        expand = (e_row == e_col).astype(BF16)

        def key_row(j, o):
            kj = k_ref[b, pl.ds(ci * lc + j, 1), :]
            cj = cum_ref[b, pl.ds(ci * lc + j, 1), :]
            vj = v_ref[b, pl.ds(ci * lc + j, 1), :]
            p = q * kj * jnp.exp(jnp.minimum(cum - cj, 0.0))
            p = jnp.where(rows1 >= j, p, 0.0).astype(BF16)
            return o + _dot(p, expand) * vj

        o_intra = lax.fori_loop(0, lc, key_row, jnp.zeros((lc, D_GLA_V), F32))
        return (o_intra, (q * jnp.exp(cum)).astype(BF16),
                (k * jnp.exp(cum_last - cum)).astype(BF16))

    def gla_tile(chunk_terms):
        for ci in range(n_chunks):
            for b in range(nb):
                rows = slice(ci * lc, (ci + 1) * lc)
                q = q_ref[b, rows, :]
                k = k_ref[b, rows, :]
                cum = cum_ref[b, rows, :]
                cum_last = cum[lc - 1:lc, :]
                o_intra, qh, kh = chunk_terms(b, ci, q, k, cum, cum_last)

                st = st_ref[b]
                o_inter = _dot_nt(qh, st.astype(BF16))
                vb = v_ref[b, rows, :].astype(BF16)
                update = jnp.concatenate(
                    [_dot_tn(vb[:, h * HEAD_V:(h + 1) * HEAD_V],
                             jnp.where(klane == h, kh, jnp.zeros_like(kh)))
                     for h in range(GLA_HEADS)], axis=0)
                st_ref[b] = st * jnp.exp(cum_last) + update

                o = o_inter + o_intra
                normed = []
                for h in range(GLA_HEADS):
                    oh = o[:, h * HEAD_V:(h + 1) * HEAD_V]
                    ms = jnp.mean(oh * oh, axis=-1, keepdims=True)
                    normed.append(oh * lax.rsqrt(ms + EPS) * gng_ref[...])
                y_gla = jnp.concatenate(normed, axis=-1) * gate_ref[b, rows, :]
                yg_ref[b, rows, :] = y_gla.astype(yg_ref.dtype)

    @pl.when(max_decay <= GLA_MAX_FACTORISED_DECAY)
    def _():
        gla_tile(chunk_factorised)

    @pl.when(max_decay > GLA_MAX_FACTORISED_DECAY)
    def _():
        gla_tile(chunk_exact)


def _mixer(u, q, k, la, v, gate, cw, cb, gng):
    b_sz, t_len, _ = u.shape
    tb = TOKENS_MIX
    assert t_len % tb == 0 and tb % GLA_CHUNK == 0 and tb % CONV_ROWS == 0

    def tok(width):
        return pl.BlockSpec((b_sz, tb, width), lambda t: (0, t, 0))

    def const(arr):
        return pl.BlockSpec(arr.shape, lambda t: (0,) * arr.ndim)

    return pl.pallas_call(
        _mixer_kernel,
        grid=(t_len // tb,),
        in_specs=[tok(D_CONV), tok(D_GLA_K), tok(D_GLA_K), tok(D_GLA_K), tok(D_GLA_V),
                  tok(D_GLA_V), const(cw), const(cb), const(gng)],
        out_specs=[tok(D_CONV), tok(D_GLA_V)],
        out_shape=[jax.ShapeDtypeStruct((b_sz, t_len, D_CONV), F32),
                   jax.ShapeDtypeStruct((b_sz, t_len, D_GLA_V), BF16)],
        scratch_shapes=[
            pltpu.VMEM((b_sz, SUBLANES, D_CONV // LANES, CONV_HALO + tb, LANES), F32),
            pltpu.VMEM((b_sz, D_GLA_V, D_GLA_K), F32),
            pltpu.VMEM((b_sz, tb, D_GLA_K), F32),
            pltpu.VMEM((CONV_KERNEL, SUBLANES, D_CONV), F32),
        ],
        compiler_params=pltpu.CompilerParams(
            dimension_semantics=("arbitrary",), vmem_limit_bytes=VMEM_LIMIT_BYTES),
        name="mixer",
    )(u, q, k, la, v, gate, cw, cb, gng)


def _out_ffn_kernel(x_ref, yc_ref, yg_ref, gt1_ref, sc2_ref, sh2_ref, gt2_ref, scf_ref, shf_ref,
                    gffn_ref, gfin_ref, lng_ref, lnb_ref, wout_hbm, wgu_hbm, wo_hbm, o_ref,
                    wout_ref, wgu_ref, wo_ref, sem):
    is_first = (pl.program_id(0) == 0) & (pl.program_id(1) == 0)

    @pl.when(is_first)
    def _():
        pieces = [("out", wout_hbm, wout_ref)]
        for n, (lo, hi) in enumerate(FFN_SLABS):
            up_lo, up_hi = D_FF + lo, D_FF + hi
            pieces += [(("gate", n), wgu_hbm.at[:, lo:hi], wgu_ref.at[:, lo:hi]),
                       (("up", n), wgu_hbm.at[:, up_lo:up_hi], wgu_ref.at[:, up_lo:up_hi]),
                       (("down", n), wo_hbm.at[lo:hi, :], wo_ref.at[lo:hi, :])]
        copies = {name: pltpu.make_async_copy(src, dst, sem.at[i])
                  for i, (name, src, dst) in enumerate(pieces)}
        for copy in copies.values():
            copy.start()
        _out_ffn_body(x_ref, yc_ref, yg_ref, gt1_ref, sc2_ref, sh2_ref, gt2_ref, scf_ref,
                      shf_ref, gffn_ref, gfin_ref, lng_ref, lnb_ref, o_ref,
                      wout_ref, wgu_ref, wo_ref, copies)
        assert not copies

    @pl.when(jnp.logical_not(is_first))
    def _():
        _out_ffn_body(x_ref, yc_ref, yg_ref, gt1_ref, sc2_ref, sh2_ref, gt2_ref, scf_ref,
                      shf_ref, gffn_ref, gfin_ref, lng_ref, lnb_ref, o_ref,
                      wout_ref, wgu_ref, wo_ref, {})


def _out_ffn_body(x_ref, yc_ref, yg_ref, gt1_ref, sc2_ref, sh2_ref, gt2_ref, scf_ref, shf_ref,
                  gffn_ref, gfin_ref, lng_ref, lnb_ref, o_ref, wout_ref, wgu_ref, wo_ref,
                  in_flight):
    tm = x_ref.shape[1]
    groups = [slice(r, r + FFN_ROWS) for r in range(0, tm, FFN_ROWS)]

    def arrived(piece):
        copy = in_flight.pop(piece, None)
        if copy is not None:
            copy.wait()

    def out_proj(rows):
        conv = yc_ref[0, rows, :]
        mu = jnp.mean(conv, axis=-1, keepdims=True)
        cen = conv - mu
        var = jnp.mean(cen * cen, axis=-1, keepdims=True)
        y_conv = _silu(cen * lax.rsqrt(var + EPS) * lng_ref[...] + lnb_ref[...]).astype(BF16)
        mix = (_dot(y_conv, wout_ref[0:D_CONV, :])
               + _dot(yg_ref[0, rows, :], wout_ref[D_CONV:D_CONV + D_GLA_V, :]))
        return x_ref[0, rows, :] + gt1_ref[0] * mix

    def ffn_input(x1):
        ms = jnp.mean(x1 * x1, axis=-1, keepdims=True)
        h = x1 * lax.rsqrt(ms + EPS) * gffn_ref[...]
        return (h * (1.0 + sc2_ref[0]) + sh2_ref[0]).astype(BF16)

    def ffn(x1, hb):
        acc = jnp.zeros_like(x1)
        for n, (lo, hi) in enumerate(FFN_SLABS):
            arrived(("gate", n))
            gate = _dot(hb, wgu_ref[:, lo:hi])
            arrived(("up", n))
            up = _dot(hb, wgu_ref[:, D_FF + lo:D_FF + hi])
            act = (_silu(gate) * up).astype(BF16)
            arrived(("down", n))
            acc = acc + _dot(act, wo_ref[lo:hi, :])
        return x1 + gt2_ref[0] * acc

    def final_norm(rows, x2):
        ms = jnp.mean(x2 * x2, axis=-1, keepdims=True)
        xn = x2 * lax.rsqrt(ms + EPS) * gfin_ref[...]
        o_ref[0, rows, :] = xn * (1.0 + scf_ref[0]) + shf_ref[0]

    arrived("out")
    x1s = [out_proj(rows) for rows in groups]
    pending = None
    for rows, x1 in zip(groups, x1s):
        hb = ffn_input(x1)
        if pending is not None:
            final_norm(*pending)
        pending = (rows, ffn(x1, hb))
    final_norm(*pending)


def _out_ffn(x, yc, yg, mod, gffn, gfin, lng, lnb, wout, wgu, wo):
    b_sz, t_len, d = x.shape
    tm = TOKENS_FFN
    assert t_len % tm == 0 and tm % FFN_ROWS == 0

    def tok(width):
        return pl.BlockSpec((1, tm, width), lambda b, t: (b, t, 0))

    def const(arr):
        return pl.BlockSpec(arr.shape, lambda b, t: (0,) * arr.ndim,
                            pipeline_mode=pl.Buffered(1))

    mods = (MOD_GT1, MOD_SC2, MOD_SH2, MOD_GT2, MOD_SCF, MOD_SHF)
    in_hbm = pl.BlockSpec(memory_space=pl.ANY)
    return pl.pallas_call(
        _out_ffn_kernel,
        grid=(b_sz, t_len // tm),
        in_specs=[tok(d), tok(D_CONV), tok(D_GLA_V)] + [_mod_spec(m) for m in mods]
                 + [const(gffn), const(gfin), const(lng), const(lnb), in_hbm, in_hbm, in_hbm],
        out_specs=tok(d),
        out_shape=jax.ShapeDtypeStruct((b_sz, t_len, d), F32),
        scratch_shapes=[
            pltpu.VMEM(wout.shape, BF16),
            pltpu.VMEM(wgu.shape, BF16),
            pltpu.VMEM(wo.shape, BF16),
            pltpu.SemaphoreType.DMA((1 + 3 * len(FFN_SLABS),)),
        ],
        compiler_params=pltpu.CompilerParams(
            dimension_semantics=("arbitrary", "arbitrary"), vmem_limit_bytes=VMEM_LIMIT_BYTES),
        name="out_ffn",
    )(x, yc, yg, *([mod] * len(mods)), gffn, gfin, lng, lnb, wout, wgu, wo)


def kernel(x, c, w_ada, b_ada, g_mix, w_in, conv_w, conv_b, ln_g, ln_b, w_a2, b_a2,
           gla_norm_g, w_out, g_ffn, w_ffn_in, w_ffn_out, w_ada_final, b_ada_final, g_final):
    b_sz, t_len, d = x.shape
    assert w_ada.shape[0] == 1, "single-layer block"
    assert d == D_MODEL

    c_lanes = jnp.broadcast_to(c[:, :, None], (b_sz, d, LANES))
    mod = _modulation(c_lanes, w_ada[0], b_ada[0], w_ada_final, b_ada_final)

    w = jnp.concatenate(
        [w_in[0].astype(BF16), jnp.zeros((d, LANES - GATE_RANK), BF16)], axis=1)

    u, q, k, v, gate, la, wout, wgu, wo = _in_proj(
        x, mod, g_mix[0].reshape(1, d), w, w_a2, b_a2[0].reshape(1, D_GLA_K),
        (w_out[0], w_ffn_in[0], w_ffn_out[0]))

    y_conv, y_gla = _mixer(u, q, k, la, v, gate, conv_w, conv_b[0].reshape(1, D_CONV),
                           gla_norm_g[0].reshape(1, HEAD_V))

    return _out_ffn(
        x, y_conv, y_gla, mod, g_ffn[0].reshape(1, d), g_final.reshape(1, d),
        ln_g[0].reshape(1, D_CONV), ln_b[0].reshape(1, D_CONV), wout, wgu, wo)
```
